```python
import math
import jax, jax.numpy as jnp
from jax import lax
import numpy as np

D_MODEL = 1024
BATCH = 4
SEQ = 8192
DEPTH = 1
DEC_BATCH = 32
DEC_SEQ = 8
PAST_LEN = 16384
PAGE_SIZE = 128

HEAD_DIM = 64
D_ATT = D_MODEL // 2
D_RET = D_MODEL - D_ATT
N_ATT_HEADS = D_ATT // HEAD_DIM
N_RET_HEADS = D_RET // HEAD_DIM
MOBA_BLOCK = 256
MOBA_TOPK = 3
Q_BLOCK = 64
RET_CHUNK = 128
N_BUCKETS = 32
MAX_DISTANCE = 128
ROPE_BASE = 10000.0
EPS = 1e-6
NEG = -1e30

kernel_name = "hymba_moba_retention_step"


def rms_norm(x, g):
    xf = x.astype(jnp.float32)
    y = xf * lax.rsqrt(jnp.mean(xf * xf, axis=-1, keepdims=True) + EPS)
    return (y * g.astype(jnp.float32)).astype(x.dtype)


def group_norm_heads(o, g):
    mu = jnp.mean(o, axis=-1, keepdims=True)
    var = jnp.mean(jnp.square(o - mu), axis=-1, keepdims=True)
    y = (o - mu) * lax.rsqrt(var + EPS)
    return y.reshape(*o.shape[:-2], -1) * g.astype(jnp.float32)


def rotary(x, pos):
    half = HEAD_DIM // 2
    inv = ROPE_BASE ** (-jnp.arange(half, dtype=jnp.float32) / half)
    ang = pos.astype(jnp.float32)[:, None] * inv[None, :]
    cos = jnp.cos(ang)[None, :, None, :]
    sin = jnp.sin(ang)[None, :, None, :]
    x1 = x[..., :half].astype(jnp.float32)
    x2 = x[..., half:].astype(jnp.float32)
    return jnp.concatenate([x1 * cos - x2 * sin, x1 * sin + x2 * cos], axis=-1).astype(x.dtype)


def t5_bucket(dist):
    max_exact = N_BUCKETS // 2
    d = jnp.maximum(dist, 1).astype(jnp.float32)
    large = max_exact + (jnp.log(d / max_exact) / math.log(MAX_DISTANCE / max_exact)
                         * (N_BUCKETS - max_exact)).astype(jnp.int32)
    large = jnp.minimum(large, N_BUCKETS - 1)
    return jnp.where(dist < max_exact, dist, large)


def project(x, pos, ln_g, w_in, q_norm_g, k_norm_g):
    B, S, _ = x.shape
    h = rms_norm(x, ln_g)
    proj = jnp.einsum('bsd,de->bse', h, w_in)
    cuts = [D_ATT, 2 * D_ATT, 3 * D_ATT, 4 * D_ATT,
            4 * D_ATT + D_RET, 4 * D_ATT + 2 * D_RET, 4 * D_ATT + 3 * D_RET]
    qa, ka, va, ga, qr, kr, vr, gr = jnp.split(proj, cuts, axis=-1)
    qa = rms_norm(qa.reshape(B, S, N_ATT_HEADS, HEAD_DIM), q_norm_g)
    ka = rms_norm(ka.reshape(B, S, N_ATT_HEADS, HEAD_DIM), k_norm_g)
    va = va.reshape(B, S, N_ATT_HEADS, HEAD_DIM)
    qr = rotary(qr.reshape(B, S, N_RET_HEADS, HEAD_DIM), pos)
    kr = rotary(kr.reshape(B, S, N_RET_HEADS, HEAD_DIM), pos) * HEAD_DIM ** -0.5
    vr = vr.reshape(B, S, N_RET_HEADS, HEAD_DIM)
    return qa, ka, va, ga, qr, kr, vr, gr


def moba_blocks(k, v):
    B, T, H, D = k.shape
    nb = -(-T // MOBA_BLOCK)
    pad = nb * MOBA_BLOCK - T
    kp = jnp.pad(k, ((0, 0), (0, pad), (0, 0), (0, 0))).reshape(B, nb, MOBA_BLOCK, H, D)
    vp = jnp.pad(v, ((0, 0), (0, pad), (0, 0), (0, 0))).reshape(B, nb, MOBA_BLOCK, H, D)
    km = jnp.mean(kp.astype(jnp.float32), axis=2)
    return kp, vp, km


def moba_attend(q, qpos, kb, vb, km, rel_bias):
    B, Q, H, D = q.shape
    nb = kb.shape[1]
    qb = qpos // MOBA_BLOCK
    gate = jnp.einsum('bqhd,bnhd->bhqn', q.astype(jnp.float32), km)
    past_ok = jnp.arange(nb)[None, :] < qb[:, None]
    gate = jnp.where(past_ok[None, None], gate, NEG)
    k_top = min(MOBA_TOPK, nb)
    _, top_idx = lax.top_k(gate, k_top)
    top_ok = top_idx < qb[None, None, :, None]
    own = jnp.broadcast_to(qb[None, None, :, None], (B, H, Q, 1))
    sel = jnp.concatenate([top_idx, own], axis=-1)
    slot_ok = jnp.concatenate([top_ok, jnp.ones((B, H, Q, 1), dtype=bool)], axis=-1)
    bi = jnp.arange(B)[:, None, None, None]
    hi = jnp.arange(H)[None, :, None, None]
    kg = kb[bi, sel, :, hi]
    vg = vb[bi, sel, :, hi]
    kpos = sel[..., None] * MOBA_BLOCK + jnp.arange(MOBA_BLOCK)
    dist = qpos[None, None, :, None, None] - kpos
    ok = slot_ok[..., None] & (dist >= 0)
    bias = rel_bias.astype(jnp.float32)[t5_bucket(jnp.maximum(dist, 0)), hi[..., None]]
    logits = jnp.einsum('bqhd,bhqskd->bhqsk', q, kg,
                        preferred_element_type=jnp.float32) * (D ** -0.5) + bias
    logits = jnp.where(ok, logits, NEG)
    p = jax.nn.softmax(logits.reshape(B, H, Q, -1), axis=-1).reshape(logits.shape).astype(vg.dtype)
    return jnp.einsum('bhqsk,bhqskd->bqhd', p, vg)


def moba_prompt(q, k, v, rel_bias):
    B, S, H, D = q.shape
    kb, vb, km = moba_blocks(k, v)
    nq = S // Q_BLOCK
    qs = q.reshape(B, nq, Q_BLOCK, H, D).transpose(1, 0, 2, 3, 4)
    pos = jnp.arange(S, dtype=jnp.int32).reshape(nq, Q_BLOCK)
    out = lax.map(lambda a: moba_attend(a[0], a[1], kb, vb, km, rel_bias), (qs, pos))
    return out.transpose(1, 0, 2, 3, 4).reshape(B, S, H, D)


def retention_chunk(R, q, k, v, log_g):
    L = q.shape[1]
    n = jnp.arange(L, dtype=jnp.float32)
    diff = n[:, None] - n[None, :]
    decay = jnp.where(diff >= 0, jnp.exp(log_g[:, None, None] * jnp.maximum(diff, 0.0)), 0.0)
    qf, kf, vf = q.astype(jnp.float32), k.astype(jnp.float32), v.astype(jnp.float32)
    s = jnp.einsum('blhd,bmhd->bhlm', qf, kf) * decay[None]
    inner = jnp.einsum('bhlm,bmhd->blhd', s, vf)
    q_decay = jnp.exp(log_g[None, :] * (n[:, None] + 1.0))
    cross = jnp.einsum('blhd,bhde->blhe', qf, R) * q_decay[None, :, :, None]
    k_decay = jnp.exp(log_g[None, :] * (L - 1.0 - n)[:, None])
    R_new = jnp.exp(log_g * L)[None, :, None, None] * R + \
        jnp.einsum('blhd,blhe->bhde', kf * k_decay[None, :, :, None], vf)
    return R_new, inner + cross


def retention_prompt(q, k, v, log_g):
    B, S, H, D = q.shape
    nc = S // RET_CHUNK

    def to_chunks(t):
        return t.reshape(B, nc, RET_CHUNK, H, D).transpose(1, 0, 2, 3, 4)

    def step(R, qkv):
        return retention_chunk(R, qkv[0], qkv[1], qkv[2], log_g)

    R0 = jnp.zeros((B, H, D, D), jnp.float32)
    R, o = lax.scan(step, R0, (to_chunks(q), to_chunks(k), to_chunks(v)))
    return o.transpose(1, 0, 2, 3, 4).reshape(B, S, H, D), R


def merge(x, o_att, ga, o_ret, gr, ret_norm_g, w_out):
    B, S, _ = x.shape
    att = o_att.reshape(B, S, D_ATT) * jax.nn.silu(ga)
    ret = group_norm_heads(o_ret, ret_norm_g).astype(x.dtype) * jax.nn.silu(gr)
    return x + jnp.einsum('bse,ed->bsd', jnp.concatenate([att, ret], axis=-1), w_out)


def setup_inputs(seed: int = 0) -> dict:
    key = jax.random.key(seed)
    ks = jax.random.split(key, 16)
    f32 = jnp.float32
    n_pages = PAST_LEN // PAGE_SIZE
    n_used = DEC_BATCH * n_pages
    n_phys = n_used + n_used // 4
    d_in = 4 * D_ATT + 4 * D_RET
    x_prompt = jax.random.normal(ks[0], (BATCH, SEQ, D_MODEL), f32)
    x_sample = jax.random.normal(ks[1], (DEC_BATCH, DEC_SEQ, D_MODEL), f32)
    cache_k = jax.random.normal(ks[2], (DEPTH, n_phys, PAGE_SIZE, N_ATT_HEADS, HEAD_DIM), f32)
    cache_v = jax.random.normal(ks[3], (DEPTH, n_phys, PAGE_SIZE, N_ATT_HEADS, HEAD_DIM), f32)
    state_ret = 0.5 * jax.random.normal(ks[4], (DEPTH, DEC_BATCH, N_RET_HEADS, HEAD_DIM, HEAD_DIM), f32)
    page_table = jax.random.permutation(ks[5], n_phys)[:n_used].reshape(DEC_BATCH, n_pages).astype(jnp.int32)
    ln_g = 1.0 + 0.02 * jax.random.normal(ks[6], (DEPTH, D_MODEL), f32)
    w_in = jax.random.normal(ks[7], (DEPTH, D_MODEL, d_in), f32) * D_MODEL ** -0.5
    q_norm_g = 1.0 + 0.02 * jax.random.normal(ks[8], (DEPTH, HEAD_DIM), f32)
    k_norm_g = 1.0 + 0.02 * jax.random.normal(ks[9], (DEPTH, HEAD_DIM), f32)
    rel_bias = 0.5 * jax.random.normal(ks[10], (N_BUCKETS, N_ATT_HEADS), f32)
    ret_norm_g = 1.0 + 0.02 * jax.random.normal(ks[11], (DEPTH, D_RET), f32)
    w_out = jax.random.normal(ks[12], (DEPTH, D_ATT + D_RET, D_MODEL), f32) * (D_ATT + D_RET) ** -0.5
    return {"x_prompt": x_prompt, "x_sample": x_sample, "cache_k": cache_k, "cache_v": cache_v,
            "state_ret": state_ret, "page_table": page_table, "ln_g": ln_g, "w_in": w_in,
            "q_norm_g": q_norm_g, "k_norm_g": k_norm_g, "rel_bias": rel_bias,
            "ret_norm_g": ret_norm_g, "w_out": w_out}


def reference(x_prompt, x_sample, cache_k, cache_v, state_ret, page_table, ln_g, w_in,
              q_norm_g, k_norm_g, rel_bias, ret_norm_g, w_out):
    pos_p = jnp.arange(SEQ, dtype=jnp.int32)
    pos_s = PAST_LEN + jnp.arange(DEC_SEQ, dtype=jnp.int32)
    log_g = jnp.log1p(-jnp.exp2(-5.0 - jnp.arange(N_RET_HEADS, dtype=jnp.float32)))
    hp, hs = x_prompt, x_sample
    kp_l, vp_l, rp_l, ks_l, vs_l, rs_l = [], [], [], [], [], []
    for l in range(DEPTH):
        qa, ka, va, ga, qr, kr, vr, gr = project(hp, pos_p, ln_g[l], w_in[l], q_norm_g[l], k_norm_g[l])
        o_att = moba_prompt(qa, ka, va, rel_bias)
        o_ret, r_p = retention_prompt(qr, kr, vr, log_g)
        hp = merge(hp, o_att, ga, o_ret, gr, ret_norm_g[l], w_out[l])
        kp_l.append(ka)
        vp_l.append(va)
        rp_l.append(r_p.astype(state_ret.dtype))
        qa, ka, va, ga, qr, kr, vr, gr = project(hs, pos_s, ln_g[l], w_in[l], q_norm_g[l], k_norm_g[l])
        past_k = cache_k[l][page_table].reshape(DEC_BATCH, -1, N_ATT_HEADS, HEAD_DIM)
        past_v = cache_v[l][page_table].reshape(DEC_BATCH, -1, N_ATT_HEADS, HEAD_DIM)
        kb, vb, km = moba_blocks(jnp.concatenate([past_k, ka], axis=1),
                                 jnp.concatenate([past_v, va], axis=1))
        o_att_s = moba_attend(qa, pos_s, kb, vb, km, rel_bias)
        r_s, o_ret_s = retention_chunk(state_ret[l].astype(jnp.float32), qr, kr, vr, log_g)
        hs = merge(hs, o_att_s, ga, o_ret_s, gr, ret_norm_g[l], w_out[l])
        ks_l.append(ka)
        vs_l.append(va)
        rs_l.append(r_s.astype(state_ret.dtype))
    k_prompt_new = jnp.stack(kp_l)
    v_prompt_new = jnp.stack(vp_l)
    ret_state_prompt = jnp.stack(rp_l)
    k_sample_new = jnp.stack(ks_l)
    v_sample_new = jnp.stack(vs_l)
    ret_state_sample = jnp.stack(rs_l)
    return (hp, hs, k_prompt_new, v_prompt_new, ret_state_prompt, k_sample_new, v_sample_new, ret_state_sample)
```

```python
import functools
import math

import jax
import jax.numpy as jnp
from jax import lax
from jax.experimental import pallas as pl
from jax.experimental.pallas import tpu as pltpu

D_MODEL = 1024
HEAD_DIM = 64
N_HEADS = 8
D_GRP = N_HEADS * HEAD_DIM
LANES = 128
N_PAIRS = D_GRP // LANES
MOBA_BLOCK = 256
MOBA_TOPK = 3
RET_CHUNK = 128
PAGE_SIZE = 128
N_BUCKETS = 32
MAX_DISTANCE = 128
ROPE_BASE = 10000.0
EPS = 1e-6
NEG = -1e30
BIG = 1e30
LOWEST = -3e38
VMEM_LIMIT = 56 * 1024 * 1024

F32 = jnp.float32
BF16 = jnp.bfloat16
NT_DIMS = (((1,), (1,)), ((), ()))


def _silu(g):
    return g * (1.0 / (1.0 + jnp.exp(-g)))


def _head_mean(z):
    lane = lax.broadcasted_iota(jnp.int32, z.shape, 1)
    lo = lane < HEAD_DIM
    s0 = jnp.sum(jnp.where(lo, z, 0.0), axis=-1, keepdims=True)
    s1 = jnp.sum(jnp.where(lo, 0.0, z), axis=-1, keepdims=True)
    return jnp.where(lo, s0, s1) * (1.0 / HEAD_DIM)


def _top_k_blocks(gate, n_valid, blk, axis):
    n_blk = gate.shape[axis]
    g = jnp.where(blk < n_valid, gate, NEG)
    sel = jnp.zeros(gate.shape, F32)
    for _ in range(MOBA_TOPK):
        mx = jnp.max(g, axis=axis, keepdims=True)
        first = jnp.min(jnp.where(g == mx, blk, float(n_blk)), axis=axis, keepdims=True)
        hit = blk == first
        sel = jnp.where(hit, 1.0, sel)
        g = jnp.where(hit, LOWEST, g)
    return jnp.where(blk < n_valid, sel, 0.0)


def _proj_kernel(x_ref, cos_ref, sin_ref, lng_ref, w_ref, qg_ref, kg_ref,
                 qpm_ref, kpm_ref, k_ref, v_ref, vt_ref, km_ref, ga_ref, gr_ref,
                 qr_ref, kr_ref, vr_ref, *, feature_major):
    x = x_ref[...]
    ms = jnp.mean(x * x, axis=-1, keepdims=True)
    h = (x * lax.rsqrt(ms + EPS) * lng_ref[...]).astype(BF16)

    def seg(s):
        return jnp.dot(h, w_ref[:, s * D_GRP:(s + 1) * D_GRP], preferred_element_type=F32)

    def chunks(z):
        return [z[:, c * LANES:(c + 1) * LANES] for c in range(N_PAIRS)]

    qa = seg(0)
    for c, z in enumerate(chunks(qa)):
        g = qg_ref[:, c * LANES:(c + 1) * LANES]
        zn = z * lax.rsqrt(_head_mean(z * z) + EPS) * g
        qpm_ref[c] = (zn * (HEAD_DIM ** -0.5)).astype(BF16)
    ka = seg(1)
    kn = []
    for c, z in enumerate(chunks(ka)):
        g = kg_ref[:, c * LANES:(c + 1) * LANES]
        zn = z * lax.rsqrt(_head_mean(z * z) + EPS) * g
        kpm_ref[c] = zn.astype(BF16)
        kn.append(zn)
    kn = jnp.concatenate(kn, axis=1)
    for b in range(km_ref.shape[1]):
        km_ref[0, b:b + 1, :] = jnp.mean(kn[b * MOBA_BLOCK:(b + 1) * MOBA_BLOCK], axis=0, keepdims=True)
    va = seg(2)
    vat = va.T
    for b in range(vt_ref.shape[0]):
        vt_ref[b] = vat[:, b * MOBA_BLOCK:(b + 1) * MOBA_BLOCK].astype(BF16)
    if feature_major:
        k_ref[0] = kn.T
        v_ref[0] = vat
    else:
        k_ref[...] = kn
        v_ref[...] = va
    ga_ref[...] = seg(3)

    cos = cos_ref[...]
    sin = sin_ref[...]
    lane = lax.broadcasted_iota(jnp.int32, cos.shape, 1)
    first_half = (lane % HEAD_DIM) < (HEAD_DIM // 2)

    def rope(z):
        swapped = jnp.where(first_half,
                            pltpu.roll(z, LANES - HEAD_DIM // 2, 1),
                            pltpu.roll(z, HEAD_DIM // 2, 1))
        return z * cos + swapped * sin

    qr = seg(4)
    for c, z in enumerate(chunks(qr)):
        qr_ref[:, c * LANES:(c + 1) * LANES] = rope(z)
    kr = seg(5)
    for c, z in enumerate(chunks(kr)):
        kr_ref[:, c * LANES:(c + 1) * LANES] = rope(z) * (HEAD_DIM ** -0.5)
    vr_ref[...] = seg(6)
    gr_ref[...] = seg(7)


def _proj(x, cos, sin, ln_g, w_bf, qg, kg, *, tm, seq_len, feature_major):
    R = x.shape[0]
    assert tm % MOBA_BLOCK == 0 and R % tm == 0 and cos.shape[0] % tm == 0 and seq_len % tm == 0
    n_pos = cos.shape[0] // tm
    nb = tm // MOBA_BLOCK
    nt = seq_len // tm
    row = lambda i: (i, 0)
    full = lambda i: (0, 0)
    wide = pl.BlockSpec((tm, D_GRP), row)
    wide_shape = jax.ShapeDtypeStruct((R, D_GRP), F32)
    if feature_major:
        kv_shape = jax.ShapeDtypeStruct((R // seq_len, D_GRP, seq_len), F32)
        kv_spec = pl.BlockSpec((1, D_GRP, tm), lambda i: (i // nt, 0, i % nt))
    else:
        kv_shape, kv_spec = wide_shape, wide
    out_shape = (
        jax.ShapeDtypeStruct((N_PAIRS, R, LANES), BF16),
        jax.ShapeDtypeStruct((N_PAIRS, R, LANES), BF16),
        kv_shape,
        kv_shape,
        jax.ShapeDtypeStruct((R // MOBA_BLOCK, D_GRP, MOBA_BLOCK), BF16),
        jax.ShapeDtypeStruct((R // tm, nb, D_GRP), F32),
        wide_shape,
        wide_shape,
        wide_shape,
        wide_shape,
        wide_shape,
    )
    pm = pl.BlockSpec((N_PAIRS, tm, LANES), lambda i: (0, i, 0))
    out_specs = (
        pm, pm, kv_spec, kv_spec,
        pl.BlockSpec((nb, D_GRP, MOBA_BLOCK), lambda i: (i, 0, 0)),
        pl.BlockSpec((1, nb, D_GRP), lambda i: (i, 0, 0)),
        wide, wide, wide, wide, wide,
    )
    in_specs = [
        pl.BlockSpec((tm, D_MODEL), row),
        pl.BlockSpec((tm, LANES), lambda i: (i % n_pos, 0)),
        pl.BlockSpec((tm, LANES), lambda i: (i % n_pos, 0)),
        pl.BlockSpec((1, D_MODEL), full),
        pl.BlockSpec((D_MODEL, 8 * D_GRP), full),
        pl.BlockSpec((1, D_GRP), full),
        pl.BlockSpec((1, D_GRP), full),
    ]
    return pl.pallas_call(
        functools.partial(_proj_kernel, feature_major=feature_major),
        out_shape=out_shape,
        grid=(R // tm,),
        in_specs=in_specs,
        out_specs=out_specs,
        compiler_params=pltpu.CompilerParams(
            dimension_semantics=("arbitrary",), vmem_limit_bytes=VMEM_LIMIT),
        name="proj",
    )(x, cos, sin, ln_g, w_bf, qg, kg)


def _ret_kernel(q_ref, k_ref, v_ref, g_ref, r0_ref, d2_ref, qd_ref, kd_ref, gl_ref, gng_ref,
                out_ref, rfin_ref, r_scr, *, T, L):
    t = pl.program_id(2)

    @pl.when(t == 0)
    def _():
        r_scr[...] = r0_ref[0, 0]

    lane = lax.broadcasted_iota(jnp.int32, (L, LANES), 1)
    lo = lane < HEAD_DIM
    ri = lax.broadcasted_iota(jnp.int32, (LANES, LANES), 0)
    ci = lax.broadcasted_iota(jnp.int32, (LANES, LANES), 1)
    same_head = (ri < HEAD_DIM) == (ci < HEAD_DIM)
    rows_c = min(T, L)

    def load(ref, rows):
        z = ref[rows, :]
        if rows_c < L:
            z = jnp.concatenate([z, jnp.zeros((L - rows_c, LANES), F32)], axis=0)
        return z

    def chunk(ci_, carry):
        if T > L:
            rows = pl.ds(pl.multiple_of(ci_ * L, L), L)
        else:
            rows = pl.ds(0, rows_c)
        q = load(q_ref, rows)
        k = load(k_ref, rows)
        v = load(v_ref, rows)
        gate = load(g_ref, rows)
        R = r_scr[...]
        vc = v.astype(BF16)
        qs = jnp.concatenate([jnp.where(lo, q, 0.0), jnp.where(lo, 0.0, q)], axis=0).astype(BF16)
        s = lax.dot_general(qs, k.astype(BF16), NT_DIMS, preferred_element_type=F32) * d2_ref[0]
        inner2 = jnp.dot(s.astype(BF16), vc, preferred_element_type=F32)
        inner = jnp.where(lo, inner2[:L], inner2[L:])
        cross = jnp.dot(q.astype(BF16), R.astype(BF16), preferred_element_type=F32) * qd_ref[0]
        o = inner + cross
        kdk = (k * kd_ref[0]).T.astype(BF16)
        upd = jnp.dot(kdk, vc, preferred_element_type=F32)
        r_scr[...] = gl_ref[0] * R + jnp.where(same_head, upd, 0.0)
        mu = _head_mean(o)
        d = o - mu
        var = _head_mean(d * d)
        y = d * lax.rsqrt(var + EPS) * gng_ref[...]
        res = (y * _silu(gate)).astype(out_ref.dtype)
        out_ref[rows, :] = res[:rows_c]
        return carry

    lax.fori_loop(0, max(T // L, 1), chunk, 0)

    @pl.when(t == pl.num_programs(2) - 1)
    def _():
        rfin_ref[0, 0] = r_scr[...]


def _retention(q, k, v, g, r0, tabs, gn_g, *, B, S, T, out_dtype):
    d2, qd, kd, gl = tabs
    L = RET_CHUNK
    nt = S // T
    tok = pl.BlockSpec((T, LANES), lambda b, c, t: (b * nt + t, c))
    tab = lambda shape: pl.BlockSpec((1,) + shape, lambda b, c, t: (c, 0, 0))
    state = pl.BlockSpec((1, 1, LANES, LANES), lambda b, c, t: (b, c, 0, 0))
    return pl.pallas_call(
        functools.partial(_ret_kernel, T=T, L=L),
        out_shape=(jax.ShapeDtypeStruct((B * S, D_GRP), out_dtype),
                   jax.ShapeDtypeStruct((B, N_PAIRS, LANES, LANES), F32)),
        grid=(B, N_PAIRS, nt),
        in_specs=[tok, tok, tok, tok, state,
                  tab((2 * L, L)), tab((L, LANES)), tab((L, LANES)), tab((1, LANES)),
                  pl.BlockSpec((1, LANES), lambda b, c, t: (0, c))],
        out_specs=(tok, state),
        scratch_shapes=[pltpu.VMEM((LANES, LANES), F32)],
        compiler_params=pltpu.CompilerParams(
            dimension_semantics=("arbitrary", "arbitrary", "arbitrary"), vmem_limit_bytes=VMEM_LIMIT),
        name="retention",
    )(q, k, v, g, r0, d2, qd, kd, gl, gn_g)


def _ret_tables(log_g, l_true):
    L = RET_CHUNK
    n = jnp.arange(L, dtype=F32)
    diff = n[:, None] - n[None, :]
    decay = jnp.where(diff >= 0, jnp.exp(log_g[:, None, None] * jnp.maximum(diff, 0.0)), 0.0)
    d2 = decay.reshape(N_PAIRS, 2 * L, L)

    def per_lane(t):
        return jnp.repeat(t, HEAD_DIM, axis=1).reshape(t.shape[0], N_PAIRS, LANES).transpose(1, 0, 2)

    qd = per_lane(jnp.exp(log_g[None, :] * (n[:, None] + 1.0)))
    kd = per_lane(jnp.exp(log_g[None, :] * (l_true - 1.0 - n)[:, None]))
    gl = per_lane(jnp.exp(log_g * l_true)[None, :])
    return d2, qd, kd, gl


def _moba_kernel(q_ref, k_ref, vt_ref, km_ref, bo_ref, bp_ref, ga_ref, out_ref,
                 acc_ref, m_ref, l_ref, sel_ref, ot_ref, *, NI):
    i = pl.program_id(1)
    lane = lax.broadcasted_iota(jnp.int32, (MOBA_BLOCK, LANES), 1)
    blk = lax.broadcasted_iota(jnp.int32, (NI, MOBA_BLOCK), 0).astype(F32)
    i_f = i.astype(F32)

    def head(h, carry):
        c = h // 2
        lo_edge = (h % 2) * HEAD_DIM
        in_head = (lane >= lo_edge) & (lane < lo_edge + HEAD_DIM)
        qm = jnp.where(in_head, q_ref[c].astype(F32), 0.0).astype(BF16)

        gate = lax.dot_general(km_ref[0, c], qm, NT_DIMS, preferred_element_type=F32)
        sel_ref[...] = _top_k_blocks(gate, i_f, blk, 0)

        def scores(j):
            kj = k_ref[c, pl.ds(pl.multiple_of(j * MOBA_BLOCK, MOBA_BLOCK), MOBA_BLOCK), :]
            return lax.dot_general(kj, qm, NT_DIMS, preferred_element_type=F32)

        def values(j):
            return vt_ref[j, pl.ds(pl.multiple_of(h * HEAD_DIM, HEAD_DIM), HEAD_DIM), :]

        s = scores(i) + bo_ref[h]
        m = jnp.max(s, axis=0, keepdims=True)
        p = jnp.exp(s - m)
        m_ref[...] = m
        l_ref[...] = jnp.sum(p, axis=0, keepdims=True)
        acc_ref[...] = jnp.dot(values(i), p.astype(BF16), preferred_element_type=F32)

        def past(j, bias):
            s = scores(j)
            if bias is not None:
                s = s + bias
            selj = sel_ref[pl.ds(j, 1), :] > 0.0
            m_old = m_ref[...]
            m_new = jnp.where(selj, jnp.maximum(m_old, jnp.max(s, axis=0, keepdims=True)), m_old)
            alpha = jnp.exp(m_old - m_new)
            p = jnp.exp(s - jnp.where(selj, m_new, BIG))
            l_ref[...] = alpha * l_ref[...] + jnp.sum(p, axis=0, keepdims=True)
            acc_ref[...] = alpha * acc_ref[...] + jnp.dot(values(j), p.astype(BF16),
                                                          preferred_element_type=F32)
            m_ref[...] = m_new

        @pl.when(i >= 1)
        def _():
            past(i - 1, bp_ref[h])

        def far(j, carry2):
            past(j, None)
            return carry2

        lax.fori_loop(0, jnp.maximum(i - 1, 0), far, 0)
        ot_ref[pl.ds(pl.multiple_of(h * HEAD_DIM, HEAD_DIM), HEAD_DIM), :] = acc_ref[...] / l_ref[...]
        return carry

    lax.fori_loop(0, N_HEADS, head, 0)
    out_ref[...] = (ot_ref[...].T * _silu(ga_ref[...])).astype(out_ref.dtype)


def _moba_prompt(qpm, kpm, vt, km_pm, bias_own, bias_prev, ga, *, B, S):
    NI = S // MOBA_BLOCK
    const3 = lambda b, i: (0, 0, 0)
    return pl.pallas_call(
        functools.partial(_moba_kernel, NI=NI),
        out_shape=jax.ShapeDtypeStruct((B * S, D_GRP), BF16),
        grid=(B, NI),
        in_specs=[
            pl.BlockSpec((N_PAIRS, MOBA_BLOCK, LANES), lambda b, i: (0, b * NI + i, 0)),
            pl.BlockSpec((N_PAIRS, S, LANES), lambda b, i: (0, b, 0)),
            pl.BlockSpec((NI, D_GRP, MOBA_BLOCK), lambda b, i: (b, 0, 0)),
            pl.BlockSpec((1, N_PAIRS, NI, LANES), lambda b, i: (b, 0, 0, 0)),
            pl.BlockSpec((N_HEADS, MOBA_BLOCK, MOBA_BLOCK), const3),
            pl.BlockSpec((N_HEADS, MOBA_BLOCK, MOBA_BLOCK), const3),
            pl.BlockSpec((MOBA_BLOCK, D_GRP), lambda b, i: (b * NI + i, 0)),
        ],
        out_specs=pl.BlockSpec((MOBA_BLOCK, D_GRP), lambda b, i: (b * NI + i, 0)),
        scratch_shapes=[
            pltpu.VMEM((HEAD_DIM, MOBA_BLOCK), F32),
            pltpu.VMEM((1, MOBA_BLOCK), F32),
            pltpu.VMEM((1, MOBA_BLOCK), F32),
            pltpu.VMEM((NI, MOBA_BLOCK), F32),
            pltpu.VMEM((D_GRP, MOBA_BLOCK), F32),
        ],
        compiler_params=pltpu.CompilerParams(
            dimension_semantics=("arbitrary", "arbitrary"), vmem_limit_bytes=VMEM_LIMIT),
        name="moba_prompt",
    )(qpm, kpm, vt, km_pm, bias_own, bias_prev, ga)


def _sample_kernel(pt_ref, *refs, NP, NS, NB, TQ):
    del pt_ref
    kp = refs[:NP]
    vp = refs[NP:2 * NP]
    (q_ref, kn_ref, vn_ref, ga_ref, blast_ref, bnew_ref, out_ref,
     s_ref, kmt_ref, bmax_ref, meff_ref, m_ref, l_ref, acc_ref, qbd_ref) = refs[2 * NP:]
    s = pl.program_id(1)
    HQ = N_HEADS * TQ
    bps = NP * PAGE_SIZE // MOBA_BLOCK
    lane_f = lax.broadcasted_iota(jnp.int32, (D_GRP, LANES), 1)
    lane_q = lax.broadcasted_iota(jnp.int32, (HQ, LANES), 1)

    @pl.when(s == 0)
    def _():
        qt = jnp.concatenate([q_ref[0]] * N_HEADS, axis=0)
        r = lax.broadcasted_iota(jnp.int32, (HQ, D_GRP), 0)
        l = lax.broadcasted_iota(jnp.int32, (HQ, D_GRP), 1)
        qbd_ref[...] = jnp.where(r // TQ == l // HEAD_DIM, qt, 0.0).astype(BF16)
        kmt_ref[...] = jnp.zeros(kmt_ref.shape, F32)
        bmax_ref[...] = jnp.zeros(bmax_ref.shape, F32)

    def block(pages, u):
        return jnp.concatenate([pages[2 * u][0], pages[2 * u + 1][0]], axis=1)

    @pl.when(s < NS)
    def _():
        for u in range(bps):
            jb = s * bps + u
            kt = block(kp, u)
            km = jnp.sum(kt, axis=1, keepdims=True) * (1.0 / MOBA_BLOCK)
            kmt_ref[...] = jnp.where(lane_f == jb, km, kmt_ref[...])
            sc = jnp.dot(qbd_ref[...], kt.astype(BF16), preferred_element_type=F32)
            if u == bps - 1:
                sc = sc + jnp.where(s == NS - 1, blast_ref[...], 0.0)
            s_ref[jb] = sc
            bmax_ref[...] = jnp.where(lane_q == jb, jnp.max(sc, axis=1, keepdims=True), bmax_ref[...])

    @pl.when(s == NS - 1)
    def _():
        gate = jnp.dot(qbd_ref[...], kmt_ref[...].astype(BF16), preferred_element_type=F32)
        sel = _top_k_blocks(gate, float(NB), lane_q.astype(F32), 1)
        pad = jnp.zeros((LANES - TQ, D_GRP), F32)
        kn = jnp.concatenate([kn_ref[0], pad], axis=0).astype(BF16)
        vn = jnp.concatenate([vn_ref[0], pad], axis=0).astype(BF16)
        sn = lax.dot_general(qbd_ref[...], kn, NT_DIMS, preferred_element_type=F32) + bnew_ref[...]
        m = jnp.maximum(jnp.max(jnp.where(sel > 0.0, bmax_ref[...], NEG), axis=1, keepdims=True),
                        jnp.max(sn, axis=1, keepdims=True))
        m_ref[...] = m
        meff_ref[...] = jnp.where(sel > 0.0, m, BIG)
        pn = jnp.exp(sn - m)
        l_ref[...] = jnp.sum(pn, axis=1, keepdims=True)
        acc_ref[...] = jnp.dot(pn.astype(BF16), vn, preferred_element_type=F32)

    @pl.when(s >= NS)
    def _():
        for u in range(bps):
            jb = (s - NS) * bps + u
            vt = block(vp, u).astype(BF16)
            meff = jnp.sum(jnp.where(lane_q == jb, meff_ref[...], 0.0), axis=1, keepdims=True)
            p = jnp.exp(s_ref[jb] - meff)
            l_ref[...] += jnp.sum(p, axis=1, keepdims=True)
            acc_ref[...] += lax.dot_general(p.astype(BF16), vt, NT_DIMS, preferred_element_type=F32)

    @pl.when(s == 2 * NS - 1)
    def _():
        o_bd = acc_ref[...] / l_ref[...]
        lane = lax.broadcasted_iota(jnp.int32, (TQ, D_GRP), 1)
        o = jnp.zeros((TQ, D_GRP), F32)
        for h in range(N_HEADS):
            o = jnp.where(lane // HEAD_DIM == h, o_bd[h * TQ:(h + 1) * TQ], o)
        out_ref[0] = o * _silu(ga_ref[0])


def _sample_attention(page_table, cache_kt, cache_vt, q, k_new, v_new, ga, bias_last, bias_new, *, NP):
    B, n_pages = page_table.shape
    TQ = q.shape[1]
    HQ = N_HEADS * TQ
    NS = n_pages // NP
    NB = n_pages * PAGE_SIZE // MOBA_BLOCK
    assert NB <= LANES and HQ <= LANES and TQ <= LANES
    page = (1, D_GRP, PAGE_SIZE)

    def k_map(t):
        return lambda b, s, pt: (pt[b, NP * jnp.minimum(s, NS - 1) + t], 0, 0)

    def v_map(t):
        return lambda b, s, pt: (pt[b, NP * jnp.maximum(s - NS, 0) + t], 0, 0)

    per_b = pl.BlockSpec((1, TQ, D_GRP), lambda b, s, pt: (b, 0, 0))
    in_specs = ([pl.BlockSpec(page, k_map(t)) for t in range(NP)]
                + [pl.BlockSpec(page, v_map(t)) for t in range(NP)]
                + [per_b, per_b, per_b, per_b,
                   pl.BlockSpec((HQ, MOBA_BLOCK), lambda b, s, pt: (0, 0)),
                   pl.BlockSpec((HQ, LANES), lambda b, s, pt: (0, 0))])
    grid_spec = pltpu.PrefetchScalarGridSpec(
        num_scalar_prefetch=1,
        grid=(B, 2 * NS),
        in_specs=in_specs,
        out_specs=per_b,
        scratch_shapes=[
            pltpu.VMEM((NB, HQ, MOBA_BLOCK), F32),
            pltpu.VMEM((D_GRP, LANES), F32),
            pltpu.VMEM((HQ, LANES), F32),
            pltpu.VMEM((HQ, LANES), F32),
            pltpu.VMEM((HQ, 1), F32),
            pltpu.VMEM((HQ, 1), F32),
            pltpu.VMEM((HQ, D_GRP), F32),
            pltpu.VMEM((HQ, D_GRP), BF16),
        ],
    )
    return pl.pallas_call(
        functools.partial(_sample_kernel, NP=NP, NS=NS, NB=NB, TQ=TQ),
        out_shape=jax.ShapeDtypeStruct((B, TQ, D_GRP), F32),
        grid_spec=grid_spec,
        compiler_params=pltpu.CompilerParams(
            dimension_semantics=("arbitrary", "arbitrary"), vmem_limit_bytes=VMEM_LIMIT),
        name="sample_attention",
    )(page_table, *([cache_kt] * NP), *([cache_vt] * NP), q, k_new, v_new, ga, bias_last, bias_new)


def _out_kernel(x_ref, a_ref, r_ref, w_ref, y_ref):
    cat = jnp.concatenate([a_ref[...].astype(BF16), r_ref[...].astype(BF16)], axis=1)
    y_ref[...] = x_ref[...] + jnp.dot(cat, w_ref[...], preferred_element_type=F32)


def _out_proj(x, att, ret, w_bf, *, tm):
    R = x.shape[0]
    row = lambda i: (i, 0)
    return pl.pallas_call(
        _out_kernel,
        out_shape=jax.ShapeDtypeStruct((R, D_MODEL), F32),
        grid=(R // tm,),
        in_specs=[pl.BlockSpec((tm, D_MODEL), row), pl.BlockSpec((tm, D_GRP), row),
                  pl.BlockSpec((tm, D_GRP), row), pl.BlockSpec((2 * D_GRP, D_MODEL), lambda i: (0, 0))],
        out_specs=pl.BlockSpec((tm, D_MODEL), row),
        compiler_params=pltpu.CompilerParams(
            dimension_semantics=("arbitrary",), vmem_limit_bytes=VMEM_LIMIT),
        name="out_proj",
    )(x, att, ret, w_bf)


def _rope_tables(pos):
    half = HEAD_DIM // 2
    inv = ROPE_BASE ** (-jnp.arange(half, dtype=F32) / half)
    ang = pos.astype(F32)[:, None] * inv[None, :]
    cos, sin = jnp.cos(ang), jnp.sin(ang)
    return (jnp.tile(jnp.concatenate([cos, cos], axis=1), (1, 2)),
            jnp.tile(jnp.concatenate([-sin, sin], axis=1), (1, 2)))


def _t5_bucket(dist):
    max_exact = N_BUCKETS // 2
    d = jnp.maximum(dist, 1).astype(F32)
    large = max_exact + (jnp.log(d / max_exact) / math.log(MAX_DISTANCE / max_exact)
                         * (N_BUCKETS - max_exact)).astype(jnp.int32)
    large = jnp.minimum(large, N_BUCKETS - 1)
    return jnp.where(dist < max_exact, dist, large)


def _rel_bias_of(dist, rel_bias):
    rb = rel_bias.astype(F32)
    b = rb[_t5_bucket(jnp.maximum(dist, 0))] - rb[N_BUCKETS - 1]
    return jnp.where((dist >= 0)[..., None], b, NEG)


def _pair_states_to_blockdiag(state):
    B = state.shape[0]
    s5 = state.reshape(B, N_PAIRS, 2, HEAD_DIM, HEAD_DIM)
    eye = jnp.eye(2, dtype=state.dtype)
    bd = s5[:, :, :, :, None, :] * eye[None, None, :, None, :, None]
    return bd.reshape(B, N_PAIRS, LANES, LANES)


def _blockdiag_to_head_states(bd):
    B = bd.shape[0]
    r = bd.reshape(B, N_PAIRS, 2, HEAD_DIM, 2, HEAD_DIM)
    d = jnp.stack([r[:, :, 0, :, 0, :], r[:, :, 1, :, 1, :]], axis=2)
    return d.reshape(B, N_HEADS, HEAD_DIM, HEAD_DIM)


def _feature_major_pages(cache):
    n_phys = cache.shape[0]
    return jnp.transpose(cache, (0, 2, 3, 1)).reshape(n_phys, D_GRP, PAGE_SIZE)


def _token_major_5d(xt, B, S):
    return jnp.transpose(xt.reshape(B, N_HEADS, HEAD_DIM, S), (0, 3, 1, 2))[None]


def kernel(x_prompt, x_sample, cache_k, cache_v, state_ret, page_table, ln_g, w_in,
           q_norm_g, k_norm_g, rel_bias, ret_norm_g, w_out):
    depth = ln_g.shape[0]
    assert depth == 1, "single-layer stack"
    B, S, _ = x_prompt.shape
    DB, DS, _ = x_sample.shape
    n_pages = page_table.shape[1]
    past_len = n_pages * PAGE_SIZE
    assert past_len % MOBA_BLOCK == 0 and DS <= MOBA_BLOCK

    w_in_bf = w_in[0].astype(BF16)
    w_out_bf = w_out[0].astype(BF16)
    lng = ln_g[0][None, :]
    qg = jnp.tile(q_norm_g[0], N_HEADS)[None, :]
    kg = jnp.tile(k_norm_g[0], N_HEADS)[None, :]
    gng = ret_norm_g[0][None, :]
    log_g = jnp.log1p(-jnp.exp2(-5.0 - jnp.arange(N_HEADS, dtype=F32)))

    xp = x_prompt.reshape(B * S, D_MODEL)
    cos_p, sin_p = _rope_tables(jnp.arange(S, dtype=jnp.int32))
    (qpm, kpm, kt, vt, vt_blk, km, ga, gr, qr, kr, vr) = _proj(
        xp, cos_p, sin_p, lng, w_in_bf, qg, kg, tm=MOBA_BLOCK, seq_len=S, feature_major=True)
    ret_p, rfin_p = _retention(qr, kr, vr, gr, jnp.zeros((B, N_PAIRS, LANES, LANES), F32),
                               _ret_tables(log_g, float(RET_CHUNK)), gng, B=B, S=S, T=1024, out_dtype=BF16)
    NI = S // MOBA_BLOCK
    km_pm = km.reshape(B, NI, N_PAIRS, LANES).transpose(0, 2, 1, 3).astype(BF16)
    kq = jnp.arange(MOBA_BLOCK, dtype=jnp.int32)
    dist_own = kq[None, :] - kq[:, None]
    bias_own = _rel_bias_of(dist_own, rel_bias).transpose(2, 0, 1)
    bias_prev = _rel_bias_of(dist_own + MOBA_BLOCK, rel_bias).transpose(2, 0, 1)
    att_p = _moba_prompt(qpm, kpm, vt_blk, km_pm, bias_own, bias_prev, ga, B=B, S=S)
    y_p = _out_proj(xp, att_p, ret_p, w_out_bf, tm=512).reshape(B, S, D_MODEL)

    RS = DB * DS
    xs = x_sample.reshape(RS, D_MODEL)
    pos_s = past_len + jnp.arange(DS, dtype=jnp.int32)
    cos_s, sin_s = _rope_tables(jnp.tile(pos_s, DB))
    (qpm_s, _, ka_s, va_s, _, _, ga_s, gr_s, qr_s, kr_s, vr_s) = _proj(
        xs, cos_s, sin_s, lng, w_in_bf, qg, kg, tm=RS, seq_len=RS, feature_major=False)
    ret_s, rfin_s = _retention(qr_s, kr_s, vr_s, gr_s, _pair_states_to_blockdiag(state_ret[0].astype(F32)),
                               _ret_tables(log_g, float(DS)), gng, B=DB, S=DS, T=DS, out_dtype=F32)
    q_s = qpm_s.transpose(1, 0, 2).reshape(DB, DS, D_GRP).astype(F32)
    t = jnp.arange(DS, dtype=jnp.int32)
    kk = jnp.arange(MOBA_BLOCK, dtype=jnp.int32)
    b_last = _rel_bias_of(t[:, None] + (MOBA_BLOCK - kk)[None, :], rel_bias)
    b_last = b_last.transpose(2, 0, 1).reshape(N_HEADS * DS, MOBA_BLOCK)
    b_new = _rel_bias_of(t[:, None] - t[None, :], rel_bias)
    b_new = b_new.transpose(2, 0, 1).reshape(N_HEADS * DS, DS)
    b_new = jnp.pad(b_new, ((0, 0), (0, LANES - DS)), constant_values=NEG)
    att_s = _sample_attention(page_table, _feature_major_pages(cache_k[0]), _feature_major_pages(cache_v[0]),
                              q_s, ka_s.reshape(DB, DS, D_GRP), va_s.reshape(DB, DS, D_GRP),
                              ga_s.reshape(DB, DS, D_GRP), b_last, b_new,
                              NP=math.gcd(n_pages, 16))
    y_s = _out_proj(xs, att_s.reshape(RS, D_GRP), ret_s, w_out_bf, tm=RS).reshape(DB, DS, D_MODEL)

    sdt = state_ret.dtype
    return (y_p, y_s,
            _token_major_5d(kt, B, S), _token_major_5d(vt, B, S),
            _blockdiag_to_head_states(rfin_p).astype(sdt)[None],
            ka_s.reshape(1, DB, DS, N_HEADS, HEAD_DIM), va_s.reshape(1, DB, DS, N_HEADS, HEAD_DIM),
            _blockdiag_to_head_states(rfin_s).astype(sdt)[None])
```

```python
import functools
import math

import jax
import jax.numpy as jnp
import numpy as np
from jax import lax
from jax.experimental import pallas as pl
from jax.experimental.pallas import tpu as pltpu

D_MODEL = 1024
HEAD_DIM = 64
N_HEADS = 8
D_GRP = N_HEADS * HEAD_DIM
LANES = 128
N_PAIRS = D_GRP // LANES
MOBA_BLOCK = 256
MOBA_TOPK = 3
SCORE_LOOKAHEAD = 4
RET_CHUNK = 128
PAGE_SIZE = 128
N_BUCKETS = 32
MAX_DISTANCE = 128
ROPE_BASE = 10000.0
EPS = 1e-6
NEG = -1e30
BIG = 1e30
LOWEST = -3e38
VMEM_LIMIT = 56 * 1024 * 1024

F32 = jnp.float32
BF16 = jnp.bfloat16
NT_DIMS = (((1,), (1,)), ((), ()))


def _silu(g):
    return g * (1.0 / (1.0 + jnp.exp(-g)))


def _head_mean(z):
    lane = lax.broadcasted_iota(jnp.int32, z.shape, 1)
    lo = lane < HEAD_DIM
    s0 = jnp.sum(jnp.where(lo, z, 0.0), axis=-1, keepdims=True)
    s1 = jnp.sum(jnp.where(lo, 0.0, z), axis=-1, keepdims=True)
    return jnp.where(lo, s0, s1) * (1.0 / HEAD_DIM)


def _top_k_blocks(gate, n_valid, blk, axis):
    n_blk = gate.shape[axis]
    g = jnp.where(blk < n_valid, gate, NEG)
    sel = jnp.zeros(gate.shape, F32)
    for _ in range(MOBA_TOPK):
        mx = jnp.max(g, axis=axis, keepdims=True)
        first = jnp.min(jnp.where(g == mx, blk, float(n_blk)), axis=axis, keepdims=True)
        hit = blk == first
        sel = jnp.where(hit, 1.0, sel)
        g = jnp.where(hit, LOWEST, g)
    return jnp.where(blk < n_valid, sel, 0.0)


def _proj_kernel(x_ref, cos_ref, sin_ref, lng_ref, w_ref, qg_ref, kg_ref,
                 qpm_ref, kpm_ref, k_ref, v_ref, vt_ref, km_ref, ga_ref, gr_ref,
                 qr_ref, kr_ref, vr_ref, *, feature_major):
    x = x_ref[...]
    ms = jnp.mean(x * x, axis=-1, keepdims=True)
    h = (x * lax.rsqrt(ms + EPS) * lng_ref[...]).astype(BF16)

    def seg(s):
        return jnp.dot(h, w_ref[:, s * D_GRP:(s + 1) * D_GRP], preferred_element_type=F32)

    def chunks(z):
        return [z[:, c * LANES:(c + 1) * LANES] for c in range(N_PAIRS)]

    qa = seg(0)
    for c, z in enumerate(chunks(qa)):
        g = qg_ref[:, c * LANES:(c + 1) * LANES]
        zn = z * lax.rsqrt(_head_mean(z * z) + EPS) * g
        qpm_ref[c] = (zn * (HEAD_DIM ** -0.5)).astype(BF16)
    ka = seg(1)
    kn = []
    for c, z in enumerate(chunks(ka)):
        g = kg_ref[:, c * LANES:(c + 1) * LANES]
        zn = z * lax.rsqrt(_head_mean(z * z) + EPS) * g
        kpm_ref[c] = zn.astype(BF16)
        kn.append(zn)
    kn = jnp.concatenate(kn, axis=1)
    for b in range(km_ref.shape[1]):
        km_ref[0, b:b + 1, :] = jnp.mean(kn[b * MOBA_BLOCK:(b + 1) * MOBA_BLOCK], axis=0, keepdims=True)
    va = seg(2)
    vat = va.T
    for b in range(vt_ref.shape[0]):
        vt_ref[b] = vat[:, b * MOBA_BLOCK:(b + 1) * MOBA_BLOCK].astype(BF16)
    if feature_major:
        k_ref[0] = kn.T
        v_ref[0] = vat
    else:
        k_ref[...] = kn
        v_ref[...] = va
    ga_ref[...] = seg(3)

    cos = cos_ref[...]
    sin = sin_ref[...]
    lane = lax.broadcasted_iota(jnp.int32, cos.shape, 1)
    first_half = (lane % HEAD_DIM) < (HEAD_DIM // 2)

    def rope(z):
        swapped = jnp.where(first_half,
                            pltpu.roll(z, LANES - HEAD_DIM // 2, 1),
                            pltpu.roll(z, HEAD_DIM // 2, 1))
        return z * cos + swapped * sin

    qr = seg(4)
    for c, z in enumerate(chunks(qr)):
        qr_ref[:, c * LANES:(c + 1) * LANES] = rope(z)
    kr = seg(5)
    for c, z in enumerate(chunks(kr)):
        kr_ref[:, c * LANES:(c + 1) * LANES] = rope(z) * (HEAD_DIM ** -0.5)
    vr_ref[...] = seg(6)
    gr_ref[...] = seg(7)


def _proj(x, cos, sin, ln_g, w_bf, qg, kg, *, tm, seq_len, feature_major):
    R = x.shape[0]
    assert tm % MOBA_BLOCK == 0 and R % tm == 0 and cos.shape[0] % tm == 0 and seq_len % tm == 0
    n_pos = cos.shape[0] // tm
    nb = tm // MOBA_BLOCK
    nt = seq_len // tm
    row = lambda i: (i, 0)
    full = lambda i: (0, 0)
    wide = pl.BlockSpec((tm, D_GRP), row)
    wide_shape = jax.ShapeDtypeStruct((R, D_GRP), F32)
    if feature_major:
        kv_shape = jax.ShapeDtypeStruct((R // seq_len, D_GRP, seq_len), F32)
        kv_spec = pl.BlockSpec((1, D_GRP, tm), lambda i: (i // nt, 0, i % nt))
    else:
        kv_shape, kv_spec = wide_shape, wide
    out_shape = (
        jax.ShapeDtypeStruct((N_PAIRS, R, LANES), BF16),
        jax.ShapeDtypeStruct((N_PAIRS, R, LANES), BF16),
        kv_shape,
        kv_shape,
        jax.ShapeDtypeStruct((R // MOBA_BLOCK, D_GRP, MOBA_BLOCK), BF16),
        jax.ShapeDtypeStruct((R // tm, nb, D_GRP), F32),
        wide_shape,
        wide_shape,
        wide_shape,
        wide_shape,
        wide_shape,
    )
    pm = pl.BlockSpec((N_PAIRS, tm, LANES), lambda i: (0, i, 0))
    out_specs = (
        pm, pm, kv_spec, kv_spec,
        pl.BlockSpec((nb, D_GRP, MOBA_BLOCK), lambda i: (i, 0, 0)),
        pl.BlockSpec((1, nb, D_GRP), lambda i: (i, 0, 0)),
        wide, wide, wide, wide, wide,
    )
    in_specs = [
        pl.BlockSpec((tm, D_MODEL), row),
        pl.BlockSpec((tm, LANES), lambda i: (i % n_pos, 0)),
        pl.BlockSpec((tm, LANES), lambda i: (i % n_pos, 0)),
        pl.BlockSpec((1, D_MODEL), full),
        pl.BlockSpec((D_MODEL, 8 * D_GRP), full),
        pl.BlockSpec((1, D_GRP), full),
        pl.BlockSpec((1, D_GRP), full),
    ]
    return pl.pallas_call(
        functools.partial(_proj_kernel, feature_major=feature_major),
        out_shape=out_shape,
        grid=(R // tm,),
        in_specs=in_specs,
        out_specs=out_specs,
        compiler_params=pltpu.CompilerParams(
            dimension_semantics=("arbitrary",), vmem_limit_bytes=VMEM_LIMIT),
        name="proj",
    )(x, cos, sin, ln_g, w_bf, qg, kg)


def _ret_kernel(q_ref, k_ref, v_ref, g_ref, r0_ref, d2_ref, qd_ref, kd_ref, gl_ref, gng_ref,
                out_ref, rfin_ref, r_scr, *, T, L):
    t = pl.program_id(1)

    @pl.when(t == 0)
    def _():
        r_scr[...] = r0_ref[0]

    lane = lax.broadcasted_iota(jnp.int32, (L, LANES), 1)
    lo = lane < HEAD_DIM
    ri = lax.broadcasted_iota(jnp.int32, (LANES, LANES), 0)
    ci = lax.broadcasted_iota(jnp.int32, (LANES, LANES), 1)
    same_head = (ri < HEAD_DIM) == (ci < HEAD_DIM)
    rows_c = min(T, L)

    def load(ref, rows, cols):
        z = ref[rows, cols]
        if rows_c < L:
            z = jnp.concatenate([z, jnp.zeros((L - rows_c, LANES), F32)], axis=0)
        return z

    def chunk(ci_, carry):
        if T > L:
            rows = pl.ds(pl.multiple_of(ci_ * L, L), L)
        else:
            rows = pl.ds(0, rows_c)
        for c in range(N_PAIRS):
            cols = slice(c * LANES, (c + 1) * LANES)
            q = load(q_ref, rows, cols)
            k = load(k_ref, rows, cols)
            v = load(v_ref, rows, cols)
            gate = load(g_ref, rows, cols)
            R = r_scr[c]
            vc = v.astype(BF16)
            qs = jnp.concatenate([jnp.where(lo, q, 0.0), jnp.where(lo, 0.0, q)], axis=0).astype(BF16)
            s = lax.dot_general(qs, k.astype(BF16), NT_DIMS, preferred_element_type=F32) * d2_ref[c]
            inner2 = jnp.dot(s.astype(BF16), vc, preferred_element_type=F32)
            inner = jnp.where(lo, inner2[:L], inner2[L:])
            cross = jnp.dot(q.astype(BF16), R.astype(BF16), preferred_element_type=F32) * qd_ref[:, cols]
            o = inner + cross
            kdk = (k * kd_ref[:, cols]).T.astype(BF16)
            upd = jnp.dot(kdk, vc, preferred_element_type=F32)
            r_scr[c] = gl_ref[:, cols] * R + jnp.where(same_head, upd, 0.0)
            mu = _head_mean(o)
            d = o - mu
            var = _head_mean(d * d)
            y = d * lax.rsqrt(var + EPS) * gng_ref[:, cols]
            res = (y * _silu(gate)).astype(out_ref.dtype)
            out_ref[rows, cols] = res[:rows_c]
        return carry

    lax.fori_loop(0, max(T // L, 1), chunk, 0)

    @pl.when(t == pl.num_programs(1) - 1)
    def _():
        rfin_ref[0] = r_scr[...]


def _retention(q, k, v, g, r0, tabs, gn_g, *, B, S, T, out_dtype):
    d2, qd, kd, gl = tabs
    L = RET_CHUNK
    nt = S // T
    tok = pl.BlockSpec((T, D_GRP), lambda b, t: (b * nt + t, 0))
    state = pl.BlockSpec((1, N_PAIRS, LANES, LANES), lambda b, t: (b, 0, 0, 0))
    const2 = lambda b, t: (0, 0)
    return pl.pallas_call(
        functools.partial(_ret_kernel, T=T, L=L),
        out_shape=(jax.ShapeDtypeStruct((B * S, D_GRP), out_dtype),
                   jax.ShapeDtypeStruct((B, N_PAIRS, LANES, LANES), F32)),
        grid=(B, nt),
        in_specs=[tok, tok, tok, tok, state,
                  pl.BlockSpec((N_PAIRS, 2 * L, L), lambda b, t: (0, 0, 0)),
                  pl.BlockSpec((L, D_GRP), const2), pl.BlockSpec((L, D_GRP), const2),
                  pl.BlockSpec((1, D_GRP), const2), pl.BlockSpec((1, D_GRP), const2)],
        out_specs=(tok, state),
        scratch_shapes=[pltpu.VMEM((N_PAIRS, LANES, LANES), F32)],
        compiler_params=pltpu.CompilerParams(
            dimension_semantics=("arbitrary", "arbitrary"), vmem_limit_bytes=VMEM_LIMIT),
        name="retention",
    )(q, k, v, g, r0, d2, qd, kd, gl, gn_g)


def _ret_tables(log_g, l_true):
    L = RET_CHUNK
    n = jnp.arange(L, dtype=F32)
    diff = n[:, None] - n[None, :]
    decay = jnp.where(diff >= 0, jnp.exp(log_g[:, None, None] * jnp.maximum(diff, 0.0)), 0.0)
    d2 = decay.reshape(N_PAIRS, 2 * L, L)
    per_lane = lambda t: jnp.repeat(t, HEAD_DIM, axis=1)
    qd = per_lane(jnp.exp(log_g[None, :] * (n[:, None] + 1.0)))
    kd = per_lane(jnp.exp(log_g[None, :] * (l_true - 1.0 - n)[:, None]))
    gl = per_lane(jnp.exp(log_g * l_true)[None, :])
    return d2, qd, kd, gl


def _t5_bucket_starts(max_dist):
    d = np.arange(max_dist)
    max_exact = N_BUCKETS // 2
    scaled = (np.log(np.maximum(d, 1).astype(np.float32) / np.float32(max_exact))
              / np.float32(math.log(MAX_DISTANCE / max_exact)) * np.float32(N_BUCKETS - max_exact))
    bucket = np.where(d < max_exact, d, np.minimum(max_exact + scaled.astype(np.int32), N_BUCKETS - 1))
    assert np.all(np.diff(bucket) >= 0)
    return [(int(b), int(d[bucket == b][0])) for b in np.unique(bucket)]


def _bias_tables(rb_ref, bo_ref, bp_ref):
    shape = (MOBA_BLOCK, MOBA_BLOCK)
    d_own = lax.broadcasted_iota(jnp.int32, shape, 1) - lax.broadcasted_iota(jnp.int32, shape, 0)
    starts = _t5_bucket_starts(2 * MOBA_BLOCK)
    assert starts[0] == (0, 0) and starts[-1][0] == N_BUCKETS - 1

    def head(h, carry):
        far = rb_ref[N_BUCKETS - 1, h]
        for ref, d in ((bo_ref, d_own), (bp_ref, d_own + MOBA_BLOCK)):
            val = jnp.full(shape, rb_ref[0, h] - far, F32)
            for bkt, lo in starts[1:]:
                val = jnp.where(d >= lo, rb_ref[bkt, h] - far, val)
            ref[h] = jnp.where(d >= 0, val, NEG)
        return carry

    lax.fori_loop(0, N_HEADS, head, 0)


def _moba_kernel(rb_ref, q_ref, k_ref, vt_ref, km_ref, ga_ref, out_ref,
                 qm_ref, acc_ref, m_ref, l_ref, sel_ref, ot_ref, bo_ref, bp_ref, s_scr, *, NI):
    i = pl.program_id(1)

    @pl.when((pl.program_id(0) == 0) & (i == 0))
    def _():
        _bias_tables(rb_ref, bo_ref, bp_ref)

    lane = lax.broadcasted_iota(jnp.int32, (MOBA_BLOCK, LANES), 1)
    blk = lax.broadcasted_iota(jnp.int32, (NI, MOBA_BLOCK), 0).astype(F32)
    i_f = i.astype(F32)
    for c in range(N_PAIRS):
        qc = q_ref[c].astype(F32)
        qm_ref[2 * c] = jnp.where(lane < HEAD_DIM, qc, 0.0).astype(BF16)
        qm_ref[2 * c + 1] = jnp.where(lane < HEAD_DIM, 0.0, qc).astype(BF16)

    def scores(j, h, bias_ref):
        kj = k_ref[h // 2, pl.ds(pl.multiple_of(j * MOBA_BLOCK, MOBA_BLOCK), MOBA_BLOCK), :]
        s = lax.dot_general(kj, qm_ref[h], NT_DIMS, preferred_element_type=F32)
        return s if bias_ref is None else s + bias_ref[h]

    def values(j, h):
        return vt_ref[j, h * HEAD_DIM:(h + 1) * HEAD_DIM, :]

    for h in range(N_HEADS):
        gate = lax.dot_general(km_ref[0, h // 2], qm_ref[h], NT_DIMS, preferred_element_type=F32)
        sel_ref[h] = _top_k_blocks(gate, i_f, blk, 0)

    d_far = _t5_bucket_starts(2 * MOBA_BLOCK)[-1][1]
    k0 = (MOBA_BLOCK - d_far + 1) // 8 * 8
    assert d_far - 1 <= LANES
    LA = SCORE_LOOKAHEAD

    def key_block(j, j_next, own):
        local = {h: scores(j, h, bo_ref) for h in range(LA)} if own else {}
        near = None if own else (j == i - 1).astype(F32)
        for h in range(N_HEADS):
            t = h + LA
            if t < N_HEADS:
                local[t] = scores(j, t, bo_ref if own else None)
            else:
                s_scr[t - N_HEADS] = scores(j_next, t - N_HEADS, None)
            s = local.pop(h) if h in local else s_scr[h]
            if own:
                cmax = jnp.max(s, axis=0, keepdims=True)
                p = jnp.exp(s - cmax)
                m_ref[h] = cmax
                l_ref[h] = jnp.sum(p, axis=0, keepdims=True)
                acc_ref[h] = jnp.dot(values(j, h), p.astype(BF16), preferred_element_type=F32)
            else:
                corner = s[k0:, :LANES] + near * bp_ref[h, k0:, :LANES]
                s = jnp.concatenate([s[:k0], jnp.concatenate([corner, s[k0:, LANES:]], axis=1)], axis=0)
                cmax = jnp.max(s, axis=0, keepdims=True)
                selj = sel_ref[h, pl.ds(j, 1), :] > 0.0
                m_old = m_ref[h]
                m_new = jnp.where(selj, jnp.maximum(m_old, cmax), m_old)
                alpha = jnp.exp(m_old - m_new)
                p = jnp.exp(s - jnp.where(selj, m_new, BIG))
                l_ref[h] = alpha * l_ref[h] + jnp.sum(p, axis=0, keepdims=True)
                acc_ref[h] = alpha * acc_ref[h] + jnp.dot(values(j, h), p.astype(BF16),
                                                          preferred_element_type=F32)
                m_ref[h] = m_new

    key_block(i, 0, True)

    def past(j, carry):
        key_block(j, j + 1, False)
        return carry

    lax.fori_loop(0, i, past, 0)
    for h in range(N_HEADS):
        ot_ref[h * HEAD_DIM:(h + 1) * HEAD_DIM, :] = acc_ref[h] / l_ref[h]
    out_ref[...] = (ot_ref[...].T * _silu(ga_ref[...])).astype(out_ref.dtype)


def _moba_prompt(rel_bias, qpm, kpm, vt, km_pm, ga, *, B, S):
    NI = S // MOBA_BLOCK
    return pl.pallas_call(
        functools.partial(_moba_kernel, NI=NI),
        out_shape=jax.ShapeDtypeStruct((B * S, D_GRP), BF16),
        grid=(B, NI),
        in_specs=[
            pl.BlockSpec(memory_space=pltpu.SMEM),
            pl.BlockSpec((N_PAIRS, MOBA_BLOCK, LANES), lambda b, i: (0, b * NI + i, 0)),
            pl.BlockSpec((N_PAIRS, S, LANES), lambda b, i: (0, b, 0)),
            pl.BlockSpec((NI, D_GRP, MOBA_BLOCK), lambda b, i: (b, 0, 0)),
            pl.BlockSpec((1, N_PAIRS, NI, LANES), lambda b, i: (b, 0, 0, 0)),
            pl.BlockSpec((MOBA_BLOCK, D_GRP), lambda b, i: (b * NI + i, 0)),
        ],
        out_specs=pl.BlockSpec((MOBA_BLOCK, D_GRP), lambda b, i: (b * NI + i, 0)),
        scratch_shapes=[
            pltpu.VMEM((N_HEADS, MOBA_BLOCK, LANES), BF16),
            pltpu.VMEM((N_HEADS, HEAD_DIM, MOBA_BLOCK), F32),
            pltpu.VMEM((N_HEADS, 1, MOBA_BLOCK), F32),
            pltpu.VMEM((N_HEADS, 1, MOBA_BLOCK), F32),
            pltpu.VMEM((N_HEADS, NI, MOBA_BLOCK), F32),
            pltpu.VMEM((D_GRP, MOBA_BLOCK), F32),
            pltpu.VMEM((N_HEADS, MOBA_BLOCK, MOBA_BLOCK), F32),
            pltpu.VMEM((N_HEADS, MOBA_BLOCK, MOBA_BLOCK), F32),
            pltpu.VMEM((SCORE_LOOKAHEAD, MOBA_BLOCK, MOBA_BLOCK), F32),
        ],
        compiler_params=pltpu.CompilerParams(
            dimension_semantics=("arbitrary", "arbitrary"), vmem_limit_bytes=VMEM_LIMIT),
        name="moba_prompt",
    )(rel_bias, qpm, kpm, vt, km_pm, ga)


def _sample_kernel(pt_ref, *refs, NP, NS, NB, TQ):
    del pt_ref
    kp = refs[:NP]
    vp = refs[NP:2 * NP]
    (q_ref, kn_ref, vn_ref, ga_ref, blast_ref, bnew_ref, out_ref,
     s_ref, kmt_ref, bmax_ref, meff_ref, m_ref, l_ref, acc_ref, qbd_ref) = refs[2 * NP:]
    s = pl.program_id(1)
    HQ = N_HEADS * TQ
    bps = NP * PAGE_SIZE // MOBA_BLOCK
    lane_f = lax.broadcasted_iota(jnp.int32, (D_GRP, LANES), 1)
    lane_q = lax.broadcasted_iota(jnp.int32, (HQ, LANES), 1)

    @pl.when(s == 0)
    def _():
        qt = jnp.concatenate([q_ref[0]] * N_HEADS, axis=0)
        r = lax.broadcasted_iota(jnp.int32, (HQ, D_GRP), 0)
        l = lax.broadcasted_iota(jnp.int32, (HQ, D_GRP), 1)
        qbd_ref[...] = jnp.where(r // TQ == l // HEAD_DIM, qt, 0.0).astype(BF16)
        kmt_ref[...] = jnp.zeros(kmt_ref.shape, F32)
        bmax_ref[...] = jnp.zeros(bmax_ref.shape, F32)

    def block(pages, u):
        return jnp.concatenate([pages[2 * u][0], pages[2 * u + 1][0]], axis=1)

    @pl.when(s < NS)
    def _():
        for u in range(bps):
            jb = s * bps + u
            kt = block(kp, u)
            km = jnp.sum(kt, axis=1, keepdims=True) * (1.0 / MOBA_BLOCK)
            kmt_ref[...] = jnp.where(lane_f == jb, km, kmt_ref[...])
            sc = jnp.dot(qbd_ref[...], kt.astype(BF16), preferred_element_type=F32)
            if u == bps - 1:
                sc = sc + jnp.where(s == NS - 1, blast_ref[...], 0.0)
            s_ref[jb] = sc
            bmax_ref[...] = jnp.where(lane_q == jb, jnp.max(sc, axis=1, keepdims=True), bmax_ref[...])

    @pl.when(s == NS - 1)
    def _():
        gate = jnp.dot(qbd_ref[...], kmt_ref[...].astype(BF16), preferred_element_type=F32)
        sel = _top_k_blocks(gate, float(NB), lane_q.astype(F32), 1)
        pad = jnp.zeros((LANES - TQ, D_GRP), F32)
        kn = jnp.concatenate([kn_ref[0], pad], axis=0).astype(BF16)
        vn = jnp.concatenate([vn_ref[0], pad], axis=0).astype(BF16)
        sn = lax.dot_general(qbd_ref[...], kn, NT_DIMS, preferred_element_type=F32) + bnew_ref[...]
        m = jnp.maximum(jnp.max(jnp.where(sel > 0.0, bmax_ref[...], NEG), axis=1, keepdims=True),
                        jnp.max(sn, axis=1, keepdims=True))
        m_ref[...] = m
        meff_ref[...] = jnp.where(sel > 0.0, m, BIG)
        pn = jnp.exp(sn - m)
        l_ref[...] = jnp.sum(pn, axis=1, keepdims=True)
        acc_ref[...] = jnp.dot(pn.astype(BF16), vn, preferred_element_type=F32)

    @pl.when(s >= NS)
    def _():
        for u in range(bps):
            jb = (s - NS) * bps + u
            vt = block(vp, u).astype(BF16)
            meff = jnp.sum(jnp.where(lane_q == jb, meff_ref[...], 0.0), axis=1, keepdims=True)
            p = jnp.exp(s_ref[jb] - meff)
            l_ref[...] += jnp.sum(p, axis=1, keepdims=True)
            acc_ref[...] += lax.dot_general(p.astype(BF16), vt, NT_DIMS, preferred_element_type=F32)

    @pl.when(s == 2 * NS - 1)
    def _():
        o_bd = acc_ref[...] / l_ref[...]
        lane = lax.broadcasted_iota(jnp.int32, (TQ, D_GRP), 1)
        o = jnp.zeros((TQ, D_GRP), F32)
        for h in range(N_HEADS):
            o = jnp.where(lane // HEAD_DIM == h, o_bd[h * TQ:(h + 1) * TQ], o)
        out_ref[0] = o * _silu(ga_ref[0])


def _sample_attention(page_table, cache_kt, cache_vt, q, k_new, v_new, ga, bias_last, bias_new, *, NP):
    B, n_pages = page_table.shape
    TQ = q.shape[1]
    HQ = N_HEADS * TQ
    NS = n_pages // NP
    NB = n_pages * PAGE_SIZE // MOBA_BLOCK
    assert NB <= LANES and HQ <= LANES and TQ <= LANES
    page = (1, D_GRP, PAGE_SIZE)

    def k_map(t):
        return lambda b, s, pt: (pt[b, NP * jnp.minimum(s, NS - 1) + t], 0, 0)

    def v_map(t):
        return lambda b, s, pt: (pt[b, NP * jnp.maximum(s - NS, 0) + t], 0, 0)

    per_b = pl.BlockSpec((1, TQ, D_GRP), lambda b, s, pt: (b, 0, 0))
    in_specs = ([pl.BlockSpec(page, k_map(t)) for t in range(NP)]
                + [pl.BlockSpec(page, v_map(t)) for t in range(NP)]
                + [per_b, per_b, per_b, per_b,
                   pl.BlockSpec((HQ, MOBA_BLOCK), lambda b, s, pt: (0, 0)),
                   pl.BlockSpec((HQ, LANES), lambda b, s, pt: (0, 0))])
    grid_spec = pltpu.PrefetchScalarGridSpec(
        num_scalar_prefetch=1,
        grid=(B, 2 * NS),
        in_specs=in_specs,
        out_specs=per_b,
        scratch_shapes=[
            pltpu.VMEM((NB, HQ, MOBA_BLOCK), F32),
            pltpu.VMEM((D_GRP, LANES), F32),
            pltpu.VMEM((HQ, LANES), F32),
            pltpu.VMEM((HQ, LANES), F32),
            pltpu.VMEM((HQ, 1), F32),
            pltpu.VMEM((HQ, 1), F32),
            pltpu.VMEM((HQ, D_GRP), F32),
            pltpu.VMEM((HQ, D_GRP), BF16),
        ],
    )
    return pl.pallas_call(
        functools.partial(_sample_kernel, NP=NP, NS=NS, NB=NB, TQ=TQ),
        out_shape=jax.ShapeDtypeStruct((B, TQ, D_GRP), F32),
        grid_spec=grid_spec,
        compiler_params=pltpu.CompilerParams(
            dimension_semantics=("arbitrary", "arbitrary"), vmem_limit_bytes=VMEM_LIMIT),
        name="sample_attention",
    )(page_table, *([cache_kt] * NP), *([cache_vt] * NP), q, k_new, v_new, ga, bias_last, bias_new)


def _out_kernel(x_ref, a_ref, r_ref, w_ref, y_ref):
    cat = jnp.concatenate([a_ref[...].astype(BF16), r_ref[...].astype(BF16)], axis=1)
    y_ref[...] = x_ref[...] + jnp.dot(cat, w_ref[...], preferred_element_type=F32)


def _out_proj(x, att, ret, w_bf, *, tm):
    R = x.shape[0]
    row = lambda i: (i, 0)
    return pl.pallas_call(
        _out_kernel,
        out_shape=jax.ShapeDtypeStruct((R, D_MODEL), F32),
        grid=(R // tm,),
        in_specs=[pl.BlockSpec((tm, D_MODEL), row), pl.BlockSpec((tm, D_GRP), row),
                  pl.BlockSpec((tm, D_GRP), row), pl.BlockSpec((2 * D_GRP, D_MODEL), lambda i: (0, 0))],
        out_specs=pl.BlockSpec((tm, D_MODEL), row),
        compiler_params=pltpu.CompilerParams(
            dimension_semantics=("arbitrary",), vmem_limit_bytes=VMEM_LIMIT),
        name="out_proj",
    )(x, att, ret, w_bf)


def _rope_tables(pos):
    half = HEAD_DIM // 2
    inv = ROPE_BASE ** (-jnp.arange(half, dtype=F32) / half)
    ang = pos.astype(F32)[:, None] * inv[None, :]
    cos, sin = jnp.cos(ang), jnp.sin(ang)
    return (jnp.tile(jnp.concatenate([cos, cos], axis=1), (1, 2)),
            jnp.tile(jnp.concatenate([-sin, sin], axis=1), (1, 2)))


def _t5_bucket(dist):
    max_exact = N_BUCKETS // 2
    d = jnp.maximum(dist, 1).astype(F32)
    large = max_exact + (jnp.log(d / max_exact) / math.log(MAX_DISTANCE / max_exact)
                         * (N_BUCKETS - max_exact)).astype(jnp.int32)
    large = jnp.minimum(large, N_BUCKETS - 1)
    return jnp.where(dist < max_exact, dist, large)


def _rel_bias_of(dist, rel_bias):
    rb = rel_bias.astype(F32)
    b = rb[_t5_bucket(jnp.maximum(dist, 0))] - rb[N_BUCKETS - 1]
    return jnp.where((dist >= 0)[..., None], b, NEG)


def _pair_states_to_blockdiag(state):
    B = state.shape[0]
    s5 = state.reshape(B, N_PAIRS, 2, HEAD_DIM, HEAD_DIM)
    top = jnp.pad(s5[:, :, 0], ((0, 0), (0, 0), (0, 0), (0, HEAD_DIM)))
    bot = jnp.pad(s5[:, :, 1], ((0, 0), (0, 0), (0, 0), (HEAD_DIM, 0)))
    return jnp.concatenate([top, bot], axis=2)


def _blockdiag_to_head_states(bd):
    B = bd.shape[0]
    d = jnp.stack([bd[:, :, :HEAD_DIM, :HEAD_DIM], bd[:, :, HEAD_DIM:, HEAD_DIM:]], axis=2)
    return d.reshape(B, N_HEADS, HEAD_DIM, HEAD_DIM)


def _feature_major_pages(cache):
    n_phys = cache.shape[0]
    return jnp.transpose(cache, (0, 2, 3, 1)).reshape(n_phys, D_GRP, PAGE_SIZE)


def _token_major_5d(xt, B, S):
    return jnp.transpose(xt.reshape(B, N_HEADS, HEAD_DIM, S), (0, 3, 1, 2))[None]


def kernel(x_prompt, x_sample, cache_k, cache_v, state_ret, page_table, ln_g, w_in,
           q_norm_g, k_norm_g, rel_bias, ret_norm_g, w_out):
    depth = ln_g.shape[0]
    assert depth == 1, "single-layer stack"
    B, S, _ = x_prompt.shape
    DB, DS, _ = x_sample.shape
    n_pages = page_table.shape[1]
    past_len = n_pages * PAGE_SIZE
    assert past_len % MOBA_BLOCK == 0 and DS <= MOBA_BLOCK

    w_in_bf = w_in[0].astype(BF16)
    w_out_bf = w_out[0].astype(BF16)
    lng = ln_g[0][None, :]
    qg = jnp.tile(q_norm_g[0], N_HEADS)[None, :]
    kg = jnp.tile(k_norm_g[0], N_HEADS)[None, :]
    gng = ret_norm_g[0][None, :]
    log_g = jnp.log1p(-jnp.exp2(-5.0 - jnp.arange(N_HEADS, dtype=F32)))

    xp = x_prompt.reshape(B * S, D_MODEL)
    cos_p, sin_p = _rope_tables(jnp.arange(S, dtype=jnp.int32))
    (qpm, kpm, kt, vt, vt_blk, km, ga, gr, qr, kr, vr) = _proj(
        xp, cos_p, sin_p, lng, w_in_bf, qg, kg, tm=MOBA_BLOCK, seq_len=S, feature_major=True)
    ret_p, rfin_p = _retention(qr, kr, vr, gr, jnp.zeros((B, N_PAIRS, LANES, LANES), F32),
                               _ret_tables(log_g, float(RET_CHUNK)), gng, B=B, S=S, T=1024, out_dtype=BF16)
    NI = S // MOBA_BLOCK
    km_pm = km.reshape(B, NI, N_PAIRS, LANES).transpose(0, 2, 1, 3).astype(BF16)
    att_p = _moba_prompt(rel_bias.astype(F32), qpm, kpm, vt_blk, km_pm, ga, B=B, S=S)
    y_p = _out_proj(xp, att_p, ret_p, w_out_bf, tm=512).reshape(B, S, D_MODEL)

    RS = DB * DS
    xs = x_sample.reshape(RS, D_MODEL)
    pos_s = past_len + jnp.arange(DS, dtype=jnp.int32)
    cos_s, sin_s = _rope_tables(jnp.tile(pos_s, DB))
    (qpm_s, _, ka_s, va_s, _, _, ga_s, gr_s, qr_s, kr_s, vr_s) = _proj(
        xs, cos_s, sin_s, lng, w_in_bf, qg, kg, tm=RS, seq_len=RS, feature_major=False)
    ret_s, rfin_s = _retention(qr_s, kr_s, vr_s, gr_s, _pair_states_to_blockdiag(state_ret[0].astype(F32)),
                               _ret_tables(log_g, float(DS)), gng, B=DB, S=DS, T=DS, out_dtype=F32)
    q_s = qpm_s.transpose(1, 0, 2).reshape(DB, DS, D_GRP).astype(F32)
    t = jnp.arange(DS, dtype=jnp.int32)
    kk = jnp.arange(MOBA_BLOCK, dtype=jnp.int32)
    b_last = _rel_bias_of(t[:, None] + (MOBA_BLOCK - kk)[None, :], rel_bias)
    b_last = b_last.transpose(2, 0, 1).reshape(N_HEADS * DS, MOBA_BLOCK)
    b_new = _rel_bias_of(t[:, None] - t[None, :], rel_bias)
    b_new = b_new.transpose(2, 0, 1).reshape(N_HEADS * DS, DS)
    b_new = jnp.pad(b_new, ((0, 0), (0, LANES - DS)), constant_values=NEG)
    att_s = _sample_attention(page_table, _feature_major_pages(cache_k[0]), _feature_major_pages(cache_v[0]),
                              q_s, ka_s.reshape(DB, DS, D_GRP), va_s.reshape(DB, DS, D_GRP),
                              ga_s.reshape(DB, DS, D_GRP), b_last, b_new,
                              NP=math.gcd(n_pages, 16))
    y_s = _out_proj(xs, att_s.reshape(RS, D_GRP), ret_s, w_out_bf, tm=RS).reshape(DB, DS, D_MODEL)

    sdt = state_ret.dtype
    return (y_p, y_s,
            _token_major_5d(kt, B, S), _token_major_5d(vt, B, S),
            _blockdiag_to_head_states(rfin_p).astype(sdt)[None],
            ka_s.reshape(1, DB, DS, N_HEADS, HEAD_DIM), va_s.reshape(1, DB, DS, N_HEADS, HEAD_DIM),
            _blockdiag_to_head_states(rfin_s).astype(sdt)[None])
```

```python
import functools
import math

import jax
import jax.numpy as jnp
import numpy as np
from jax import lax
from jax.experimental import pallas as pl
from jax.experimental.pallas import tpu as pltpu

D_MODEL = 1024
HEAD_DIM = 64
N_HEADS = 8
D_GRP = N_HEADS * HEAD_DIM
LANES = 128
N_PAIRS = D_GRP // LANES
MOBA_BLOCK = 256
MOBA_TOPK = 3
SCORE_LOOKAHEAD = 4
RET_CHUNK = 128
PAGE_SIZE = 128
N_BUCKETS = 32
MAX_DISTANCE = 128
ROPE_BASE = 10000.0
EPS = 1e-6
NEG = -1e30
BIG = 1e30
LOWEST = -3e38
LOG2E = math.log2(math.e)
Q_SCALE = HEAD_DIM ** -0.5 * LOG2E
V_ROWS = HEAD_DIM + 16
VMEM_LIMIT = 56 * 1024 * 1024

F32 = jnp.float32
BF16 = jnp.bfloat16
NT_DIMS = (((1,), (1,)), ((), ()))


def _silu(g):
    return g * (1.0 / (1.0 + jnp.exp(-g)))


def _head_mean(z):
    lane = lax.broadcasted_iota(jnp.int32, z.shape, 1)
    lo = lane < HEAD_DIM
    s0 = jnp.sum(jnp.where(lo, z, 0.0), axis=-1, keepdims=True)
    s1 = jnp.sum(jnp.where(lo, 0.0, z), axis=-1, keepdims=True)
    return jnp.where(lo, s0, s1) * (1.0 / HEAD_DIM)


def _top_k_blocks(gate, n_valid, blk, axis):
    n_blk = gate.shape[axis]
    g = jnp.where(blk < n_valid, gate, NEG)
    sel = jnp.zeros(gate.shape, F32)
    for _ in range(MOBA_TOPK):
        mx = jnp.max(g, axis=axis, keepdims=True)
        first = jnp.min(jnp.where(g == mx, blk, float(n_blk)), axis=axis, keepdims=True)
        hit = blk == first
        sel = jnp.where(hit, 1.0, sel)
        g = jnp.where(hit, LOWEST, g)
    return jnp.where(blk < n_valid, sel, 0.0)


def _proj_kernel(x_ref, cos_ref, sin_ref, lng_ref, w_ref, qg_ref, kg_ref,
                 qpm_ref, kpm_ref, k_ref, v_ref, vt_ref, km_ref, ga_ref, gr_ref,
                 qr_ref, kr_ref, vr_ref, *, feature_major):
    x = x_ref[...]
    ms = jnp.mean(x * x, axis=-1, keepdims=True)
    h = (x * lax.rsqrt(ms + EPS) * lng_ref[...]).astype(BF16)

    def seg(s):
        return jnp.dot(h, w_ref[:, s * D_GRP:(s + 1) * D_GRP], preferred_element_type=F32)

    def chunks(z):
        return [z[:, c * LANES:(c + 1) * LANES] for c in range(N_PAIRS)]

    qa = seg(0)
    for c, z in enumerate(chunks(qa)):
        g = qg_ref[:, c * LANES:(c + 1) * LANES]
        zn = z * lax.rsqrt(_head_mean(z * z) + EPS) * g
        qpm_ref[c] = (zn * Q_SCALE).astype(BF16)
    ka = seg(1)
    kn = []
    for c, z in enumerate(chunks(ka)):
        g = kg_ref[:, c * LANES:(c + 1) * LANES]
        zn = z * lax.rsqrt(_head_mean(z * z) + EPS) * g
        kpm_ref[c] = zn.astype(BF16)
        kn.append(zn)
    kn = jnp.concatenate(kn, axis=1)
    for b in range(km_ref.shape[1]):
        km_ref[0, b:b + 1, :] = jnp.mean(kn[b * MOBA_BLOCK:(b + 1) * MOBA_BLOCK], axis=0, keepdims=True)
    va = seg(2)
    vat = va.T
    ones = jnp.ones((V_ROWS - HEAD_DIM, MOBA_BLOCK), F32)
    for b in range(vt_ref.shape[0]):
        vb = vat[:, b * MOBA_BLOCK:(b + 1) * MOBA_BLOCK]
        vt_ref[b] = jnp.concatenate(
            [t for h in range(N_HEADS) for t in (vb[h * HEAD_DIM:(h + 1) * HEAD_DIM], ones)],
            axis=0).astype(BF16)
    if feature_major:
        k_ref[0] = kn.T
        v_ref[0] = vat
    else:
        k_ref[...] = kn
        v_ref[...] = va
    ga_ref[...] = seg(3)

    cos = cos_ref[...]
    sin = sin_ref[...]
    lane = lax.broadcasted_iota(jnp.int32, cos.shape, 1)
    first_half = (lane % HEAD_DIM) < (HEAD_DIM // 2)

    def rope(z):
        swapped = jnp.where(first_half,
                            pltpu.roll(z, LANES - HEAD_DIM // 2, 1),
                            pltpu.roll(z, HEAD_DIM // 2, 1))
        return z * cos + swapped * sin

    qr = seg(4)
    for c, z in enumerate(chunks(qr)):
        qr_ref[:, c * LANES:(c + 1) * LANES] = rope(z)
    kr = seg(5)
    for c, z in enumerate(chunks(kr)):
        kr_ref[:, c * LANES:(c + 1) * LANES] = rope(z) * (HEAD_DIM ** -0.5)
    vr_ref[...] = seg(6)
    gr_ref[...] = seg(7)


def _proj(x, cos, sin, ln_g, w_bf, qg, kg, *, tm, seq_len, feature_major):
    R = x.shape[0]
    assert tm % MOBA_BLOCK == 0 and R % tm == 0 and cos.shape[0] % tm == 0 and seq_len % tm == 0
    n_pos = cos.shape[0] // tm
    nb = tm // MOBA_BLOCK
    nt = seq_len // tm
    row = lambda i: (i, 0)
    full = lambda i: (0, 0)
    wide = pl.BlockSpec((tm, D_GRP), row)
    wide_shape = jax.ShapeDtypeStruct((R, D_GRP), F32)
    if feature_major:
        kv_shape = jax.ShapeDtypeStruct((R // seq_len, D_GRP, seq_len), F32)
        kv_spec = pl.BlockSpec((1, D_GRP, tm), lambda i: (i // nt, 0, i % nt))
    else:
        kv_shape, kv_spec = wide_shape, wide
    out_shape = (
        jax.ShapeDtypeStruct((N_PAIRS, R, LANES), BF16),
        jax.ShapeDtypeStruct((N_PAIRS, R, LANES), BF16),
        kv_shape,
        kv_shape,
        jax.ShapeDtypeStruct((R // MOBA_BLOCK, N_HEADS * V_ROWS, MOBA_BLOCK), BF16),
        jax.ShapeDtypeStruct((R // tm, nb, D_GRP), F32),
        wide_shape,
        wide_shape,
        wide_shape,
        wide_shape,
        wide_shape,
    )
    pm = pl.BlockSpec((N_PAIRS, tm, LANES), lambda i: (0, i, 0))
    out_specs = (
        pm, pm, kv_spec, kv_spec,
        pl.BlockSpec((nb, N_HEADS * V_ROWS, MOBA_BLOCK), lambda i: (i, 0, 0)),
        pl.BlockSpec((1, nb, D_GRP), lambda i: (i, 0, 0)),
        wide, wide, wide, wide, wide,
    )
    in_specs = [
        pl.BlockSpec((tm, D_MODEL), row),
        pl.BlockSpec((tm, LANES), lambda i: (i % n_pos, 0)),
        pl.BlockSpec((tm, LANES), lambda i: (i % n_pos, 0)),
        pl.BlockSpec((1, D_MODEL), full),
        pl.BlockSpec((D_MODEL, 8 * D_GRP), full),
        pl.BlockSpec((1, D_GRP), full),
        pl.BlockSpec((1, D_GRP), full),
    ]
    return pl.pallas_call(
        functools.partial(_proj_kernel, feature_major=feature_major),
        out_shape=out_shape,
        grid=(R // tm,),
        in_specs=in_specs,
        out_specs=out_specs,
        compiler_params=pltpu.CompilerParams(
            dimension_semantics=("arbitrary",), vmem_limit_bytes=VMEM_LIMIT),
        name="proj",
    )(x, cos, sin, ln_g, w_bf, qg, kg)


def _ret_kernel(q_ref, k_ref, v_ref, g_ref, r0_ref, d2_ref, qd_ref, kd_ref, gl_ref, gng_ref,
                out_ref, rfin_ref, r_scr, *, T, L):
    t = pl.program_id(1)

    @pl.when(t == 0)
    def _():
        r_scr[...] = r0_ref[0]

    lane = lax.broadcasted_iota(jnp.int32, (L, LANES), 1)
    lo = lane < HEAD_DIM
    ri = lax.broadcasted_iota(jnp.int32, (LANES, LANES), 0)
    ci = lax.broadcasted_iota(jnp.int32, (LANES, LANES), 1)
    same_head = (ri < HEAD_DIM) == (ci < HEAD_DIM)
    rows_c = min(T, L)

    def load(ref, rows, cols):
        z = ref[rows, cols]
        if rows_c < L:
            z = jnp.concatenate([z, jnp.zeros((L - rows_c, LANES), F32)], axis=0)
        return z

    def chunk(ci_, carry):
        if T > L:
            rows = pl.ds(pl.multiple_of(ci_ * L, L), L)
        else:
            rows = pl.ds(0, rows_c)
        for c in range(N_PAIRS):
            cols = slice(c * LANES, (c + 1) * LANES)
            q = load(q_ref, rows, cols)
            k = load(k_ref, rows, cols)
            v = load(v_ref, rows, cols)
            gate = load(g_ref, rows, cols)
            R = r_scr[c]
            vc = v.astype(BF16)
            qs = jnp.concatenate([jnp.where(lo, q, 0.0), jnp.where(lo, 0.0, q)], axis=0).astype(BF16)
            s = lax.dot_general(qs, k.astype(BF16), NT_DIMS, preferred_element_type=F32) * d2_ref[c]
            inner2 = jnp.dot(s.astype(BF16), vc, preferred_element_type=F32)
            inner = jnp.where(lo, inner2[:L], inner2[L:])
            cross = jnp.dot(q.astype(BF16), R.astype(BF16), preferred_element_type=F32) * qd_ref[:, cols]
            o = inner + cross
            kdk = (k * kd_ref[:, cols]).T.astype(BF16)
            upd = jnp.dot(kdk, vc, preferred_element_type=F32)
            r_scr[c] = gl_ref[:, cols] * R + jnp.where(same_head, upd, 0.0)
            mu = _head_mean(o)
            d = o - mu
            var = _head_mean(d * d)
            y = d * lax.rsqrt(var + EPS) * gng_ref[:, cols]
            res = (y * _silu(gate)).astype(out_ref.dtype)
            out_ref[rows, cols] = res[:rows_c]
        return carry

    lax.fori_loop(0, max(T // L, 1), chunk, 0)

    @pl.when(t == pl.num_programs(1) - 1)
    def _():
        rfin_ref[0] = r_scr[...]


def _retention(q, k, v, g, r0, tabs, gn_g, *, B, S, T, out_dtype):
    d2, qd, kd, gl = tabs
    L = RET_CHUNK
    nt = S // T
    tok = pl.BlockSpec((T, D_GRP), lambda b, t: (b * nt + t, 0))
    state = pl.BlockSpec((1, N_PAIRS, LANES, LANES), lambda b, t: (b, 0, 0, 0))
    const2 = lambda b, t: (0, 0)
    return pl.pallas_call(
        functools.partial(_ret_kernel, T=T, L=L),
        out_shape=(jax.ShapeDtypeStruct((B * S, D_GRP), out_dtype),
                   jax.ShapeDtypeStruct((B, N_PAIRS, LANES, LANES), F32)),
        grid=(B, nt),
        in_specs=[tok, tok, tok, tok, state,
                  pl.BlockSpec((N_PAIRS, 2 * L, L), lambda b, t: (0, 0, 0)),
                  pl.BlockSpec((L, D_GRP), const2), pl.BlockSpec((L, D_GRP), const2),
                  pl.BlockSpec((1, D_GRP), const2), pl.BlockSpec((1, D_GRP), const2)],
        out_specs=(tok, state),
        scratch_shapes=[pltpu.VMEM((N_PAIRS, LANES, LANES), F32)],
        compiler_params=pltpu.CompilerParams(
            dimension_semantics=("arbitrary", "arbitrary"), vmem_limit_bytes=VMEM_LIMIT),
        name="retention",
    )(q, k, v, g, r0, d2, qd, kd, gl, gn_g)


def _ret_tables(log_g, l_true):
    L = RET_CHUNK
    n = jnp.arange(L, dtype=F32)
    diff = n[:, None] - n[None, :]
    decay = jnp.where(diff >= 0, jnp.exp(log_g[:, None, None] * jnp.maximum(diff, 0.0)), 0.0)
    d2 = decay.reshape(N_PAIRS, 2 * L, L)
    per_lane = lambda t: jnp.repeat(t, HEAD_DIM, axis=1)
    qd = per_lane(jnp.exp(log_g[None, :] * (n[:, None] + 1.0)))
    kd = per_lane(jnp.exp(log_g[None, :] * (l_true - 1.0 - n)[:, None]))
    gl = per_lane(jnp.exp(log_g * l_true)[None, :])
    return d2, qd, kd, gl


def _t5_bucket_starts(max_dist):
    d = np.arange(max_dist)
    max_exact = N_BUCKETS // 2
    scaled = (np.log(np.maximum(d, 1).astype(np.float32) / np.float32(max_exact))
              / np.float32(math.log(MAX_DISTANCE / max_exact)) * np.float32(N_BUCKETS - max_exact))
    bucket = np.where(d < max_exact, d, np.minimum(max_exact + scaled.astype(np.int32), N_BUCKETS - 1))
    assert np.all(np.diff(bucket) >= 0)
    return [(int(b), int(d[bucket == b][0])) for b in np.unique(bucket)]


def _bias_tables(rb_ref, bo_ref, bp_ref):
    shape = (MOBA_BLOCK, MOBA_BLOCK)
    d_own = lax.broadcasted_iota(jnp.int32, shape, 1) - lax.broadcasted_iota(jnp.int32, shape, 0)
    starts = _t5_bucket_starts(2 * MOBA_BLOCK)
    assert starts[0] == (0, 0) and starts[-1][0] == N_BUCKETS - 1

    def head(h, carry):
        far = rb_ref[N_BUCKETS - 1, h]
        for ref, d in ((bo_ref, d_own), (bp_ref, d_own + MOBA_BLOCK)):
            val = jnp.full(shape, (rb_ref[0, h] - far) * LOG2E, F32)
            for bkt, lo in starts[1:]:
                val = jnp.where(d >= lo, (rb_ref[bkt, h] - far) * LOG2E, val)
            ref[h] = jnp.where(d >= 0, val, NEG)
        return carry

    lax.fori_loop(0, N_HEADS, head, 0)


def _moba_kernel(rb_ref, q_ref, k_ref, vt_ref, km_ref, ga_ref, out_ref,
                 qm_ref, acc_ref, m_ref, sel_ref, ot_ref, bo_ref, bp_ref, s_scr, p_scr, alpha_scr, *, NI):
    i = pl.program_id(1)

    @pl.when((pl.program_id(0) == 0) & (i == 0))
    def _():
        _bias_tables(rb_ref, bo_ref, bp_ref)

    lane = lax.broadcasted_iota(jnp.int32, (MOBA_BLOCK, LANES), 1)
    blk = lax.broadcasted_iota(jnp.int32, (NI, MOBA_BLOCK), 0).astype(F32)
    i_f = i.astype(F32)
    for c in range(N_PAIRS):
        qc = q_ref[c].astype(F32)
        qm_ref[2 * c] = jnp.where(lane < HEAD_DIM, qc, 0.0).astype(BF16)
        qm_ref[2 * c + 1] = jnp.where(lane < HEAD_DIM, 0.0, qc).astype(BF16)

    def scores(j, h, bias_ref):
        kj = k_ref[h // 2, pl.ds(pl.multiple_of(j * MOBA_BLOCK, MOBA_BLOCK), MOBA_BLOCK), :]
        s = lax.dot_general(kj, qm_ref[h], NT_DIMS, preferred_element_type=F32)
        return s if bias_ref is None else s + bias_ref[h]

    def values(j, h):
        return vt_ref[j, h * V_ROWS:(h + 1) * V_ROWS, :]

    for h in range(N_HEADS):
        gate = lax.dot_general(km_ref[0, h // 2], qm_ref[h], NT_DIMS, preferred_element_type=F32)
        sel_ref[h] = _top_k_blocks(gate, i_f, blk, 0)

    d_far = _t5_bucket_starts(2 * MOBA_BLOCK)[-1][1]
    k0 = (MOBA_BLOCK - d_far + 1) // 8 * 8
    assert d_far - 1 <= LANES
    LA = SCORE_LOOKAHEAD

    LAST = N_HEADS - 1

    def finish_last(j_prev):
        acc_ref[LAST] = alpha_scr[...] * acc_ref[LAST] + jnp.dot(values(j_prev, LAST), p_scr[...],
                                                                preferred_element_type=F32)

    def key_block(j, j_prev, j_next, own):
        local = {h: scores(j, h, bo_ref) for h in range(LA)} if own else {}
        near = None if own else (j == i - 1).astype(F32)
        if not own:
            finish_last(j_prev)
        for h in range(N_HEADS):
            t = h + LA
            if t < N_HEADS:
                local[t] = scores(j, t, bo_ref if own else None)
            else:
                s_scr[t - N_HEADS] = scores(j_next, t - N_HEADS, None)
            s = local.pop(h) if h in local else s_scr[h]
            if own:
                m_new = jnp.max(s, axis=0, keepdims=True)
                alpha = None
                p = jnp.exp2(s - m_new)
            else:
                corner = s[k0:, :LANES] + near * bp_ref[h, k0:, :LANES]
                s = jnp.concatenate([s[:k0], jnp.concatenate([corner, s[k0:, LANES:]], axis=1)], axis=0)
                selj = sel_ref[h, pl.ds(j, 1), :] > 0.0
                m_old = m_ref[h]
                m_new = jnp.where(selj, jnp.maximum(m_old, jnp.max(s, axis=0, keepdims=True)), m_old)
                alpha = jnp.exp2(m_old - m_new)
                p = jnp.exp2(s - jnp.where(selj, m_new, BIG))
            m_ref[h] = m_new
            pb = p.astype(BF16)
            if h == LAST:
                p_scr[...] = pb
                alpha_scr[...] = jnp.zeros_like(m_new) if own else alpha
                if own:
                    acc_ref[h] = jnp.zeros(acc_ref.shape[1:], F32)
            elif own:
                acc_ref[h] = jnp.dot(values(j, h), pb, preferred_element_type=F32)
            else:
                acc_ref[h] = alpha * acc_ref[h] + jnp.dot(values(j, h), pb, preferred_element_type=F32)

    key_block(i, None, 0, True)

    def past_pair(jj, carry):
        j = 2 * jj
        key_block(j, jnp.where(j == 0, i, j - 1), j + 1, False)
        key_block(j + 1, j, j + 2, False)
        return carry

    lax.fori_loop(0, i // 2, past_pair, 0)

    @pl.when(i % 2 == 1)
    def _():
        key_block(i - 1, jnp.where(i == 1, i, i - 2), i, False)
    finish_last(jnp.maximum(i - 1, 0))
    for h in range(N_HEADS):
        o = acc_ref[h]
        ot_ref[h * HEAD_DIM:(h + 1) * HEAD_DIM, :] = o[:HEAD_DIM] / o[HEAD_DIM:HEAD_DIM + 1]
    out_ref[...] = (ot_ref[...].T * _silu(ga_ref[...])).astype(out_ref.dtype)


def _moba_prompt(rel_bias, qpm, kpm, vt, km_pm, ga, *, B, S):
    NI = S // MOBA_BLOCK
    return pl.pallas_call(
        functools.partial(_moba_kernel, NI=NI),
        out_shape=jax.ShapeDtypeStruct((B * S, D_GRP), BF16),
        grid=(B, NI),
        in_specs=[
            pl.BlockSpec(memory_space=pltpu.SMEM),
            pl.BlockSpec((N_PAIRS, MOBA_BLOCK, LANES), lambda b, i: (0, b * NI + i, 0)),
            pl.BlockSpec((N_PAIRS, S, LANES), lambda b, i: (0, b, 0)),
            pl.BlockSpec((NI, N_HEADS * V_ROWS, MOBA_BLOCK), lambda b, i: (b, 0, 0)),
            pl.BlockSpec((1, N_PAIRS, NI, LANES), lambda b, i: (b, 0, 0, 0)),
            pl.BlockSpec((MOBA_BLOCK, D_GRP), lambda b, i: (b * NI + i, 0)),
        ],
        out_specs=pl.BlockSpec((MOBA_BLOCK, D_GRP), lambda b, i: (b * NI + i, 0)),
        scratch_shapes=[
            pltpu.VMEM((N_HEADS, MOBA_BLOCK, LANES), BF16),
            pltpu.VMEM((N_HEADS, V_ROWS, MOBA_BLOCK), F32),
            pltpu.VMEM((N_HEADS, 1, MOBA_BLOCK), F32),
            pltpu.VMEM((N_HEADS, NI, MOBA_BLOCK), F32),
            pltpu.VMEM((D_GRP, MOBA_BLOCK), F32),
            pltpu.VMEM((N_HEADS, MOBA_BLOCK, MOBA_BLOCK), F32),
            pltpu.VMEM((N_HEADS, MOBA_BLOCK, MOBA_BLOCK), F32),
            pltpu.VMEM((SCORE_LOOKAHEAD, MOBA_BLOCK, MOBA_BLOCK), F32),
            pltpu.VMEM((MOBA_BLOCK, MOBA_BLOCK), BF16),
            pltpu.VMEM((1, MOBA_BLOCK), F32),
        ],
        compiler_params=pltpu.CompilerParams(
            dimension_semantics=("arbitrary", "arbitrary"), vmem_limit_bytes=VMEM_LIMIT),
        name="moba_prompt",
    )(rel_bias, qpm, kpm, vt, km_pm, ga)


def _sample_kernel(pt_ref, *refs, NP, NS, NB, TQ):
    (kc_hbm, vc_hbm, q_ref, kn_ref, vn_ref, ga_ref, blast_ref, bnew_ref, out_ref,
     s_ref, kmt_ref, bmax_ref, meff_ref, m_ref, l_ref, acc_ref, qbd_ref, pages_ref, sem) = refs
    b = pl.program_id(0)
    s = pl.program_id(1)
    n_b = pl.num_programs(0)
    HQ = N_HEADS * TQ
    bps = NP * PAGE_SIZE // MOBA_BLOCK
    lane_f = lax.broadcasted_iota(jnp.int32, (D_GRP, LANES), 1)
    lane_q = lax.broadcasted_iota(jnp.int32, (HQ, LANES), 1)
    slot = s % 2

    def page_copy(cache_hbm, bb, ss, t, dst_slot):
        page = pt_ref[bb, NP * (ss % NS) + t]
        return pltpu.make_async_copy(cache_hbm.at[page], pages_ref.at[dst_slot, t], sem.at[dst_slot])

    def start_step(bb, ss, dst_slot):
        @pl.when(ss < NS)
        def _():
            for t in range(NP):
                page_copy(kc_hbm, bb, ss, t, dst_slot).start()

        @pl.when(ss >= NS)
        def _():
            for t in range(NP):
                page_copy(vc_hbm, bb, ss, t, dst_slot).start()

    @pl.when((b == 0) & (s == 0))
    def _():
        start_step(b, s, slot)

    last_of_batch = s == 2 * NS - 1

    @pl.when(jnp.logical_not(last_of_batch & (b == n_b - 1)))
    def _():
        start_step(jnp.where(last_of_batch, b + 1, b), jnp.where(last_of_batch, 0, s + 1), 1 - slot)

    for t in range(NP):
        page_copy(kc_hbm, b, s, t, slot).wait()

    kp = vp = [pages_ref.at[slot, t] for t in range(NP)]

    @pl.when(s == 0)
    def _():
        qt = jnp.concatenate([q_ref[0]] * N_HEADS, axis=0)
        r = lax.broadcasted_iota(jnp.int32, (HQ, D_GRP), 0)
        l = lax.broadcasted_iota(jnp.int32, (HQ, D_GRP), 1)
        qbd_ref[...] = jnp.where(r // TQ == l // HEAD_DIM, qt, 0.0).astype(BF16)
        kmt_ref[...] = jnp.zeros(kmt_ref.shape, F32)
        bmax_ref[...] = jnp.zeros(bmax_ref.shape, F32)

    def block(pages, u):
        return jnp.concatenate([pages[2 * u][...], pages[2 * u + 1][...]], axis=1)

    @pl.when(s < NS)
    def _():
        for u in range(bps):
            jb = s * bps + u
            kt = block(kp, u)
            km = jnp.sum(kt, axis=1, keepdims=True) * (1.0 / MOBA_BLOCK)
            kmt_ref[...] = jnp.where(lane_f == jb, km, kmt_ref[...])
            sc = jnp.dot(qbd_ref[...], kt.astype(BF16), preferred_element_type=F32)
            if u == bps - 1:
                sc = sc + jnp.where(s == NS - 1, blast_ref[...], 0.0)
            s_ref[jb] = sc
            bmax_ref[...] = jnp.where(lane_q == jb, jnp.max(sc, axis=1, keepdims=True), bmax_ref[...])

    @pl.when(s == NS - 1)
    def _():
        gate = jnp.dot(qbd_ref[...], kmt_ref[...].astype(BF16), preferred_element_type=F32)
        sel = _top_k_blocks(gate, float(NB), lane_q.astype(F32), 1)
        pad = jnp.zeros((LANES - TQ, D_GRP), F32)
        kn = jnp.concatenate([kn_ref[0], pad], axis=0).astype(BF16)
        vn = jnp.concatenate([vn_ref[0], pad], axis=0).astype(BF16)
        sn = lax.dot_general(qbd_ref[...], kn, NT_DIMS, preferred_element_type=F32) + bnew_ref[...]
        m = jnp.maximum(jnp.max(jnp.where(sel > 0.0, bmax_ref[...], NEG), axis=1, keepdims=True),
                        jnp.max(sn, axis=1, keepdims=True))
        m_ref[...] = m
        meff_ref[...] = jnp.where(sel > 0.0, m, BIG)
        pn = jnp.exp2(sn - m)
        l_ref[...] = jnp.sum(pn, axis=1, keepdims=True)
        acc_ref[...] = jnp.dot(pn.astype(BF16), vn, preferred_element_type=F32)

    @pl.when(s >= NS)
    def _():
        for u in range(bps):
            jb = (s - NS) * bps + u
            vt = block(vp, u).astype(BF16)
            meff = jnp.sum(jnp.where(lane_q == jb, meff_ref[...], 0.0), axis=1, keepdims=True)
            p = jnp.exp2(s_ref[jb] - meff)
            l_ref[...] += jnp.sum(p, axis=1, keepdims=True)
            acc_ref[...] += lax.dot_general(p.astype(BF16), vt, NT_DIMS, preferred_element_type=F32)

    @pl.when(s == 2 * NS - 1)
    def _():
        o_bd = acc_ref[...] / l_ref[...]
        lane = lax.broadcasted_iota(jnp.int32, (TQ, D_GRP), 1)
        o = jnp.zeros((TQ, D_GRP), F32)
        for h in range(N_HEADS):
            o = jnp.where(lane // HEAD_DIM == h, o_bd[h * TQ:(h + 1) * TQ], o)
        out_ref[0] = o * _silu(ga_ref[0])


def _sample_attention(page_table, cache_kt, cache_vt, q, k_new, v_new, ga, bias_last, bias_new, *, NP):
    B, n_pages = page_table.shape
    TQ = q.shape[1]
    HQ = N_HEADS * TQ
    NS = n_pages // NP
    NB = n_pages * PAGE_SIZE // MOBA_BLOCK
    assert NB <= LANES and HQ <= LANES and TQ <= LANES
    per_b = pl.BlockSpec((1, TQ, D_GRP), lambda b, s, pt: (b, 0, 0))
    in_specs = [pl.BlockSpec(memory_space=pl.ANY), pl.BlockSpec(memory_space=pl.ANY),
                per_b, per_b, per_b, per_b,
                pl.BlockSpec((HQ, MOBA_BLOCK), lambda b, s, pt: (0, 0)),
                pl.BlockSpec((HQ, LANES), lambda b, s, pt: (0, 0))]
    grid_spec = pltpu.PrefetchScalarGridSpec(
        num_scalar_prefetch=1,
        grid=(B, 2 * NS),
        in_specs=in_specs,
        out_specs=per_b,
        scratch_shapes=[
            pltpu.VMEM((NB, HQ, MOBA_BLOCK), F32),
            pltpu.VMEM((D_GRP, LANES), F32),
            pltpu.VMEM((HQ, LANES), F32),
            pltpu.VMEM((HQ, LANES), F32),
            pltpu.VMEM((HQ, 1), F32),
            pltpu.VMEM((HQ, 1), F32),
            pltpu.VMEM((HQ, D_GRP), F32),
            pltpu.VMEM((HQ, D_GRP), BF16),
            pltpu.VMEM((2, NP, D_GRP, PAGE_SIZE), F32),
            pltpu.SemaphoreType.DMA((2,)),
        ],
    )
    return pl.pallas_call(
        functools.partial(_sample_kernel, NP=NP, NS=NS, NB=NB, TQ=TQ),
        out_shape=jax.ShapeDtypeStruct((B, TQ, D_GRP), F32),
        grid_spec=grid_spec,
        compiler_params=pltpu.CompilerParams(
            dimension_semantics=("arbitrary", "arbitrary"), vmem_limit_bytes=VMEM_LIMIT),
        name="sample_attention",
    )(page_table, cache_kt, cache_vt, q, k_new, v_new, ga, bias_last, bias_new)


def _out_kernel(x_ref, a_ref, r_ref, w_ref, y_ref):
    cat = jnp.concatenate([a_ref[...].astype(BF16), r_ref[...].astype(BF16)], axis=1)
    y_ref[...] = x_ref[...] + jnp.dot(cat, w_ref[...], preferred_element_type=F32)


def _out_proj(x, att, ret, w_bf, *, tm):
    R = x.shape[0]
    row = lambda i: (i, 0)
    return pl.pallas_call(
        _out_kernel,
        out_shape=jax.ShapeDtypeStruct((R, D_MODEL), F32),
        grid=(R // tm,),
        in_specs=[pl.BlockSpec((tm, D_MODEL), row), pl.BlockSpec((tm, D_GRP), row),
                  pl.BlockSpec((tm, D_GRP), row), pl.BlockSpec((2 * D_GRP, D_MODEL), lambda i: (0, 0))],
        out_specs=pl.BlockSpec((tm, D_MODEL), row),
        compiler_params=pltpu.CompilerParams(
            dimension_semantics=("arbitrary",), vmem_limit_bytes=VMEM_LIMIT),
        name="out_proj",
    )(x, att, ret, w_bf)


def _rope_tables(pos):
    half = HEAD_DIM // 2
    inv = ROPE_BASE ** (-jnp.arange(half, dtype=F32) / half)
    ang = pos.astype(F32)[:, None] * inv[None, :]
    cos, sin = jnp.cos(ang), jnp.sin(ang)
    return (jnp.tile(jnp.concatenate([cos, cos], axis=1), (1, 2)),
            jnp.tile(jnp.concatenate([-sin, sin], axis=1), (1, 2)))


def _t5_bucket(dist):
    max_exact = N_BUCKETS // 2
    d = jnp.maximum(dist, 1).astype(F32)
    large = max_exact + (jnp.log(d / max_exact) / math.log(MAX_DISTANCE / max_exact)
                         * (N_BUCKETS - max_exact)).astype(jnp.int32)
    large = jnp.minimum(large, N_BUCKETS - 1)
    return jnp.where(dist < max_exact, dist, large)


def _rel_bias_of(dist, rel_bias):
    rb = rel_bias.astype(F32)
    b = (rb[_t5_bucket(jnp.maximum(dist, 0))] - rb[N_BUCKETS - 1]) * LOG2E
    return jnp.where((dist >= 0)[..., None], b, NEG)


def _pair_states_to_blockdiag(state):
    B = state.shape[0]
    s5 = state.reshape(B, N_PAIRS, 2, HEAD_DIM, HEAD_DIM)
    top = jnp.pad(s5[:, :, 0], ((0, 0), (0, 0), (0, 0), (0, HEAD_DIM)))
    bot = jnp.pad(s5[:, :, 1], ((0, 0), (0, 0), (0, 0), (HEAD_DIM, 0)))
    return jnp.concatenate([top, bot], axis=2)


def _blockdiag_to_head_states(bd):
    B = bd.shape[0]
    d = jnp.stack([bd[:, :, :HEAD_DIM, :HEAD_DIM], bd[:, :, HEAD_DIM:, HEAD_DIM:]], axis=2)
    return d.reshape(B, N_HEADS, HEAD_DIM, HEAD_DIM)


def _feature_major_pages(cache):
    n_phys = cache.shape[0]
    return jnp.transpose(cache, (0, 2, 3, 1)).reshape(n_phys, D_GRP, PAGE_SIZE)


def _token_major_5d(xt, B, S):
    return jnp.transpose(xt.reshape(B, N_HEADS, HEAD_DIM, S), (0, 3, 1, 2))[None]


def kernel(x_prompt, x_sample, cache_k, cache_v, state_ret, page_table, ln_g, w_in,
           q_norm_g, k_norm_g, rel_bias, ret_norm_g, w_out):
    depth = ln_g.shape[0]
    assert depth == 1, "single-layer stack"
    B, S, _ = x_prompt.shape
    DB, DS, _ = x_sample.shape
    n_pages = page_table.shape[1]
    past_len = n_pages * PAGE_SIZE
    assert past_len % MOBA_BLOCK == 0 and DS <= MOBA_BLOCK

    w_in_bf = w_in[0].astype(BF16)
    w_out_bf = w_out[0].astype(BF16)
    lng = ln_g[0][None, :]
    qg = jnp.tile(q_norm_g[0], N_HEADS)[None, :]
    kg = jnp.tile(k_norm_g[0], N_HEADS)[None, :]
    gng = ret_norm_g[0][None, :]
    log_g = jnp.log1p(-jnp.exp2(-5.0 - jnp.arange(N_HEADS, dtype=F32)))

    xp = x_prompt.reshape(B * S, D_MODEL)
    cos_p, sin_p = _rope_tables(jnp.arange(S, dtype=jnp.int32))
    (qpm, kpm, kt, vt, vt_blk, km, ga, gr, qr, kr, vr) = _proj(
        xp, cos_p, sin_p, lng, w_in_bf, qg, kg, tm=MOBA_BLOCK, seq_len=S, feature_major=True)
    ret_p, rfin_p = _retention(qr, kr, vr, gr, jnp.zeros((B, N_PAIRS, LANES, LANES), F32),
                               _ret_tables(log_g, float(RET_CHUNK)), gng, B=B, S=S, T=1024, out_dtype=BF16)
    NI = S // MOBA_BLOCK
    km_pm = km.reshape(B, NI, N_PAIRS, LANES).transpose(0, 2, 1, 3).astype(BF16)
    att_p = _moba_prompt(rel_bias.astype(F32), qpm, kpm, vt_blk, km_pm, ga, B=B, S=S)
    y_p = _out_proj(xp, att_p, ret_p, w_out_bf, tm=512).reshape(B, S, D_MODEL)

    RS = DB * DS
    xs = x_sample.reshape(RS, D_MODEL)
    pos_s = past_len + jnp.arange(DS, dtype=jnp.int32)
    cos_s, sin_s = _rope_tables(jnp.tile(pos_s, DB))
    (qpm_s, _, ka_s, va_s, _, _, ga_s, gr_s, qr_s, kr_s, vr_s) = _proj(
        xs, cos_s, sin_s, lng, w_in_bf, qg, kg, tm=RS, seq_len=RS, feature_major=False)
    ret_s, rfin_s = _retention(qr_s, kr_s, vr_s, gr_s, _pair_states_to_blockdiag(state_ret[0].astype(F32)),
                               _ret_tables(log_g, float(DS)), gng, B=DB, S=DS, T=DS, out_dtype=F32)
    q_s = qpm_s.transpose(1, 0, 2).reshape(DB, DS, D_GRP).astype(F32)
    t = jnp.arange(DS, dtype=jnp.int32)
    kk = jnp.arange(MOBA_BLOCK, dtype=jnp.int32)
    b_last = _rel_bias_of(t[:, None] + (MOBA_BLOCK - kk)[None, :], rel_bias)
    b_last = b_last.transpose(2, 0, 1).reshape(N_HEADS * DS, MOBA_BLOCK)
    b_new = _rel_bias_of(t[:, None] - t[None, :], rel_bias)
    b_new = b_new.transpose(2, 0, 1).reshape(N_HEADS * DS, DS)
    b_new = jnp.pad(b_new, ((0, 0), (0, LANES - DS)), constant_values=NEG)
    att_s = _sample_attention(page_table, _feature_major_pages(cache_k[0]), _feature_major_pages(cache_v[0]),
                              q_s, ka_s.reshape(DB, DS, D_GRP), va_s.reshape(DB, DS, D_GRP),
                              ga_s.reshape(DB, DS, D_GRP), b_last, b_new,
                              NP=math.gcd(n_pages, 16))
    y_s = _out_proj(xs, att_s.reshape(RS, D_GRP), ret_s, w_out_bf, tm=RS).reshape(DB, DS, D_MODEL)

    sdt = state_ret.dtype
    return (y_p, y_s,
            _token_major_5d(kt, B, S), _token_major_5d(vt, B, S),
            _blockdiag_to_head_states(rfin_p).astype(sdt)[None],
            ka_s.reshape(1, DB, DS, N_HEADS, HEAD_DIM), va_s.reshape(1, DB, DS, N_HEADS, HEAD_DIM),
            _blockdiag_to_head_states(rfin_s).astype(sdt)[None])
```

```python
import functools
import math

import jax
import jax.numpy as jnp
import numpy as np
from jax import lax
from jax.experimental import pallas as pl
from jax.experimental.pallas import tpu as pltpu

D_MODEL = 1024
HEAD_DIM = 64
N_HEADS = 8
D_GRP = N_HEADS * HEAD_DIM
LANES = 128
N_PAIRS = D_GRP // LANES
MOBA_BLOCK = 256
MOBA_TOPK = 3
SCORE_LOOKAHEAD = 5
RET_CHUNK = 128
PAGE_SIZE = 128
N_BUCKETS = 32
MAX_DISTANCE = 128
ROPE_BASE = 10000.0
EPS = 1e-6
NEG = -1e30
BIG = 1e30
LOWEST = -3e38
LOG2E = math.log2(math.e)
Q_SCALE = HEAD_DIM ** -0.5 * LOG2E
V_ROWS = HEAD_DIM + 16
VMEM_LIMIT = 56 * 1024 * 1024

F32 = jnp.float32
BF16 = jnp.bfloat16
NT_DIMS = (((1,), (1,)), ((), ()))


def _silu(g):
    return g * (1.0 / (1.0 + jnp.exp(-g)))


def _head_mean(z):
    lane = lax.broadcasted_iota(jnp.int32, z.shape, 1)
    lo = lane < HEAD_DIM
    s0 = jnp.sum(jnp.where(lo, z, 0.0), axis=-1, keepdims=True)
    s1 = jnp.sum(jnp.where(lo, 0.0, z), axis=-1, keepdims=True)
    return jnp.where(lo, s0, s1) * (1.0 / HEAD_DIM)


def _top_k_blocks(gate, n_valid, blk, axis):
    n_blk = gate.shape[axis]
    g = jnp.where(blk < n_valid, gate, NEG)
    sel = jnp.zeros(gate.shape, F32)
    for _ in range(MOBA_TOPK):
        mx = jnp.max(g, axis=axis, keepdims=True)
        first = jnp.min(jnp.where(g == mx, blk, float(n_blk)), axis=axis, keepdims=True)
        hit = blk == first
        sel = jnp.where(hit, 1.0, sel)
        g = jnp.where(hit, LOWEST, g)
    return jnp.where(blk < n_valid, sel, 0.0)


def _proj_kernel(x_ref, cos_ref, sin_ref, lng_ref, w_ref, qg_ref, kg_ref,
                 qpm_ref, kpm_ref, k_ref, v_ref, vt_ref, km_ref, ga_ref, gr_ref,
                 qr_ref, kr_ref, vr_ref, *, feature_major):
    x = x_ref[...]
    ms = jnp.mean(x * x, axis=-1, keepdims=True)
    h = (x * lax.rsqrt(ms + EPS) * lng_ref[...]).astype(BF16)

    def seg(s):
        return jnp.dot(h, w_ref[:, s * D_GRP:(s + 1) * D_GRP], preferred_element_type=F32)

    def chunks(z):
        return [z[:, c * LANES:(c + 1) * LANES] for c in range(N_PAIRS)]

    qa = seg(0)
    for c, z in enumerate(chunks(qa)):
        g = qg_ref[:, c * LANES:(c + 1) * LANES]
        zn = z * lax.rsqrt(_head_mean(z * z) + EPS) * g
        qpm_ref[c] = (zn * Q_SCALE).astype(BF16)
    ka = seg(1)
    kn = []
    for c, z in enumerate(chunks(ka)):
        g = kg_ref[:, c * LANES:(c + 1) * LANES]
        zn = z * lax.rsqrt(_head_mean(z * z) + EPS) * g
        kpm_ref[c] = zn.astype(BF16)
        kn.append(zn)
    kn = jnp.concatenate(kn, axis=1)
    for b in range(km_ref.shape[1]):
        km_ref[0, b:b + 1, :] = jnp.mean(kn[b * MOBA_BLOCK:(b + 1) * MOBA_BLOCK], axis=0, keepdims=True)
    va = seg(2)
    vat = va.T
    ones = jnp.ones((V_ROWS - HEAD_DIM, MOBA_BLOCK), F32)
    for b in range(vt_ref.shape[0]):
        vb = vat[:, b * MOBA_BLOCK:(b + 1) * MOBA_BLOCK]
        vt_ref[b] = jnp.concatenate(
            [t for h in range(N_HEADS) for t in (vb[h * HEAD_DIM:(h + 1) * HEAD_DIM], ones)],
            axis=0).astype(BF16)
    if feature_major:
        k_ref[0] = kn.T
        v_ref[0] = vat
    else:
        k_ref[...] = kn
        v_ref[...] = va
    ga_ref[...] = seg(3)

    cos = cos_ref[...]
    sin = sin_ref[...]
    lane = lax.broadcasted_iota(jnp.int32, cos.shape, 1)
    first_half = (lane % HEAD_DIM) < (HEAD_DIM // 2)

    def rope(z):
        swapped = jnp.where(first_half,
                            pltpu.roll(z, LANES - HEAD_DIM // 2, 1),
                            pltpu.roll(z, HEAD_DIM // 2, 1))
        return z * cos + swapped * sin

    qr = seg(4)
    for c, z in enumerate(chunks(qr)):
        qr_ref[:, c * LANES:(c + 1) * LANES] = rope(z)
    kr = seg(5)
    for c, z in enumerate(chunks(kr)):
        kr_ref[:, c * LANES:(c + 1) * LANES] = rope(z) * (HEAD_DIM ** -0.5)
    vr_ref[...] = seg(6)
    gr_ref[...] = seg(7)


def _proj(x, cos, sin, ln_g, w_bf, qg, kg, *, tm, seq_len, feature_major):
    R = x.shape[0]
    assert tm % MOBA_BLOCK == 0 and R % tm == 0 and cos.shape[0] % tm == 0 and seq_len % tm == 0
    n_pos = cos.shape[0] // tm
    nb = tm // MOBA_BLOCK
    nt = seq_len // tm
    row = lambda i: (i, 0)
    full = lambda i: (0, 0)
    wide = pl.BlockSpec((tm, D_GRP), row)
    wide_shape = jax.ShapeDtypeStruct((R, D_GRP), F32)
    if feature_major:
        kv_shape = jax.ShapeDtypeStruct((R // seq_len, D_GRP, seq_len), F32)
        kv_spec = pl.BlockSpec((1, D_GRP, tm), lambda i: (i // nt, 0, i % nt))
    else:
        kv_shape, kv_spec = wide_shape, wide
    out_shape = (
        jax.ShapeDtypeStruct((N_PAIRS, R, LANES), BF16),
        jax.ShapeDtypeStruct((N_PAIRS, R, LANES), BF16),
        kv_shape,
        kv_shape,
        jax.ShapeDtypeStruct((R // MOBA_BLOCK, N_HEADS * V_ROWS, MOBA_BLOCK), BF16),
        jax.ShapeDtypeStruct((R // tm, nb, D_GRP), F32),
        wide_shape,
        wide_shape,
        wide_shape,
        wide_shape,
        wide_shape,
    )
    pm = pl.BlockSpec((N_PAIRS, tm, LANES), lambda i: (0, i, 0))
    out_specs = (
        pm, pm, kv_spec, kv_spec,
        pl.BlockSpec((nb, N_HEADS * V_ROWS, MOBA_BLOCK), lambda i: (i, 0, 0)),
        pl.BlockSpec((1, nb, D_GRP), lambda i: (i, 0, 0)),
        wide, wide, wide, wide, wide,
    )
    in_specs = [
        pl.BlockSpec((tm, D_MODEL), row),
        pl.BlockSpec((tm, LANES), lambda i: (i % n_pos, 0)),
        pl.BlockSpec((tm, LANES), lambda i: (i % n_pos, 0)),
        pl.BlockSpec((1, D_MODEL), full),
        pl.BlockSpec((D_MODEL, 8 * D_GRP), full),
        pl.BlockSpec((1, D_GRP), full),
        pl.BlockSpec((1, D_GRP), full),
    ]
    return pl.pallas_call(
        functools.partial(_proj_kernel, feature_major=feature_major),
        out_shape=out_shape,
        grid=(R // tm,),
        in_specs=in_specs,
        out_specs=out_specs,
        compiler_params=pltpu.CompilerParams(
            dimension_semantics=("arbitrary",), vmem_limit_bytes=VMEM_LIMIT),
        name="proj",
    )(x, cos, sin, ln_g, w_bf, qg, kg)


def _ret_kernel(q_ref, k_ref, v_ref, g_ref, r0_ref, d2_ref, qd_ref, kd_ref, gl_ref, gng_ref,
                out_ref, rfin_ref, r_scr, *, T, L):
    t = pl.program_id(1)

    @pl.when(t == 0)
    def _():
        r_scr[...] = r0_ref[0]

    lane = lax.broadcasted_iota(jnp.int32, (L, LANES), 1)
    lo = lane < HEAD_DIM
    ri = lax.broadcasted_iota(jnp.int32, (LANES, LANES), 0)
    ci = lax.broadcasted_iota(jnp.int32, (LANES, LANES), 1)
    same_head = (ri < HEAD_DIM) == (ci < HEAD_DIM)
    rows_c = min(T, L)

    def load(ref, rows, cols):
        z = ref[rows, cols]
        if rows_c < L:
            z = jnp.concatenate([z, jnp.zeros((L - rows_c, LANES), F32)], axis=0)
        return z

    def chunk(ci_, carry):
        if T > L:
            rows = pl.ds(pl.multiple_of(ci_ * L, L), L)
        else:
            rows = pl.ds(0, rows_c)
        staged = []
        for c in range(N_PAIRS):
            cols = slice(c * LANES, (c + 1) * LANES)
            q = load(q_ref, rows, cols)
            k = load(k_ref, rows, cols)
            vc = load(v_ref, rows, cols).astype(BF16)
            R = r_scr[c]
            qs = jnp.concatenate([jnp.where(lo, q, 0.0), jnp.where(lo, 0.0, q)], axis=0).astype(BF16)
            s = lax.dot_general(qs, k.astype(BF16), NT_DIMS, preferred_element_type=F32)
            cross = jnp.dot(q.astype(BF16), R.astype(BF16), preferred_element_type=F32)
            kdk = (k * kd_ref[:, cols]).T.astype(BF16)
            upd = jnp.dot(kdk, vc, preferred_element_type=F32)
            r_scr[c] = gl_ref[:, cols] * R + jnp.where(same_head, upd, 0.0)
            staged.append((cols, s, cross, vc))
        for c, (cols, s, cross, vc) in enumerate(staged):
            inner2 = jnp.dot((s * d2_ref[c]).astype(BF16), vc, preferred_element_type=F32)
            o = jnp.where(lo, inner2[:L], inner2[L:]) + cross * qd_ref[:, cols]
            mu = _head_mean(o)
            d = o - mu
            var = _head_mean(d * d)
            y = d * lax.rsqrt(var + EPS) * gng_ref[:, cols]
            res = (y * _silu(load(g_ref, rows, cols))).astype(out_ref.dtype)
            out_ref[rows, cols] = res[:rows_c]
        return carry

    lax.fori_loop(0, max(T // L, 1), chunk, 0)

    @pl.when(t == pl.num_programs(1) - 1)
    def _():
        rfin_ref[0] = r_scr[...]


def _retention(q, k, v, g, r0, tabs, gn_g, *, B, S, T, out_dtype):
    d2, qd, kd, gl = tabs
    L = RET_CHUNK
    nt = S // T
    tok = pl.BlockSpec((T, D_GRP), lambda b, t: (b * nt + t, 0))
    state = pl.BlockSpec((1, N_PAIRS, LANES, LANES), lambda b, t: (b, 0, 0, 0))
    const2 = lambda b, t: (0, 0)
    return pl.pallas_call(
        functools.partial(_ret_kernel, T=T, L=L),
        out_shape=(jax.ShapeDtypeStruct((B * S, D_GRP), out_dtype),
                   jax.ShapeDtypeStruct((B, N_PAIRS, LANES, LANES), F32)),
        grid=(B, nt),
        in_specs=[tok, tok, tok, tok, state,
                  pl.BlockSpec((N_PAIRS, 2 * L, L), lambda b, t: (0, 0, 0)),
                  pl.BlockSpec((L, D_GRP), const2), pl.BlockSpec((L, D_GRP), const2),
                  pl.BlockSpec((1, D_GRP), const2), pl.BlockSpec((1, D_GRP), const2)],
        out_specs=(tok, state),
        scratch_shapes=[pltpu.VMEM((N_PAIRS, LANES, LANES), F32)],
        compiler_params=pltpu.CompilerParams(
            dimension_semantics=("arbitrary", "arbitrary"), vmem_limit_bytes=VMEM_LIMIT),
        name="retention",
    )(q, k, v, g, r0, d2, qd, kd, gl, gn_g)


def _ret_tables(log_g, l_true):
    L = RET_CHUNK
    n = jnp.arange(L, dtype=F32)
    diff = n[:, None] - n[None, :]
    decay = jnp.where(diff >= 0, jnp.exp(log_g[:, None, None] * jnp.maximum(diff, 0.0)), 0.0)
    d2 = decay.reshape(N_PAIRS, 2 * L, L)
    per_lane = lambda t: jnp.repeat(t, HEAD_DIM, axis=1)
    qd = per_lane(jnp.exp(log_g[None, :] * (n[:, None] + 1.0)))
    kd = per_lane(jnp.exp(log_g[None, :] * (l_true - 1.0 - n)[:, None]))
    gl = per_lane(jnp.exp(log_g * l_true)[None, :])
    return d2, qd, kd, gl


def _t5_bucket_starts(max_dist):
    d = np.arange(max_dist)
    max_exact = N_BUCKETS // 2
    scaled = (np.log(np.maximum(d, 1).astype(np.float32) / np.float32(max_exact))
              / np.float32(math.log(MAX_DISTANCE / max_exact)) * np.float32(N_BUCKETS - max_exact))
    bucket = np.where(d < max_exact, d, np.minimum(max_exact + scaled.astype(np.int32), N_BUCKETS - 1))
    assert np.all(np.diff(bucket) >= 0)
    return [(int(b), int(d[bucket == b][0])) for b in np.unique(bucket)]


def _bias_tables(rb_ref, bo_ref, bp_ref):
    shape = (MOBA_BLOCK, MOBA_BLOCK)
    d_own = lax.broadcasted_iota(jnp.int32, shape, 1) - lax.broadcasted_iota(jnp.int32, shape, 0)
    starts = _t5_bucket_starts(2 * MOBA_BLOCK)
    assert starts[0] == (0, 0) and starts[-1][0] == N_BUCKETS - 1

    def head(h, carry):
        far = rb_ref[N_BUCKETS - 1, h]
        for ref, d in ((bo_ref, d_own), (bp_ref, d_own + MOBA_BLOCK)):
            val = jnp.full(shape, (rb_ref[0, h] - far) * LOG2E, F32)
            for bkt, lo in starts[1:]:
                val = jnp.where(d >= lo, (rb_ref[bkt, h] - far) * LOG2E, val)
            ref[h] = jnp.where(d >= 0, val, NEG)
        return carry

    lax.fori_loop(0, N_HEADS, head, 0)


def _moba_kernel(rb_ref, q_ref, k_ref, vt_ref, km_ref, ga_ref, out_ref,
                 qm_ref, acc_ref, m_ref, sel_ref, ot_ref, bo_ref, bp_ref, s_scr, p_scr, alpha_scr, *, NI):
    i = pl.program_id(1)

    @pl.when((pl.program_id(0) == 0) & (i == 0))
    def _():
        _bias_tables(rb_ref, bo_ref, bp_ref)

    lane = lax.broadcasted_iota(jnp.int32, (MOBA_BLOCK, LANES), 1)
    blk = lax.broadcasted_iota(jnp.int32, (NI, MOBA_BLOCK), 0).astype(F32)
    i_f = i.astype(F32)
    for c in range(N_PAIRS):
        qc = q_ref[c].astype(F32)
        qm_ref[2 * c] = jnp.where(lane < HEAD_DIM, qc, 0.0).astype(BF16)
        qm_ref[2 * c + 1] = jnp.where(lane < HEAD_DIM, 0.0, qc).astype(BF16)

    def scores(j, h, bias_ref):
        kj = k_ref[h // 2, pl.ds(pl.multiple_of(j * MOBA_BLOCK, MOBA_BLOCK), MOBA_BLOCK), :]
        s = lax.dot_general(kj, qm_ref[h], NT_DIMS, preferred_element_type=F32)
        return s if bias_ref is None else s + bias_ref[h]

    def values(j, h):
        return vt_ref[j, h * V_ROWS:(h + 1) * V_ROWS, :]

    for h in range(N_HEADS):
        gate = lax.dot_general(km_ref[0, h // 2], qm_ref[h], NT_DIMS, preferred_element_type=F32)
        sel_ref[h] = _top_k_blocks(gate, i_f, blk, 0)

    d_far = _t5_bucket_starts(2 * MOBA_BLOCK)[-1][1]
    k0 = (MOBA_BLOCK - d_far + 1) // 8 * 8
    assert d_far - 1 <= LANES
    LA = SCORE_LOOKAHEAD

    LAST = N_HEADS - 1

    def finish_last(j_prev):
        acc_ref[LAST] = alpha_scr[...] * acc_ref[LAST] + jnp.dot(values(j_prev, LAST), p_scr[...],
                                                                preferred_element_type=F32)

    def key_block(j, j_prev, j_next, own):
        local = {h: scores(j, h, bo_ref) for h in range(LA)} if own else {}
        near = None if own else (j == i - 1).astype(F32)
        if not own:
            finish_last(j_prev)
        for h in range(N_HEADS):
            t = h + LA
            if t < N_HEADS:
                local[t] = scores(j, t, bo_ref if own else None)
            else:
                s_scr[t - N_HEADS] = scores(j_next, t - N_HEADS, None)
            s = local.pop(h) if h in local else s_scr[h]
            if own:
                m_new = jnp.max(s, axis=0, keepdims=True)
                alpha = None
                p = jnp.exp2(s - m_new)
            else:
                corner = s[k0:, :LANES] + near * bp_ref[h, k0:, :LANES]
                s = jnp.concatenate([s[:k0], jnp.concatenate([corner, s[k0:, LANES:]], axis=1)], axis=0)
                selj = sel_ref[h, pl.ds(j, 1), :] > 0.0
                m_old = m_ref[h]
                m_new = jnp.where(selj, jnp.maximum(m_old, jnp.max(s, axis=0, keepdims=True)), m_old)
                alpha = jnp.exp2(m_old - m_new)
                p = jnp.exp2(s - jnp.where(selj, m_new, BIG))
            m_ref[h] = m_new
            pb = p.astype(BF16)
            if h == LAST:
                p_scr[...] = pb
                alpha_scr[...] = jnp.zeros_like(m_new) if own else alpha
                if own:
                    acc_ref[h] = jnp.zeros(acc_ref.shape[1:], F32)
            elif own:
                acc_ref[h] = jnp.dot(values(j, h), pb, preferred_element_type=F32)
            else:
                acc_ref[h] = alpha * acc_ref[h] + jnp.dot(values(j, h), pb, preferred_element_type=F32)

    key_block(i, None, 0, True)

    def past_pair(jj, carry):
        j = 2 * jj
        key_block(j, jnp.where(j == 0, i, j - 1), j + 1, False)
        key_block(j + 1, j, j + 2, False)
        return carry

    lax.fori_loop(0, i // 2, past_pair, 0)

    @pl.when(i % 2 == 1)
    def _():
        key_block(i - 1, jnp.where(i == 1, i, i - 2), i, False)
    finish_last(jnp.maximum(i - 1, 0))
    for h in range(N_HEADS):
        o = acc_ref[h]
        ot_ref[h * HEAD_DIM:(h + 1) * HEAD_DIM, :] = o[:HEAD_DIM] / o[HEAD_DIM:HEAD_DIM + 1]
    out_ref[...] = (ot_ref[...].T * _silu(ga_ref[...])).astype(out_ref.dtype)


def _moba_prompt(rel_bias, qpm, kpm, vt, km_pm, ga, *, B, S):
    NI = S // MOBA_BLOCK
    return pl.pallas_call(
        functools.partial(_moba_kernel, NI=NI),
        out_shape=jax.ShapeDtypeStruct((B * S, D_GRP), BF16),
        grid=(B, NI),
        in_specs=[
            pl.BlockSpec(memory_space=pltpu.SMEM),
            pl.BlockSpec((N_PAIRS, MOBA_BLOCK, LANES), lambda b, i: (0, b * NI + i, 0)),
            pl.BlockSpec((N_PAIRS, S, LANES), lambda b, i: (0, b, 0)),
            pl.BlockSpec((NI, N_HEADS * V_ROWS, MOBA_BLOCK), lambda b, i: (b, 0, 0)),
            pl.BlockSpec((1, N_PAIRS, NI, LANES), lambda b, i: (b, 0, 0, 0)),
            pl.BlockSpec((MOBA_BLOCK, D_GRP), lambda b, i: (b * NI + i, 0)),
        ],
        out_specs=pl.BlockSpec((MOBA_BLOCK, D_GRP), lambda b, i: (b * NI + i, 0)),
        scratch_shapes=[
            pltpu.VMEM((N_HEADS, MOBA_BLOCK, LANES), BF16),
            pltpu.VMEM((N_HEADS, V_ROWS, MOBA_BLOCK), F32),
            pltpu.VMEM((N_HEADS, 1, MOBA_BLOCK), F32),
            pltpu.VMEM((N_HEADS, NI, MOBA_BLOCK), F32),
            pltpu.VMEM((D_GRP, MOBA_BLOCK), F32),
            pltpu.VMEM((N_HEADS, MOBA_BLOCK, MOBA_BLOCK), F32),
            pltpu.VMEM((N_HEADS, MOBA_BLOCK, MOBA_BLOCK), F32),
            pltpu.VMEM((SCORE_LOOKAHEAD, MOBA_BLOCK, MOBA_BLOCK), F32),
            pltpu.VMEM((MOBA_BLOCK, MOBA_BLOCK), BF16),
            pltpu.VMEM((1, MOBA_BLOCK), F32),
        ],
        compiler_params=pltpu.CompilerParams(
            dimension_semantics=("arbitrary", "arbitrary"), vmem_limit_bytes=VMEM_LIMIT),
        name="moba_prompt",
    )(rel_bias, qpm, kpm, vt, km_pm, ga)


def _sample_kernel(pt_ref, *refs, NP, NS, NB, TQ):
    (kc_hbm, vc_hbm, q_ref, kn_ref, vn_ref, ga_ref, blast_ref, bnew_ref, out_ref,
     s_ref, kmt_ref, bmax_ref, meff_ref, m_ref, l_ref, acc_ref, qbd_ref, pages_ref, sem) = refs
    b = pl.program_id(0)
    s = pl.program_id(1)
    n_b = pl.num_programs(0)
    HQ = N_HEADS * TQ
    bps = NP * PAGE_SIZE // MOBA_BLOCK
    lane_f = lax.broadcasted_iota(jnp.int32, (D_GRP, LANES), 1)
    lane_q = lax.broadcasted_iota(jnp.int32, (HQ, LANES), 1)
    slot = s % 2

    def page_copy(cache_hbm, bb, ss, t, dst_slot):
        page = pt_ref[bb, NP * (ss % NS) + t]
        return pltpu.make_async_copy(cache_hbm.at[page], pages_ref.at[dst_slot, t], sem.at[dst_slot])

    def start_step(bb, ss, dst_slot):
        @pl.when(ss < NS)
        def _():
            for t in range(NP):
                page_copy(kc_hbm, bb, ss, t, dst_slot).start(priority=t % 2)

        @pl.when(ss >= NS)
        def _():
            for t in range(NP):
                page_copy(vc_hbm, bb, ss, t, dst_slot).start(priority=t % 2)

    @pl.when((b == 0) & (s == 0))
    def _():
        start_step(b, s, slot)

    last_of_batch = s == 2 * NS - 1

    @pl.when(jnp.logical_not(last_of_batch & (b == n_b - 1)))
    def _():
        start_step(jnp.where(last_of_batch, b + 1, b), jnp.where(last_of_batch, 0, s + 1), 1 - slot)

    for t in range(NP):
        page_copy(kc_hbm, b, s, t, slot).wait()

    kp = vp = [pages_ref.at[slot, t] for t in range(NP)]

    @pl.when(s == 0)
    def _():
        qt = jnp.concatenate([q_ref[0]] * N_HEADS, axis=0)
        r = lax.broadcasted_iota(jnp.int32, (HQ, D_GRP), 0)
        l = lax.broadcasted_iota(jnp.int32, (HQ, D_GRP), 1)
        qbd_ref[...] = jnp.where(r // TQ == l // HEAD_DIM, qt, 0.0).astype(BF16)
        kmt_ref[...] = jnp.zeros(kmt_ref.shape, F32)
        bmax_ref[...] = jnp.zeros(bmax_ref.shape, F32)

    def block(pages, u):
        return jnp.concatenate([pages[2 * u][...], pages[2 * u + 1][...]], axis=1)

    @pl.when(s < NS)
    def _():
        for u in range(bps):
            jb = s * bps + u
            kt = block(kp, u)
            km = jnp.sum(kt, axis=1, keepdims=True) * (1.0 / MOBA_BLOCK)
            kmt_ref[...] = jnp.where(lane_f == jb, km, kmt_ref[...])
            sc = jnp.dot(qbd_ref[...], kt.astype(BF16), preferred_element_type=F32)
            if u == bps - 1:
                sc = sc + jnp.where(s == NS - 1, blast_ref[...], 0.0)
            s_ref[jb] = sc
            bmax_ref[...] = jnp.where(lane_q == jb, jnp.max(sc, axis=1, keepdims=True), bmax_ref[...])

    @pl.when(s == NS - 1)
    def _():
        gate = jnp.dot(qbd_ref[...], kmt_ref[...].astype(BF16), preferred_element_type=F32)
        sel = _top_k_blocks(gate, float(NB), lane_q.astype(F32), 1)
        pad = jnp.zeros((LANES - TQ, D_GRP), F32)
        kn = jnp.concatenate([kn_ref[0], pad], axis=0).astype(BF16)
        vn = jnp.concatenate([vn_ref[0], pad], axis=0).astype(BF16)
        sn = lax.dot_general(qbd_ref[...], kn, NT_DIMS, preferred_element_type=F32) + bnew_ref[...]
        m = jnp.maximum(jnp.max(jnp.where(sel > 0.0, bmax_ref[...], NEG), axis=1, keepdims=True),
                        jnp.max(sn, axis=1, keepdims=True))
        m_ref[...] = m
        meff_ref[...] = jnp.where(sel > 0.0, m, BIG)
        pn = jnp.exp2(sn - m)
        l_ref[...] = jnp.sum(pn, axis=1, keepdims=True)
        acc_ref[...] = jnp.dot(pn.astype(BF16), vn, preferred_element_type=F32)

    @pl.when(s >= NS)
    def _():
        for u in range(bps):
            jb = (s - NS) * bps + u
            vt = block(vp, u).astype(BF16)
            meff = jnp.sum(jnp.where(lane_q == jb, meff_ref[...], 0.0), axis=1, keepdims=True)
            p = jnp.exp2(s_ref[jb] - meff)
            l_ref[...] += jnp.sum(p, axis=1, keepdims=True)
            acc_ref[...] += lax.dot_general(p.astype(BF16), vt, NT_DIMS, preferred_element_type=F32)

    @pl.when(s == 2 * NS - 1)
    def _():
        o_bd = acc_ref[...] / l_ref[...]
        lane = lax.broadcasted_iota(jnp.int32, (TQ, D_GRP), 1)
        o = jnp.zeros((TQ, D_GRP), F32)
        for h in range(N_HEADS):
            o = jnp.where(lane // HEAD_DIM == h, o_bd[h * TQ:(h + 1) * TQ], o)
        out_ref[0] = o * _silu(ga_ref[0])


def _sample_attention(page_table, cache_kt, cache_vt, q, k_new, v_new, ga, bias_last, bias_new, *, NP):
    B, n_pages = page_table.shape
    TQ = q.shape[1]
    HQ = N_HEADS * TQ
    NS = n_pages // NP
    NB = n_pages * PAGE_SIZE // MOBA_BLOCK
    assert NB <= LANES and HQ <= LANES and TQ <= LANES
    per_b = pl.BlockSpec((1, TQ, D_GRP), lambda b, s, pt: (b, 0, 0))
    in_specs = [pl.BlockSpec(memory_space=pl.ANY), pl.BlockSpec(memory_space=pl.ANY),
                per_b, per_b, per_b, per_b,
                pl.BlockSpec((HQ, MOBA_BLOCK), lambda b, s, pt: (0, 0)),
                pl.BlockSpec((HQ, LANES), lambda b, s, pt: (0, 0))]
    grid_spec = pltpu.PrefetchScalarGridSpec(
        num_scalar_prefetch=1,
        grid=(B, 2 * NS),
        in_specs=in_specs,
        out_specs=per_b,
        scratch_shapes=[
            pltpu.VMEM((NB, HQ, MOBA_BLOCK), F32),
            pltpu.VMEM((D_GRP, LANES), F32),
            pltpu.VMEM((HQ, LANES), F32),
            pltpu.VMEM((HQ, LANES), F32),
            pltpu.VMEM((HQ, 1), F32),
            pltpu.VMEM((HQ, 1), F32),
            pltpu.VMEM((HQ, D_GRP), F32),
            pltpu.VMEM((HQ, D_GRP), BF16),
            pltpu.VMEM((2, NP, D_GRP, PAGE_SIZE), F32),
            pltpu.SemaphoreType.DMA((2,)),
        ],
    )
    return pl.pallas_call(
        functools.partial(_sample_kernel, NP=NP, NS=NS, NB=NB, TQ=TQ),
        out_shape=jax.ShapeDtypeStruct((B, TQ, D_GRP), F32),
        grid_spec=grid_spec,
        compiler_params=pltpu.CompilerParams(
            dimension_semantics=("arbitrary", "arbitrary"), vmem_limit_bytes=VMEM_LIMIT),
        name="sample_attention",
    )(page_table, cache_kt, cache_vt, q, k_new, v_new, ga, bias_last, bias_new)


def _out_kernel(x_ref, a_ref, r_ref, w_ref, y_ref):
    cat = jnp.concatenate([a_ref[...].astype(BF16), r_ref[...].astype(BF16)], axis=1)
    y_ref[...] = x_ref[...] + jnp.dot(cat, w_ref[...], preferred_element_type=F32)


def _out_proj(x, att, ret, w_bf, *, tm):
    R = x.shape[0]
    row = lambda i: (i, 0)
    return pl.pallas_call(
        _out_kernel,
        out_shape=jax.ShapeDtypeStruct((R, D_MODEL), F32),
        grid=(R // tm,),
        in_specs=[pl.BlockSpec((tm, D_MODEL), row), pl.BlockSpec((tm, D_GRP), row),
                  pl.BlockSpec((tm, D_GRP), row), pl.BlockSpec((2 * D_GRP, D_MODEL), lambda i: (0, 0))],
        out_specs=pl.BlockSpec((tm, D_MODEL), row),
        compiler_params=pltpu.CompilerParams(
            dimension_semantics=("arbitrary",), vmem_limit_bytes=VMEM_LIMIT),
        name="out_proj",
    )(x, att, ret, w_bf)


def _rope_tables(pos):
    half = HEAD_DIM // 2
    inv = ROPE_BASE ** (-jnp.arange(half, dtype=F32) / half)
    ang = pos.astype(F32)[:, None] * inv[None, :]
    cos, sin = jnp.cos(ang), jnp.sin(ang)
    return (jnp.tile(jnp.concatenate([cos, cos], axis=1), (1, 2)),
            jnp.tile(jnp.concatenate([-sin, sin], axis=1), (1, 2)))


def _t5_bucket(dist):
    max_exact = N_BUCKETS // 2
    d = jnp.maximum(dist, 1).astype(F32)
    large = max_exact + (jnp.log(d / max_exact) / math.log(MAX_DISTANCE / max_exact)
                         * (N_BUCKETS - max_exact)).astype(jnp.int32)
    large = jnp.minimum(large, N_BUCKETS - 1)
    return jnp.where(dist < max_exact, dist, large)


def _rel_bias_of(dist, rel_bias):
    rb = rel_bias.astype(F32)
    b = (rb[_t5_bucket(jnp.maximum(dist, 0))] - rb[N_BUCKETS - 1]) * LOG2E
    return jnp.where((dist >= 0)[..., None], b, NEG)


def _pair_states_to_blockdiag(state):
    B = state.shape[0]
    s5 = state.reshape(B, N_PAIRS, 2, HEAD_DIM, HEAD_DIM)
    top = jnp.pad(s5[:, :, 0], ((0, 0), (0, 0), (0, 0), (0, HEAD_DIM)))
    bot = jnp.pad(s5[:, :, 1], ((0, 0), (0, 0), (0, 0), (HEAD_DIM, 0)))
    return jnp.concatenate([top, bot], axis=2)


def _blockdiag_to_head_states(bd):
    B = bd.shape[0]
    d = jnp.stack([bd[:, :, :HEAD_DIM, :HEAD_DIM], bd[:, :, HEAD_DIM:, HEAD_DIM:]], axis=2)
    return d.reshape(B, N_HEADS, HEAD_DIM, HEAD_DIM)


def _feature_major_pages(cache):
    n_phys = cache.shape[0]
    return jnp.transpose(cache, (0, 2, 3, 1)).reshape(n_phys, D_GRP, PAGE_SIZE)


def _token_major_5d(xt, B, S):
    return jnp.transpose(xt.reshape(B, N_HEADS, HEAD_DIM, S), (0, 3, 1, 2))[None]


def kernel(x_prompt, x_sample, cache_k, cache_v, state_ret, page_table, ln_g, w_in,
           q_norm_g, k_norm_g, rel_bias, ret_norm_g, w_out):
    depth = ln_g.shape[0]
    assert depth == 1, "single-layer stack"
    B, S, _ = x_prompt.shape
    DB, DS, _ = x_sample.shape
    n_pages = page_table.shape[1]
    past_len = n_pages * PAGE_SIZE
    assert past_len % MOBA_BLOCK == 0 and DS <= MOBA_BLOCK

    w_in_bf = w_in[0].astype(BF16)
    w_out_bf = w_out[0].astype(BF16)
    lng = ln_g[0][None, :]
    qg = jnp.tile(q_norm_g[0], N_HEADS)[None, :]
    kg = jnp.tile(k_norm_g[0], N_HEADS)[None, :]
    gng = ret_norm_g[0][None, :]
    log_g = jnp.log1p(-jnp.exp2(-5.0 - jnp.arange(N_HEADS, dtype=F32)))

    xp = x_prompt.reshape(B * S, D_MODEL)
    cos_p, sin_p = _rope_tables(jnp.arange(S, dtype=jnp.int32))
    (qpm, kpm, kt, vt, vt_blk, km, ga, gr, qr, kr, vr) = _proj(
        xp, cos_p, sin_p, lng, w_in_bf, qg, kg, tm=MOBA_BLOCK, seq_len=S, feature_major=True)
    ret_p, rfin_p = _retention(qr, kr, vr, gr, jnp.zeros((B, N_PAIRS, LANES, LANES), F32),
                               _ret_tables(log_g, float(RET_CHUNK)), gng, B=B, S=S, T=1024, out_dtype=BF16)
    NI = S // MOBA_BLOCK
    km_pm = km.reshape(B, NI, N_PAIRS, LANES).transpose(0, 2, 1, 3).astype(BF16)
    att_p = _moba_prompt(rel_bias.astype(F32), qpm, kpm, vt_blk, km_pm, ga, B=B, S=S)
    y_p = _out_proj(xp, att_p, ret_p, w_out_bf, tm=512).reshape(B, S, D_MODEL)

    RS = DB * DS
    xs = x_sample.reshape(RS, D_MODEL)
    pos_s = past_len + jnp.arange(DS, dtype=jnp.int32)
    cos_s, sin_s = _rope_tables(jnp.tile(pos_s, DB))
    (qpm_s, _, ka_s, va_s, _, _, ga_s, gr_s, qr_s, kr_s, vr_s) = _proj(
        xs, cos_s, sin_s, lng, w_in_bf, qg, kg, tm=RS, seq_len=RS, feature_major=False)
    ret_s, rfin_s = _retention(qr_s, kr_s, vr_s, gr_s, _pair_states_to_blockdiag(state_ret[0].astype(F32)),
                               _ret_tables(log_g, float(DS)), gng, B=DB, S=DS, T=DS, out_dtype=F32)
    q_s = qpm_s.transpose(1, 0, 2).reshape(DB, DS, D_GRP).astype(F32)
    t = jnp.arange(DS, dtype=jnp.int32)
    kk = jnp.arange(MOBA_BLOCK, dtype=jnp.int32)
    b_last = _rel_bias_of(t[:, None] + (MOBA_BLOCK - kk)[None, :], rel_bias)
    b_last = b_last.transpose(2, 0, 1).reshape(N_HEADS * DS, MOBA_BLOCK)
    b_new = _rel_bias_of(t[:, None] - t[None, :], rel_bias)
    b_new = b_new.transpose(2, 0, 1).reshape(N_HEADS * DS, DS)
    b_new = jnp.pad(b_new, ((0, 0), (0, LANES - DS)), constant_values=NEG)
    att_s = _sample_attention(page_table, _feature_major_pages(cache_k[0]), _feature_major_pages(cache_v[0]),
                              q_s, ka_s.reshape(DB, DS, D_GRP), va_s.reshape(DB, DS, D_GRP),
                              ga_s.reshape(DB, DS, D_GRP), b_last, b_new,
                              NP=math.gcd(n_pages, 16))
    y_s = _out_proj(xs, att_s.reshape(RS, D_GRP), ret_s, w_out_bf, tm=RS).reshape(DB, DS, D_MODEL)

    sdt = state_ret.dtype
    return (y_p, y_s,
            _token_major_5d(kt, B, S), _token_major_5d(vt, B, S),
            _blockdiag_to_head_states(rfin_p).astype(sdt)[None],
            ka_s.reshape(1, DB, DS, N_HEADS, HEAD_DIM), va_s.reshape(1, DB, DS, N_HEADS, HEAD_DIM),
            _blockdiag_to_head_states(rfin_s).astype(sdt)[None])
```

```python
import functools
import math

import jax
import jax.numpy as jnp
import numpy as np
from jax import lax
from jax.experimental import pallas as pl
from jax.experimental.pallas import tpu as pltpu

D_MODEL = 1024
HEAD_DIM = 64
N_HEADS = 8
D_GRP = N_HEADS * HEAD_DIM
LANES = 128
N_PAIRS = D_GRP // LANES
MOBA_BLOCK = 256
MOBA_TOPK = 3
SCORE_LOOKAHEAD = 5
RET_CHUNK = 128
PAGE_SIZE = 128
N_BUCKETS = 32
MAX_DISTANCE = 128
ROPE_BASE = 10000.0
EPS = 1e-6
NEG = -1e30
BIG = 1e30
LOWEST = -3e38
LOG2E = math.log2(math.e)
Q_SCALE = HEAD_DIM ** -0.5 * LOG2E
V_ROWS = HEAD_DIM + 16
SAMPLE_UNROLL = 4
VMEM_LIMIT = 56 * 1024 * 1024

F32 = jnp.float32
BF16 = jnp.bfloat16
NT_DIMS = (((1,), (1,)), ((), ()))


def _silu(g):
    return g * (1.0 / (1.0 + jnp.exp(-g)))


def _head_mean(z):
    lane = lax.broadcasted_iota(jnp.int32, z.shape, 1)
    lo = lane < HEAD_DIM
    s0 = jnp.sum(jnp.where(lo, z, 0.0), axis=-1, keepdims=True)
    s1 = jnp.sum(jnp.where(lo, 0.0, z), axis=-1, keepdims=True)
    return jnp.where(lo, s0, s1) * (1.0 / HEAD_DIM)


def _top_k_blocks(gate, n_valid, blk, axis):
    n_blk = gate.shape[axis]
    g = jnp.where(blk < n_valid, gate, NEG)
    sel = jnp.zeros(gate.shape, F32)
    for _ in range(MOBA_TOPK):
        mx = jnp.max(g, axis=axis, keepdims=True)
        first = jnp.min(jnp.where(g == mx, blk, float(n_blk)), axis=axis, keepdims=True)
        hit = blk == first
        sel = jnp.where(hit, 1.0, sel)
        g = jnp.where(hit, LOWEST, g)
    return jnp.where(blk < n_valid, sel, 0.0)


def _proj_kernel(x_ref, cos_ref, sin_ref, lng_ref, w_ref, qg_ref, kg_ref,
                 qpm_ref, kpm_ref, k_ref, v_ref, vt_ref, km_ref, ga_ref, gr_ref,
                 qr_ref, kr_ref, vr_ref, *, feature_major):
    x = x_ref[...]
    ms = jnp.mean(x * x, axis=-1, keepdims=True)
    h = (x * lax.rsqrt(ms + EPS) * lng_ref[...]).astype(BF16)

    def seg(s):
        return jnp.dot(h, w_ref[:, s * D_GRP:(s + 1) * D_GRP], preferred_element_type=F32)

    def chunks(z):
        return [z[:, c * LANES:(c + 1) * LANES] for c in range(N_PAIRS)]

    qa = seg(0)
    for c, z in enumerate(chunks(qa)):
        g = qg_ref[:, c * LANES:(c + 1) * LANES]
        zn = z * lax.rsqrt(_head_mean(z * z) + EPS) * g
        qpm_ref[c] = (zn * Q_SCALE).astype(BF16)
    ka = seg(1)
    kn = []
    for c, z in enumerate(chunks(ka)):
        g = kg_ref[:, c * LANES:(c + 1) * LANES]
        zn = z * lax.rsqrt(_head_mean(z * z) + EPS) * g
        kpm_ref[c] = zn.astype(BF16)
        kn.append(zn)
    kn = jnp.concatenate(kn, axis=1)
    for b in range(km_ref.shape[1]):
        km_ref[0, b:b + 1, :] = jnp.mean(kn[b * MOBA_BLOCK:(b + 1) * MOBA_BLOCK], axis=0, keepdims=True)
    va = seg(2)
    vat = va.T
    ones = jnp.ones((V_ROWS - HEAD_DIM, MOBA_BLOCK), F32)
    for b in range(vt_ref.shape[0]):
        vb = vat[:, b * MOBA_BLOCK:(b + 1) * MOBA_BLOCK]
        vt_ref[b] = jnp.concatenate(
            [t for h in range(N_HEADS) for t in (vb[h * HEAD_DIM:(h + 1) * HEAD_DIM], ones)],
            axis=0).astype(BF16)
    if feature_major:
        k_ref[0] = kn.T
        v_ref[0] = vat
    else:
        k_ref[...] = kn
        v_ref[...] = va
    ga_ref[...] = seg(3)

    cos = cos_ref[...]
    sin = sin_ref[...]
    lane = lax.broadcasted_iota(jnp.int32, cos.shape, 1)
    first_half = (lane % HEAD_DIM) < (HEAD_DIM // 2)

    def rope(z):
        swapped = jnp.where(first_half,
                            pltpu.roll(z, LANES - HEAD_DIM // 2, 1),
                            pltpu.roll(z, HEAD_DIM // 2, 1))
        return z * cos + swapped * sin

    qr = seg(4)
    for c, z in enumerate(chunks(qr)):
        qr_ref[:, c * LANES:(c + 1) * LANES] = rope(z)
    kr = seg(5)
    for c, z in enumerate(chunks(kr)):
        kr_ref[:, c * LANES:(c + 1) * LANES] = rope(z) * (HEAD_DIM ** -0.5)
    vr_ref[...] = seg(6)
    gr_ref[...] = seg(7)


def _proj(x, cos, sin, ln_g, w_bf, qg, kg, *, tm, seq_len, feature_major):
    R = x.shape[0]
    assert tm % MOBA_BLOCK == 0 and R % tm == 0 and cos.shape[0] % tm == 0 and seq_len % tm == 0
    n_pos = cos.shape[0] // tm
    nb = tm // MOBA_BLOCK
    nt = seq_len // tm
    row = lambda i: (i, 0)
    full = lambda i: (0, 0)
    wide = pl.BlockSpec((tm, D_GRP), row)
    wide_shape = jax.ShapeDtypeStruct((R, D_GRP), F32)
    if feature_major:
        kv_shape = jax.ShapeDtypeStruct((R // seq_len, D_GRP, seq_len), F32)
        kv_spec = pl.BlockSpec((1, D_GRP, tm), lambda i: (i // nt, 0, i % nt))
    else:
        kv_shape, kv_spec = wide_shape, wide
    out_shape = (
        jax.ShapeDtypeStruct((N_PAIRS, R, LANES), BF16),
        jax.ShapeDtypeStruct((N_PAIRS, R, LANES), BF16),
        kv_shape,
        kv_shape,
        jax.ShapeDtypeStruct((R // MOBA_BLOCK, N_HEADS * V_ROWS, MOBA_BLOCK), BF16),
        jax.ShapeDtypeStruct((R // tm, nb, D_GRP), F32),
        wide_shape,
        wide_shape,
        wide_shape,
        wide_shape,
        wide_shape,
    )
    pm = pl.BlockSpec((N_PAIRS, tm, LANES), lambda i: (0, i, 0))
    out_specs = (
        pm, pm, kv_spec, kv_spec,
        pl.BlockSpec((nb, N_HEADS * V_ROWS, MOBA_BLOCK), lambda i: (i, 0, 0)),
        pl.BlockSpec((1, nb, D_GRP), lambda i: (i, 0, 0)),
        wide, wide, wide, wide, wide,
    )
    in_specs = [
        pl.BlockSpec((tm, D_MODEL), row),
        pl.BlockSpec((tm, LANES), lambda i: (i % n_pos, 0)),
        pl.BlockSpec((tm, LANES), lambda i: (i % n_pos, 0)),
        pl.BlockSpec((1, D_MODEL), full),
        pl.BlockSpec((D_MODEL, 8 * D_GRP), full),
        pl.BlockSpec((1, D_GRP), full),
        pl.BlockSpec((1, D_GRP), full),
    ]
    return pl.pallas_call(
        functools.partial(_proj_kernel, feature_major=feature_major),
        out_shape=out_shape,
        grid=(R // tm,),
        in_specs=in_specs,
        out_specs=out_specs,
        compiler_params=pltpu.CompilerParams(
            dimension_semantics=("arbitrary",), vmem_limit_bytes=VMEM_LIMIT),
        name="proj",
    )(x, cos, sin, ln_g, w_bf, qg, kg)


def _ret_kernel(q_ref, k_ref, v_ref, g_ref, r0_ref, d2_ref, qd_ref, kd_ref, gl_ref, gng_ref,
                out_ref, rfin_ref, r_scr, *, T, L):
    t = pl.program_id(1)

    @pl.when(t == 0)
    def _():
        r_scr[...] = r0_ref[0]

    lane = lax.broadcasted_iota(jnp.int32, (L, LANES), 1)
    lo = lane < HEAD_DIM
    ri = lax.broadcasted_iota(jnp.int32, (LANES, LANES), 0)
    ci = lax.broadcasted_iota(jnp.int32, (LANES, LANES), 1)
    same_head = (ri < HEAD_DIM) == (ci < HEAD_DIM)
    rows_c = min(T, L)

    def load(ref, rows, cols):
        z = ref[rows, cols]
        if rows_c < L:
            z = jnp.concatenate([z, jnp.zeros((L - rows_c, LANES), F32)], axis=0)
        return z

    def chunk(ci_, carry):
        if T > L:
            rows = pl.ds(pl.multiple_of(ci_ * L, L), L)
        else:
            rows = pl.ds(0, rows_c)
        staged = []
        for c in range(N_PAIRS):
            cols = slice(c * LANES, (c + 1) * LANES)
            q = load(q_ref, rows, cols)
            k = load(k_ref, rows, cols)
            vc = load(v_ref, rows, cols).astype(BF16)
            R = r_scr[c]
            qs = jnp.concatenate([jnp.where(lo, q, 0.0), jnp.where(lo, 0.0, q)], axis=0).astype(BF16)
            s = lax.dot_general(qs, k.astype(BF16), NT_DIMS, preferred_element_type=F32)
            cross = jnp.dot(q.astype(BF16), R.astype(BF16), preferred_element_type=F32)
            kdk = (k * kd_ref[:, cols]).T.astype(BF16)
            upd = jnp.dot(kdk, vc, preferred_element_type=F32)
            r_scr[c] = gl_ref[:, cols] * R + jnp.where(same_head, upd, 0.0)
            staged.append((cols, s, cross, vc))
        for c, (cols, s, cross, vc) in enumerate(staged):
            inner2 = jnp.dot((s * d2_ref[c]).astype(BF16), vc, preferred_element_type=F32)
            o = jnp.where(lo, inner2[:L], inner2[L:]) + cross * qd_ref[:, cols]
            mu = _head_mean(o)
            d = o - mu
            var = _head_mean(d * d)
            y = d * lax.rsqrt(var + EPS) * gng_ref[:, cols]
            res = (y * _silu(load(g_ref, rows, cols))).astype(out_ref.dtype)
            out_ref[rows, cols] = res[:rows_c]
        return carry

    lax.fori_loop(0, max(T // L, 1), chunk, 0)

    @pl.when(t == pl.num_programs(1) - 1)
    def _():
        rfin_ref[0] = r_scr[...]


def _retention(q, k, v, g, r0, tabs, gn_g, *, B, S, T, out_dtype):
    d2, qd, kd, gl = tabs
    L = RET_CHUNK
    nt = S // T
    tok = pl.BlockSpec((T, D_GRP), lambda b, t: (b * nt + t, 0))
    state = pl.BlockSpec((1, N_PAIRS, LANES, LANES), lambda b, t: (b, 0, 0, 0))
    const2 = lambda b, t: (0, 0)
    return pl.pallas_call(
        functools.partial(_ret_kernel, T=T, L=L),
        out_shape=(jax.ShapeDtypeStruct((B * S, D_GRP), out_dtype),
                   jax.ShapeDtypeStruct((B, N_PAIRS, LANES, LANES), F32)),
        grid=(B, nt),
        in_specs=[tok, tok, tok, tok, state,
                  pl.BlockSpec((N_PAIRS, 2 * L, L), lambda b, t: (0, 0, 0)),
                  pl.BlockSpec((L, D_GRP), const2), pl.BlockSpec((L, D_GRP), const2),
                  pl.BlockSpec((1, D_GRP), const2), pl.BlockSpec((1, D_GRP), const2)],
        out_specs=(tok, state),
        scratch_shapes=[pltpu.VMEM((N_PAIRS, LANES, LANES), F32)],
        compiler_params=pltpu.CompilerParams(
            dimension_semantics=("arbitrary", "arbitrary"), vmem_limit_bytes=VMEM_LIMIT),
        name="retention",
    )(q, k, v, g, r0, d2, qd, kd, gl, gn_g)


def _ret_tables(log_g, l_true):
    L = RET_CHUNK
    n = jnp.arange(L, dtype=F32)
    diff = n[:, None] - n[None, :]
    decay = jnp.where(diff >= 0, jnp.exp(log_g[:, None, None] * jnp.maximum(diff, 0.0)), 0.0)
    d2 = decay.reshape(N_PAIRS, 2 * L, L)
    per_lane = lambda t: jnp.repeat(t, HEAD_DIM, axis=1)
    qd = per_lane(jnp.exp(log_g[None, :] * (n[:, None] + 1.0)))
    kd = per_lane(jnp.exp(log_g[None, :] * (l_true - 1.0 - n)[:, None]))
    gl = per_lane(jnp.exp(log_g * l_true)[None, :])
    return d2, qd, kd, gl


def _t5_bucket_starts(max_dist):
    d = np.arange(max_dist)
    max_exact = N_BUCKETS // 2
    scaled = (np.log(np.maximum(d, 1).astype(np.float32) / np.float32(max_exact))
              / np.float32(math.log(MAX_DISTANCE / max_exact)) * np.float32(N_BUCKETS - max_exact))
    bucket = np.where(d < max_exact, d, np.minimum(max_exact + scaled.astype(np.int32), N_BUCKETS - 1))
    assert np.all(np.diff(bucket) >= 0)
    return [(int(b), int(d[bucket == b][0])) for b in np.unique(bucket)]


def _bias_tables(rb_ref, bo_ref, bp_ref):
    shape = (MOBA_BLOCK, MOBA_BLOCK)
    d_own = lax.broadcasted_iota(jnp.int32, shape, 1) - lax.broadcasted_iota(jnp.int32, shape, 0)
    starts = _t5_bucket_starts(2 * MOBA_BLOCK)
    assert starts[0] == (0, 0) and starts[-1][0] == N_BUCKETS - 1

    def head(h, carry):
        far = rb_ref[N_BUCKETS - 1, h]
        for ref, d in ((bo_ref, d_own), (bp_ref, d_own + MOBA_BLOCK)):
            val = jnp.full(shape, (rb_ref[0, h] - far) * LOG2E, F32)
            for bkt, lo in starts[1:]:
                val = jnp.where(d >= lo, (rb_ref[bkt, h] - far) * LOG2E, val)
            ref[h] = jnp.where(d >= 0, val, NEG)
        return carry

    lax.fori_loop(0, N_HEADS, head, 0)


def _moba_step(rb_ref, q_ref, k_ref, vt_ref, km_ref, ga_ref, out_ref,
               qm_ref, acc_ref, m_ref, sel_ref, ot_ref, bo_ref, bp_ref, s_scr, p_scr, alpha_scr, *, NI):
    i = pl.program_id(1)

    @pl.when((pl.program_id(0) == 0) & (i == 0))
    def _():
        _bias_tables(rb_ref, bo_ref, bp_ref)

    lane = lax.broadcasted_iota(jnp.int32, (MOBA_BLOCK, LANES), 1)
    blk = lax.broadcasted_iota(jnp.int32, (NI, MOBA_BLOCK), 0).astype(F32)
    i_f = i.astype(F32)
    for c in range(N_PAIRS):
        qc = q_ref[c].astype(F32)
        qm_ref[2 * c] = jnp.where(lane < HEAD_DIM, qc, 0.0).T.astype(BF16)
        qm_ref[2 * c + 1] = jnp.where(lane < HEAD_DIM, 0.0, qc).T.astype(BF16)

    def scores(j, h, bias_ref):
        kj = k_ref[h // 2, pl.ds(pl.multiple_of(j * MOBA_BLOCK, MOBA_BLOCK), MOBA_BLOCK), :]
        s = jnp.dot(kj, qm_ref[h], preferred_element_type=F32)
        return s if bias_ref is None else s + bias_ref[h]

    def values(j, h):
        return vt_ref[j, h * V_ROWS:(h + 1) * V_ROWS, :]

    for h in range(N_HEADS):
        gate = jnp.dot(km_ref[0, h // 2], qm_ref[h], preferred_element_type=F32)
        sel_ref[h] = _top_k_blocks(gate, i_f, blk, 0)

    d_far = _t5_bucket_starts(2 * MOBA_BLOCK)[-1][1]
    k0 = (MOBA_BLOCK - d_far + 1) // 8 * 8
    assert d_far - 1 <= LANES
    LA = SCORE_LOOKAHEAD

    LAST = N_HEADS - 1

    def finish_last(j_prev):
        acc_ref[LAST] = alpha_scr[...] * acc_ref[LAST] + jnp.dot(values(j_prev, LAST), p_scr[...],
                                                                preferred_element_type=F32)

    def key_block(j, j_prev, j_next, own):
        local = {h: scores(j, h, bo_ref) for h in range(LA)} if own else {}
        near = None if own else (j == i - 1).astype(F32)
        if not own:
            finish_last(j_prev)
        for h in range(N_HEADS):
            t = h + LA
            if t < N_HEADS:
                local[t] = scores(j, t, bo_ref if own else None)
            else:
                s_scr[t - N_HEADS] = scores(j_next, t - N_HEADS, None)
            s = local.pop(h) if h in local else s_scr[h]
            if own:
                m_new = jnp.max(s, axis=0, keepdims=True)
                alpha = None
                p = jnp.exp2(s - m_new)
            else:
                corner = s[k0:, :LANES] + near * bp_ref[h, k0:, :LANES]
                s = jnp.concatenate([s[:k0], jnp.concatenate([corner, s[k0:, LANES:]], axis=1)], axis=0)
                selj = sel_ref[h, pl.ds(j, 1), :] > 0.0
                m_old = m_ref[h]
                m_new = jnp.where(selj, jnp.maximum(m_old, jnp.max(s, axis=0, keepdims=True)), m_old)
                alpha = jnp.exp2(m_old - m_new)
                p = jnp.exp2(s - jnp.where(selj, m_new, BIG))
            m_ref[h] = m_new
            pb = p.astype(BF16)
            if h == LAST:
                p_scr[...] = pb
                alpha_scr[...] = jnp.zeros_like(m_new) if own else alpha
                if own:
                    acc_ref[h] = jnp.zeros(acc_ref.shape[1:], F32)
            elif own:
                acc_ref[h] = jnp.dot(values(j, h), pb, preferred_element_type=F32)
            else:
                acc_ref[h] = alpha * acc_ref[h] + jnp.dot(values(j, h), pb, preferred_element_type=F32)

    key_block(i, None, 0, True)

    def past_pair(jj, carry):
        j = 2 * jj
        key_block(j, jnp.where(j == 0, i, j - 1), j + 1, False)
        key_block(j + 1, j, j + 2, False)
        return carry

    lax.fori_loop(0, i // 2, past_pair, 0)

    @pl.when(i % 2 == 1)
    def _():
        key_block(i - 1, jnp.where(i == 1, i, i - 2), i, False)
    finish_last(jnp.maximum(i - 1, 0))
    for h in range(N_HEADS):
        o = acc_ref[h]
        ot_ref[h * HEAD_DIM:(h + 1) * HEAD_DIM, :] = o[:HEAD_DIM] / o[HEAD_DIM:HEAD_DIM + 1]
    out_ref[...] = (ot_ref[...].T * _silu(ga_ref[...])).astype(out_ref.dtype)


def _sample_start(g, pt_ref, kc_hbm, vc_hbm, pages_ref, sem, *, HP):
    sb = g // 4
    ph = g % 4
    base = (ph % 2) * HP

    def copies(cache_hbm):
        def body(tt, carry):
            for e in range(2):
                t = 2 * tt + e
                pltpu.make_async_copy(cache_hbm.at[pt_ref[sb, base + t]], pages_ref.at[t],
                                      sem.at[0]).start(priority=e)
            return carry

        lax.fori_loop(0, HP // 2, body, 0)

    @pl.when(ph < 2)
    def _():
        copies(kc_hbm)

    @pl.when(ph >= 2)
    def _():
        copies(vc_hbm)


def _sample_stage(g, kc_hbm, q_ref, kn_ref, vn_ref, ga_ref, blast_ref, bnew_ref, out_ref,
                  s_ref, gate_ref, bmax_ref, meff_ref, m_ref, l_ref, acc_ref, qbd_ref, pages_ref, sem,
                  *, HP, NB, TQ):
    ph = g % 4
    HQ = N_HEADS * TQ
    bpst = HP * PAGE_SIZE // MOBA_BLOCK
    assert bpst % SAMPLE_UNROLL == 0
    lane_q = lax.broadcasted_iota(jnp.int32, (HQ, LANES), 1)

    def wait_one(t, carry):
        pltpu.make_async_copy(kc_hbm.at[0], pages_ref.at[t], sem.at[0]).wait()
        return carry

    lax.fori_loop(0, HP, wait_one, 0)

    def block(u):
        return jnp.concatenate([pages_ref[2 * u], pages_ref[2 * u + 1]], axis=1)

    @pl.when(ph == 0)
    def _():
        qt = jnp.concatenate([q_ref[0]] * N_HEADS, axis=0)
        r = lax.broadcasted_iota(jnp.int32, (HQ, D_GRP), 0)
        l = lax.broadcasted_iota(jnp.int32, (HQ, D_GRP), 1)
        qbd_ref[...] = jnp.where(r // TQ == l // HEAD_DIM, qt, 0.0).astype(BF16)
        gate_ref[...] = jnp.zeros(gate_ref.shape, F32)
        bmax_ref[...] = jnp.zeros(bmax_ref.shape, F32)

    @pl.when(ph < 2)
    def _():
        def body(uu, carry):
            gate = gate_ref[...]
            bmax = bmax_ref[...]
            for r in range(SAMPLE_UNROLL):
                u = SAMPLE_UNROLL * uu + r
                jb = ph * bpst + u
                sc = jnp.dot(qbd_ref[...], block(u).astype(BF16), preferred_element_type=F32)
                gate = jnp.where(lane_q == jb, jnp.sum(sc, axis=1, keepdims=True) * (1.0 / MOBA_BLOCK), gate)
                sc = sc + jnp.where(jb == NB - 1, blast_ref[...], 0.0)
                s_ref[jb] = sc
                bmax = jnp.where(lane_q == jb, jnp.max(sc, axis=1, keepdims=True), bmax)
            gate_ref[...] = gate
            bmax_ref[...] = bmax
            return carry

        lax.fori_loop(0, bpst // SAMPLE_UNROLL, body, 0)

    @pl.when(ph == 1)
    def _():
        sel = _top_k_blocks(gate_ref[...], float(NB), lane_q.astype(F32), 1)
        pad = jnp.zeros((LANES - TQ, D_GRP), F32)
        kn = jnp.concatenate([kn_ref[0], pad], axis=0).astype(BF16)
        vn = jnp.concatenate([vn_ref[0], pad], axis=0).astype(BF16)
        sn = lax.dot_general(qbd_ref[...], kn, NT_DIMS, preferred_element_type=F32) + bnew_ref[...]
        m = jnp.maximum(jnp.max(jnp.where(sel > 0.0, bmax_ref[...], NEG), axis=1, keepdims=True),
                        jnp.max(sn, axis=1, keepdims=True))
        m_ref[...] = m
        meff_ref[...] = jnp.where(sel > 0.0, m, BIG)
        pn = jnp.exp2(sn - m)
        l_ref[...] = jnp.sum(pn, axis=1, keepdims=True)
        acc_ref[...] = jnp.dot(pn.astype(BF16), vn, preferred_element_type=F32)

    @pl.when(ph >= 2)
    def _():
        def body(uu, carry):
            meff_all = meff_ref[...]
            l = l_ref[...]
            acc = acc_ref[...]
            for r in range(SAMPLE_UNROLL):
                u = SAMPLE_UNROLL * uu + r
                jb = (ph - 2) * bpst + u
                meff = jnp.sum(jnp.where(lane_q == jb, meff_all, 0.0), axis=1, keepdims=True)
                p = jnp.exp2(s_ref[jb] - meff)
                l = l + jnp.sum(p, axis=1, keepdims=True)
                acc = acc + lax.dot_general(p.astype(BF16), block(u).astype(BF16), NT_DIMS,
                                            preferred_element_type=F32)
            l_ref[...] = l
            acc_ref[...] = acc
            return carry

        lax.fori_loop(0, bpst // SAMPLE_UNROLL, body, 0)

    @pl.when(ph == 3)
    def _():
        o_bd = acc_ref[...] / l_ref[...]
        lane = lax.broadcasted_iota(jnp.int32, (TQ, D_GRP), 1)
        o = jnp.zeros((TQ, D_GRP), F32)
        for h in range(N_HEADS):
            o = jnp.where(lane // HEAD_DIM == h, o_bd[h * TQ:(h + 1) * TQ], o)
        out_ref[0] = o * _silu(ga_ref[0])


N_MOBA_IN, N_MOBA_SCRATCH = 6, 10


def _attention_kernel(pt_ref, *refs, NI, HP, NB, TQ):
    moba_in = refs[:N_MOBA_IN]
    kc_hbm, vc_hbm, qs_ref, kn_ref, vn_ref, gas_ref, blast_ref, bnew_ref, out_ref, outs_ref = refs[N_MOBA_IN:N_MOBA_IN + 10]
    scratch = refs[N_MOBA_IN + 10:]
    moba_scratch = scratch[:N_MOBA_SCRATCH]
    sample_scratch = scratch[N_MOBA_SCRATCH:]
    pages_ref, sem = sample_scratch[-2:]
    g = pl.program_id(0) * NI + pl.program_id(1)
    _sample_start(g, pt_ref, kc_hbm, vc_hbm, pages_ref, sem, HP=HP)
    _moba_step(*moba_in, out_ref, *moba_scratch, NI=NI)
    _sample_stage(g, kc_hbm, qs_ref, kn_ref, vn_ref, gas_ref, blast_ref, bnew_ref, outs_ref,
                  *sample_scratch, HP=HP, NB=NB, TQ=TQ)


def _attention(page_table, rel_bias, qpm, kpm, vt, km_pm, ga, cache_kt, cache_vt,
               q_s, k_new, v_new, ga_s, bias_last, bias_new, *, B, S):
    NI = S // MOBA_BLOCK
    DB, n_pages = page_table.shape
    TQ = q_s.shape[1]
    HQ = N_HEADS * TQ
    HP = n_pages // 2
    NB = n_pages * PAGE_SIZE // MOBA_BLOCK
    assert B * NI == 4 * DB and n_pages % 4 == 0
    assert NB <= LANES and HQ <= LANES and TQ <= LANES
    once = pl.Buffered(1)
    per_sb = pl.BlockSpec((1, TQ, D_GRP), lambda b, i, pt: ((b * NI + i) // 4, 0, 0))
    rows = pl.BlockSpec((MOBA_BLOCK, D_GRP), lambda b, i, pt: (b * NI + i, 0))
    in_specs = [
        pl.BlockSpec(memory_space=pltpu.SMEM),
        pl.BlockSpec((N_PAIRS, MOBA_BLOCK, LANES), lambda b, i, pt: (0, b * NI + i, 0)),
        pl.BlockSpec((N_PAIRS, S, LANES), lambda b, i, pt: (0, b, 0), pipeline_mode=once),
        pl.BlockSpec((NI, N_HEADS * V_ROWS, MOBA_BLOCK), lambda b, i, pt: (b, 0, 0), pipeline_mode=once),
        pl.BlockSpec((1, N_PAIRS, NI, LANES), lambda b, i, pt: (b, 0, 0, 0)),
        rows,
        pl.BlockSpec(memory_space=pl.ANY), pl.BlockSpec(memory_space=pl.ANY),
        per_sb, per_sb, per_sb, per_sb,
        pl.BlockSpec((HQ, MOBA_BLOCK), lambda b, i, pt: (0, 0)),
        pl.BlockSpec((HQ, LANES), lambda b, i, pt: (0, 0)),
    ]
    scratch_shapes = [
        pltpu.VMEM((N_HEADS, LANES, MOBA_BLOCK), BF16),
        pltpu.VMEM((N_HEADS, V_ROWS, MOBA_BLOCK), F32),
        pltpu.VMEM((N_HEADS, 1, MOBA_BLOCK), F32),
        pltpu.VMEM((N_HEADS, NI, MOBA_BLOCK), F32),
        pltpu.VMEM((D_GRP, MOBA_BLOCK), F32),
        pltpu.VMEM((N_HEADS, MOBA_BLOCK, MOBA_BLOCK), F32),
        pltpu.VMEM((N_HEADS, MOBA_BLOCK, MOBA_BLOCK), F32),
        pltpu.VMEM((SCORE_LOOKAHEAD, MOBA_BLOCK, MOBA_BLOCK), F32),
        pltpu.VMEM((MOBA_BLOCK, MOBA_BLOCK), BF16),
        pltpu.VMEM((1, MOBA_BLOCK), F32),
        pltpu.VMEM((NB, HQ, MOBA_BLOCK), F32),
        pltpu.VMEM((HQ, LANES), F32),
        pltpu.VMEM((HQ, LANES), F32),
        pltpu.VMEM((HQ, LANES), F32),
        pltpu.VMEM((HQ, 1), F32),
        pltpu.VMEM((HQ, 1), F32),
        pltpu.VMEM((HQ, D_GRP), F32),
        pltpu.VMEM((HQ, D_GRP), BF16),
        pltpu.VMEM((HP, D_GRP, PAGE_SIZE), F32),
        pltpu.SemaphoreType.DMA((1,)),
    ]
    assert len(scratch_shapes) == N_MOBA_SCRATCH + 10
    grid_spec = pltpu.PrefetchScalarGridSpec(
        num_scalar_prefetch=1, grid=(B, NI), in_specs=in_specs,
        out_specs=(rows, per_sb), scratch_shapes=scratch_shapes)
    return pl.pallas_call(
        functools.partial(_attention_kernel, NI=NI, HP=HP, NB=NB, TQ=TQ),
        out_shape=(jax.ShapeDtypeStruct((B * S, D_GRP), BF16),
                   jax.ShapeDtypeStruct((DB, TQ, D_GRP), F32)),
        grid_spec=grid_spec,
        compiler_params=pltpu.CompilerParams(
            dimension_semantics=("arbitrary", "arbitrary"), vmem_limit_bytes=VMEM_LIMIT),
        name="attention",
    )(page_table, rel_bias, qpm, kpm, vt, km_pm, ga, cache_kt, cache_vt,
      q_s, k_new, v_new, ga_s, bias_last, bias_new)


def _out_kernel(x_ref, a_ref, r_ref, w_ref, y_ref):
    cat = jnp.concatenate([a_ref[...].astype(BF16), r_ref[...].astype(BF16)], axis=1)
    y_ref[...] = x_ref[...] + jnp.dot(cat, w_ref[...], preferred_element_type=F32)


def _out_proj(x, att, ret, w_bf, *, tm):
    R = x.shape[0]
    row = lambda i: (i, 0)
    return pl.pallas_call(
        _out_kernel,
        out_shape=jax.ShapeDtypeStruct((R, D_MODEL), F32),
        grid=(R // tm,),
        in_specs=[pl.BlockSpec((tm, D_MODEL), row), pl.BlockSpec((tm, D_GRP), row),
                  pl.BlockSpec((tm, D_GRP), row), pl.BlockSpec((2 * D_GRP, D_MODEL), lambda i: (0, 0))],
        out_specs=pl.BlockSpec((tm, D_MODEL), row),
        compiler_params=pltpu.CompilerParams(
            dimension_semantics=("arbitrary",), vmem_limit_bytes=VMEM_LIMIT),
        name="out_proj",
    )(x, att, ret, w_bf)


def _rope_tables(pos):
    half = HEAD_DIM // 2
    inv = ROPE_BASE ** (-jnp.arange(half, dtype=F32) / half)
    ang = pos.astype(F32)[:, None] * inv[None, :]
    cos, sin = jnp.cos(ang), jnp.sin(ang)
    return (jnp.tile(jnp.concatenate([cos, cos], axis=1), (1, 2)),
            jnp.tile(jnp.concatenate([-sin, sin], axis=1), (1, 2)))


def _t5_bucket(dist):
    max_exact = N_BUCKETS // 2
    d = jnp.maximum(dist, 1).astype(F32)
    large = max_exact + (jnp.log(d / max_exact) / math.log(MAX_DISTANCE / max_exact)
                         * (N_BUCKETS - max_exact)).astype(jnp.int32)
    large = jnp.minimum(large, N_BUCKETS - 1)
    return jnp.where(dist < max_exact, dist, large)


def _rel_bias_of(dist, rel_bias):
    rb = rel_bias.astype(F32)
    b = (rb[_t5_bucket(jnp.maximum(dist, 0))] - rb[N_BUCKETS - 1]) * LOG2E
    return jnp.where((dist >= 0)[..., None], b, NEG)


def _pair_states_to_blockdiag(state):
    B = state.shape[0]
    s5 = state.reshape(B, N_PAIRS, 2, HEAD_DIM, HEAD_DIM)
    top = jnp.pad(s5[:, :, 0], ((0, 0), (0, 0), (0, 0), (0, HEAD_DIM)))
    bot = jnp.pad(s5[:, :, 1], ((0, 0), (0, 0), (0, 0), (HEAD_DIM, 0)))
    return jnp.concatenate([top, bot], axis=2)


def _blockdiag_to_head_states(bd):
    B = bd.shape[0]
    d = jnp.stack([bd[:, :, :HEAD_DIM, :HEAD_DIM], bd[:, :, HEAD_DIM:, HEAD_DIM:]], axis=2)
    return d.reshape(B, N_HEADS, HEAD_DIM, HEAD_DIM)


def _feature_major_pages(cache):
    n_phys = cache.shape[0]
    return jnp.transpose(cache, (0, 2, 3, 1)).reshape(n_phys, D_GRP, PAGE_SIZE)


def _token_major_5d(xt, B, S):
    return jnp.transpose(xt.reshape(B, N_HEADS, HEAD_DIM, S), (0, 3, 1, 2))[None]


def kernel(x_prompt, x_sample, cache_k, cache_v, state_ret, page_table, ln_g, w_in,
           q_norm_g, k_norm_g, rel_bias, ret_norm_g, w_out):
    depth = ln_g.shape[0]
    assert depth == 1, "single-layer stack"
    B, S, _ = x_prompt.shape
    DB, DS, _ = x_sample.shape
    n_pages = page_table.shape[1]
    past_len = n_pages * PAGE_SIZE
    assert past_len % MOBA_BLOCK == 0 and DS <= MOBA_BLOCK

    w_in_bf = w_in[0].astype(BF16)
    w_out_bf = w_out[0].astype(BF16)
    lng = ln_g[0][None, :]
    qg = jnp.tile(q_norm_g[0], N_HEADS)[None, :]
    kg = jnp.tile(k_norm_g[0], N_HEADS)[None, :]
    gng = ret_norm_g[0][None, :]
    log_g = jnp.log1p(-jnp.exp2(-5.0 - jnp.arange(N_HEADS, dtype=F32)))

    xp = x_prompt.reshape(B * S, D_MODEL)
    cos_p, sin_p = _rope_tables(jnp.arange(S, dtype=jnp.int32))
    (qpm, kpm, kt, vt, vt_blk, km, ga, gr, qr, kr, vr) = _proj(
        xp, cos_p, sin_p, lng, w_in_bf, qg, kg, tm=MOBA_BLOCK, seq_len=S, feature_major=True)
    ret_p, rfin_p = _retention(qr, kr, vr, gr, jnp.zeros((B, N_PAIRS, LANES, LANES), F32),
                               _ret_tables(log_g, float(RET_CHUNK)), gng, B=B, S=S, T=1024, out_dtype=BF16)
    NI = S // MOBA_BLOCK
    km_pm = km.reshape(B, NI, N_PAIRS, LANES).transpose(0, 2, 1, 3).astype(BF16)

    RS = DB * DS
    RP = -(-RS // MOBA_BLOCK) * MOBA_BLOCK
    xs = x_sample.reshape(RS, D_MODEL)
    pos_s = past_len + jnp.arange(DS, dtype=jnp.int32)
    cos_s, sin_s = _rope_tables(jnp.pad(jnp.tile(pos_s, DB), (0, RP - RS)))
    (qpm_s, _, ka_s, va_s, _, _, ga_s, gr_s, qr_s, kr_s, vr_s) = _proj(
        jnp.pad(xs, ((0, RP - RS), (0, 0))), cos_s, sin_s, lng, w_in_bf, qg, kg,
        tm=RP, seq_len=RP, feature_major=False)
    qpm_s = qpm_s[:, :RS]
    ka_s, va_s, ga_s, gr_s, qr_s, kr_s, vr_s = (a[:RS] for a in (ka_s, va_s, ga_s, gr_s, qr_s, kr_s, vr_s))
    ret_s, rfin_s = _retention(qr_s, kr_s, vr_s, gr_s, _pair_states_to_blockdiag(state_ret[0].astype(F32)),
                               _ret_tables(log_g, float(DS)), gng, B=DB, S=DS, T=DS, out_dtype=F32)
    q_s = qpm_s.transpose(1, 0, 2).reshape(DB, DS, D_GRP).astype(F32)
    t = jnp.arange(DS, dtype=jnp.int32)
    kk = jnp.arange(MOBA_BLOCK, dtype=jnp.int32)
    b_last = _rel_bias_of(t[:, None] + (MOBA_BLOCK - kk)[None, :], rel_bias)
    b_last = b_last.transpose(2, 0, 1).reshape(N_HEADS * DS, MOBA_BLOCK)
    b_new = _rel_bias_of(t[:, None] - t[None, :], rel_bias)
    b_new = b_new.transpose(2, 0, 1).reshape(N_HEADS * DS, DS)
    b_new = jnp.pad(b_new, ((0, 0), (0, LANES - DS)), constant_values=NEG)
    att_p, att_s = _attention(page_table, rel_bias.astype(F32), qpm, kpm, vt_blk, km_pm, ga,
                              _feature_major_pages(cache_k[0]), _feature_major_pages(cache_v[0]),
                              q_s, ka_s.reshape(DB, DS, D_GRP), va_s.reshape(DB, DS, D_GRP),
                              ga_s.reshape(DB, DS, D_GRP), b_last, b_new, B=B, S=S)
    y_p = _out_proj(xp, att_p, ret_p, w_out_bf, tm=512).reshape(B, S, D_MODEL)
    y_s = _out_proj(xs, att_s.reshape(RS, D_GRP), ret_s, w_out_bf, tm=RS).reshape(DB, DS, D_MODEL)

    sdt = state_ret.dtype
    return (y_p, y_s,
            _token_major_5d(kt, B, S), _token_major_5d(vt, B, S),
            _blockdiag_to_head_states(rfin_p).astype(sdt)[None],
            ka_s.reshape(1, DB, DS, N_HEADS, HEAD_DIM), va_s.reshape(1, DB, DS, N_HEADS, HEAD_DIM),
            _blockdiag_to_head_states(rfin_s).astype(sdt)[None])
```

```python
import functools
import math

import jax
import jax.numpy as jnp
import numpy as np
from jax import lax
from jax.experimental import pallas as pl
from jax.experimental.pallas import tpu as pltpu

D_MODEL = 1024
HEAD_DIM = 64
N_HEADS = 8
D_GRP = N_HEADS * HEAD_DIM
LANES = 128
N_PAIRS = D_GRP // LANES
MOBA_BLOCK = 256
MOBA_TOPK = 3
SCORE_LOOKAHEAD = 5
RET_CHUNK = 128
PAGE_SIZE = 128
N_BUCKETS = 32
MAX_DISTANCE = 128
ROPE_BASE = 10000.0
EPS = 1e-6
NEG = -1e30
BIG = 1e30
LOWEST = -3e38
LOG2E = math.log2(math.e)
Q_SCALE = HEAD_DIM ** -0.5 * LOG2E
V_ROWS = HEAD_DIM + 16
SAMPLE_UNROLL = 16
VMEM_LIMIT = 56 * 1024 * 1024

F32 = jnp.float32
BF16 = jnp.bfloat16
NT_DIMS = (((1,), (1,)), ((), ()))


def _silu(g):
    return g * (1.0 / (1.0 + jnp.exp(-g)))


def _head_mean(z):
    lane = lax.broadcasted_iota(jnp.int32, z.shape, 1)
    lo = lane < HEAD_DIM
    s0 = jnp.sum(jnp.where(lo, z, 0.0), axis=-1, keepdims=True)
    s1 = jnp.sum(jnp.where(lo, 0.0, z), axis=-1, keepdims=True)
    return jnp.where(lo, s0, s1) * (1.0 / HEAD_DIM)


def _top_k_blocks(gate, n_valid, blk, axis):
    n_blk = gate.shape[axis]
    g = jnp.where(blk < n_valid, gate, NEG)
    sel = jnp.zeros(gate.shape, F32)
    for _ in range(MOBA_TOPK):
        mx = jnp.max(g, axis=axis, keepdims=True)
        first = jnp.min(jnp.where(g == mx, blk, float(n_blk)), axis=axis, keepdims=True)
        hit = blk == first
        sel = jnp.where(hit, 1.0, sel)
        g = jnp.where(hit, LOWEST, g)
    return jnp.where(blk < n_valid, sel, 0.0)


def _proj_kernel(x_ref, cos_ref, sin_ref, lng_ref, w_ref, qg_ref, kg_ref,
                 qpm_ref, kpm_ref, k_ref, v_ref, vt_ref, km_ref, ga_ref, gr_ref,
                 qr_ref, kr_ref, vr_ref, *, feature_major):
    x = x_ref[...]
    ms = jnp.mean(x * x, axis=-1, keepdims=True)
    h = (x * lax.rsqrt(ms + EPS) * lng_ref[...]).astype(BF16)

    def seg(s):
        return jnp.dot(h, w_ref[:, s * D_GRP:(s + 1) * D_GRP], preferred_element_type=F32)

    def chunks(z):
        return [z[:, c * LANES:(c + 1) * LANES] for c in range(N_PAIRS)]

    qa = seg(0)
    for c, z in enumerate(chunks(qa)):
        g = qg_ref[:, c * LANES:(c + 1) * LANES]
        zn = z * lax.rsqrt(_head_mean(z * z) + EPS) * g
        qpm_ref[c] = (zn * Q_SCALE).astype(BF16)
    ka = seg(1)
    kn = []
    for c, z in enumerate(chunks(ka)):
        g = kg_ref[:, c * LANES:(c + 1) * LANES]
        zn = z * lax.rsqrt(_head_mean(z * z) + EPS) * g
        kpm_ref[c] = zn.astype(BF16)
        kn.append(zn)
    kn = jnp.concatenate(kn, axis=1)
    for b in range(km_ref.shape[1]):
        km_ref[0, b:b + 1, :] = jnp.mean(kn[b * MOBA_BLOCK:(b + 1) * MOBA_BLOCK], axis=0, keepdims=True)
    va = seg(2)
    vat = va.T
    ones = jnp.ones((V_ROWS - HEAD_DIM, MOBA_BLOCK), F32)
    for b in range(vt_ref.shape[0]):
        vb = vat[:, b * MOBA_BLOCK:(b + 1) * MOBA_BLOCK]
        vt_ref[b] = jnp.concatenate(
            [t for h in range(N_HEADS) for t in (vb[h * HEAD_DIM:(h + 1) * HEAD_DIM], ones)],
            axis=0).astype(BF16)
    if feature_major:
        k_ref[0] = kn.T
        v_ref[0] = vat
    else:
        k_ref[...] = kn
        v_ref[...] = va
    ga_ref[...] = seg(3)

    cos = cos_ref[...]
    sin = sin_ref[...]
    lane = lax.broadcasted_iota(jnp.int32, cos.shape, 1)
    first_half = (lane % HEAD_DIM) < (HEAD_DIM // 2)

    def rope(z):
        swapped = jnp.where(first_half,
                            pltpu.roll(z, LANES - HEAD_DIM // 2, 1),
                            pltpu.roll(z, HEAD_DIM // 2, 1))
        return z * cos + swapped * sin

    qr = seg(4)
    for c, z in enumerate(chunks(qr)):
        qr_ref[:, c * LANES:(c + 1) * LANES] = rope(z)
    kr = seg(5)
    for c, z in enumerate(chunks(kr)):
        kr_ref[:, c * LANES:(c + 1) * LANES] = rope(z) * (HEAD_DIM ** -0.5)
    vr_ref[...] = seg(6)
    gr_ref[...] = seg(7)


def _proj(x, cos, sin, ln_g, w_bf, qg, kg, *, tm, seq_len, feature_major):
    R = x.shape[0]
    assert tm % MOBA_BLOCK == 0 and R % tm == 0 and cos.shape[0] % tm == 0 and seq_len % tm == 0
    n_pos = cos.shape[0] // tm
    nb = tm // MOBA_BLOCK
    nt = seq_len // tm
    row = lambda i: (i, 0)
    full = lambda i: (0, 0)
    wide = pl.BlockSpec((tm, D_GRP), row)
    wide_shape = jax.ShapeDtypeStruct((R, D_GRP), F32)
    if feature_major:
        kv_shape = jax.ShapeDtypeStruct((R // seq_len, D_GRP, seq_len), F32)
        kv_spec = pl.BlockSpec((1, D_GRP, tm), lambda i: (i // nt, 0, i % nt))
    else:
        kv_shape, kv_spec = wide_shape, wide
    out_shape = (
        jax.ShapeDtypeStruct((N_PAIRS, R, LANES), BF16),
        jax.ShapeDtypeStruct((N_PAIRS, R, LANES), BF16),
        kv_shape,
        kv_shape,
        jax.ShapeDtypeStruct((R // MOBA_BLOCK, N_HEADS * V_ROWS, MOBA_BLOCK), BF16),
        jax.ShapeDtypeStruct((R // tm, nb, D_GRP), F32),
        wide_shape,
        wide_shape,
        wide_shape,
        wide_shape,
        wide_shape,
    )
    pm = pl.BlockSpec((N_PAIRS, tm, LANES), lambda i: (0, i, 0))
    out_specs = (
        pm, pm, kv_spec, kv_spec,
        pl.BlockSpec((nb, N_HEADS * V_ROWS, MOBA_BLOCK), lambda i: (i, 0, 0)),
        pl.BlockSpec((1, nb, D_GRP), lambda i: (i, 0, 0)),
        wide, wide, wide, wide, wide,
    )
    in_specs = [
        pl.BlockSpec((tm, D_MODEL), row),
        pl.BlockSpec((tm, LANES), lambda i: (i % n_pos, 0)),
        pl.BlockSpec((tm, LANES), lambda i: (i % n_pos, 0)),
        pl.BlockSpec((1, D_MODEL), full),
        pl.BlockSpec((D_MODEL, 8 * D_GRP), full),
        pl.BlockSpec((1, D_GRP), full),
        pl.BlockSpec((1, D_GRP), full),
    ]
    return pl.pallas_call(
        functools.partial(_proj_kernel, feature_major=feature_major),
        out_shape=out_shape,
        grid=(R // tm,),
        in_specs=in_specs,
        out_specs=out_specs,
        compiler_params=pltpu.CompilerParams(
            dimension_semantics=("arbitrary",), vmem_limit_bytes=VMEM_LIMIT),
        name="proj",
    )(x, cos, sin, ln_g, w_bf, qg, kg)


def _ret_kernel(q_ref, k_ref, v_ref, g_ref, r0_ref, d2_ref, qd_ref, kd_ref, gl_ref, gng_ref,
                out_ref, rfin_ref, r_scr, *, T, L):
    t = pl.program_id(1)

    @pl.when(t == 0)
    def _():
        r_scr[...] = r0_ref[0]

    lane = lax.broadcasted_iota(jnp.int32, (L, LANES), 1)
    lo = lane < HEAD_DIM
    ri = lax.broadcasted_iota(jnp.int32, (LANES, LANES), 0)
    ci = lax.broadcasted_iota(jnp.int32, (LANES, LANES), 1)
    same_head = (ri < HEAD_DIM) == (ci < HEAD_DIM)
    rows_c = min(T, L)

    def load(ref, rows, cols):
        z = ref[rows, cols]
        if rows_c < L:
            z = jnp.concatenate([z, jnp.zeros((L - rows_c, LANES), F32)], axis=0)
        return z

    def chunk(ci_, carry):
        if T > L:
            rows = pl.ds(pl.multiple_of(ci_ * L, L), L)
        else:
            rows = pl.ds(0, rows_c)
        staged = []
        for c in range(N_PAIRS):
            cols = slice(c * LANES, (c + 1) * LANES)
            q = load(q_ref, rows, cols)
            k = load(k_ref, rows, cols)
            vc = load(v_ref, rows, cols).astype(BF16)
            R = r_scr[c]
            qs = jnp.concatenate([jnp.where(lo, q, 0.0), jnp.where(lo, 0.0, q)], axis=0).astype(BF16)
            s = lax.dot_general(qs, k.astype(BF16), NT_DIMS, preferred_element_type=F32)
            cross = jnp.dot(q.astype(BF16), R.astype(BF16), preferred_element_type=F32)
            kdk = (k * kd_ref[:, cols]).T.astype(BF16)
            upd = jnp.dot(kdk, vc, preferred_element_type=F32)
            r_scr[c] = gl_ref[:, cols] * R + jnp.where(same_head, upd, 0.0)
            staged.append((cols, s, cross, vc))
        for c, (cols, s, cross, vc) in enumerate(staged):
            inner2 = jnp.dot((s * d2_ref[c]).astype(BF16), vc, preferred_element_type=F32)
            o = jnp.where(lo, inner2[:L], inner2[L:]) + cross * qd_ref[:, cols]
            mu = _head_mean(o)
            d = o - mu
            var = _head_mean(d * d)
            y = d * lax.rsqrt(var + EPS) * gng_ref[:, cols]
            res = (y * _silu(load(g_ref, rows, cols))).astype(out_ref.dtype)
            out_ref[rows, cols] = res[:rows_c]
        return carry

    lax.fori_loop(0, max(T // L, 1), chunk, 0)

    @pl.when(t == pl.num_programs(1) - 1)
    def _():
        rfin_ref[0] = r_scr[...]


def _retention(q, k, v, g, r0, tabs, gn_g, *, B, S, T, out_dtype):
    d2, qd, kd, gl = tabs
    L = RET_CHUNK
    nt = S // T
    tok = pl.BlockSpec((T, D_GRP), lambda b, t: (b * nt + t, 0))
    state = pl.BlockSpec((1, N_PAIRS, LANES, LANES), lambda b, t: (b, 0, 0, 0))
    const2 = lambda b, t: (0, 0)
    return pl.pallas_call(
        functools.partial(_ret_kernel, T=T, L=L),
        out_shape=(jax.ShapeDtypeStruct((B * S, D_GRP), out_dtype),
                   jax.ShapeDtypeStruct((B, N_PAIRS, LANES, LANES), F32)),
        grid=(B, nt),
        in_specs=[tok, tok, tok, tok, state,
                  pl.BlockSpec((N_PAIRS, 2 * L, L), lambda b, t: (0, 0, 0)),
                  pl.BlockSpec((L, D_GRP), const2), pl.BlockSpec((L, D_GRP), const2),
                  pl.BlockSpec((1, D_GRP), const2), pl.BlockSpec((1, D_GRP), const2)],
        out_specs=(tok, state),
        scratch_shapes=[pltpu.VMEM((N_PAIRS, LANES, LANES), F32)],
        compiler_params=pltpu.CompilerParams(
            dimension_semantics=("arbitrary", "arbitrary"), vmem_limit_bytes=VMEM_LIMIT),
        name="retention",
    )(q, k, v, g, r0, d2, qd, kd, gl, gn_g)


def _ret_tables(log_g, l_true):
    L = RET_CHUNK
    n = jnp.arange(L, dtype=F32)
    diff = n[:, None] - n[None, :]
    decay = jnp.where(diff >= 0, jnp.exp(log_g[:, None, None] * jnp.maximum(diff, 0.0)), 0.0)
    d2 = decay.reshape(N_PAIRS, 2 * L, L)
    per_lane = lambda t: jnp.repeat(t, HEAD_DIM, axis=1)
    qd = per_lane(jnp.exp(log_g[None, :] * (n[:, None] + 1.0)))
    kd = per_lane(jnp.exp(log_g[None, :] * (l_true - 1.0 - n)[:, None]))
    gl = per_lane(jnp.exp(log_g * l_true)[None, :])
    return d2, qd, kd, gl


def _t5_bucket_starts(max_dist):
    d = np.arange(max_dist)
    max_exact = N_BUCKETS // 2
    scaled = (np.log(np.maximum(d, 1).astype(np.float32) / np.float32(max_exact))
              / np.float32(math.log(MAX_DISTANCE / max_exact)) * np.float32(N_BUCKETS - max_exact))
    bucket = np.where(d < max_exact, d, np.minimum(max_exact + scaled.astype(np.int32), N_BUCKETS - 1))
    assert np.all(np.diff(bucket) >= 0)
    return [(int(b), int(d[bucket == b][0])) for b in np.unique(bucket)]


def _bias_tables(rb_ref, bo_ref, bp_ref):
    shape = (MOBA_BLOCK, MOBA_BLOCK)
    d_own = lax.broadcasted_iota(jnp.int32, shape, 1) - lax.broadcasted_iota(jnp.int32, shape, 0)
    starts = _t5_bucket_starts(2 * MOBA_BLOCK)
    assert starts[0] == (0, 0) and starts[-1][0] == N_BUCKETS - 1

    def head(h, carry):
        far = rb_ref[N_BUCKETS - 1, h]
        for ref, d in ((bo_ref, d_own), (bp_ref, d_own + MOBA_BLOCK)):
            val = jnp.full(shape, (rb_ref[0, h] - far) * LOG2E, F32)
            for bkt, lo in starts[1:]:
                val = jnp.where(d >= lo, (rb_ref[bkt, h] - far) * LOG2E, val)
            ref[h] = jnp.where(d >= 0, val, NEG)
        return carry

    lax.fori_loop(0, N_HEADS, head, 0)


def _moba_step(rb_ref, q_ref, k_ref, vt_ref, km_ref, ga_ref, out_ref,
               qm_ref, acc_ref, m_ref, sel_ref, ot_ref, bo_ref, bp_ref, s_scr, p_scr, alpha_scr, *, NI):
    i = pl.program_id(1)

    @pl.when((pl.program_id(0) == 0) & (i == 0))
    def _():
        _bias_tables(rb_ref, bo_ref, bp_ref)

    lane = lax.broadcasted_iota(jnp.int32, (MOBA_BLOCK, LANES), 1)
    blk = lax.broadcasted_iota(jnp.int32, (NI, MOBA_BLOCK), 0).astype(F32)
    i_f = i.astype(F32)
    for c in range(N_PAIRS):
        qc = q_ref[c].astype(F32)
        qm_ref[2 * c] = jnp.where(lane < HEAD_DIM, qc, 0.0).T.astype(BF16)
        qm_ref[2 * c + 1] = jnp.where(lane < HEAD_DIM, 0.0, qc).T.astype(BF16)

    def scores(j, h, bias_ref):
        kj = k_ref[h // 2, pl.ds(pl.multiple_of(j * MOBA_BLOCK, MOBA_BLOCK), MOBA_BLOCK), :]
        s = jnp.dot(kj, qm_ref[h], preferred_element_type=F32)
        return s if bias_ref is None else s + bias_ref[h]

    def values(j, h):
        return vt_ref[j, h * V_ROWS:(h + 1) * V_ROWS, :]

    for h in range(N_HEADS):
        gate = jnp.dot(km_ref[0, h // 2], qm_ref[h], preferred_element_type=F32)
        sel_ref[h] = _top_k_blocks(gate, i_f, blk, 0)

    d_far = _t5_bucket_starts(2 * MOBA_BLOCK)[-1][1]
    k0 = (MOBA_BLOCK - d_far + 1) // 8 * 8
    assert d_far - 1 <= LANES
    LA = SCORE_LOOKAHEAD

    LAST = N_HEADS - 1

    def finish_last(j_prev):
        acc_ref[LAST] = alpha_scr[...] * acc_ref[LAST] + jnp.dot(values(j_prev, LAST), p_scr[...],
                                                                preferred_element_type=F32)

    def key_block(j, j_prev, j_next, own):
        local = {h: scores(j, h, bo_ref) for h in range(LA)} if own else {}
        near = None if own else (j == i - 1).astype(F32)
        if not own:
            finish_last(j_prev)
        for h in range(N_HEADS):
            t = h + LA
            if t < N_HEADS:
                local[t] = scores(j, t, bo_ref if own else None)
            else:
                s_scr[t - N_HEADS] = scores(j_next, t - N_HEADS, None)
            s = local.pop(h) if h in local else s_scr[h]
            if own:
                m_new = jnp.max(s, axis=0, keepdims=True)
                alpha = None
                p = jnp.exp2(s - m_new)
            else:
                corner = s[k0:, :LANES] + near * bp_ref[h, k0:, :LANES]
                s = jnp.concatenate([s[:k0], jnp.concatenate([corner, s[k0:, LANES:]], axis=1)], axis=0)
                selj = sel_ref[h, pl.ds(j, 1), :] > 0.0
                m_old = m_ref[h]
                m_new = jnp.where(selj, jnp.maximum(m_old, jnp.max(s, axis=0, keepdims=True)), m_old)
                alpha = jnp.exp2(m_old - m_new)
                p = jnp.exp2(s - jnp.where(selj, m_new, BIG))
            m_ref[h] = m_new
            pb = p.astype(BF16)
            if h == LAST:
                p_scr[...] = pb
                alpha_scr[...] = jnp.zeros_like(m_new) if own else alpha
                if own:
                    acc_ref[h] = jnp.zeros(acc_ref.shape[1:], F32)
            elif own:
                acc_ref[h] = jnp.dot(values(j, h), pb, preferred_element_type=F32)
            else:
                acc_ref[h] = alpha * acc_ref[h] + jnp.dot(values(j, h), pb, preferred_element_type=F32)

    key_block(i, None, 0, True)

    def past_pair(jj, carry):
        j = 2 * jj
        key_block(j, jnp.where(j == 0, i, j - 1), j + 1, False)
        key_block(j + 1, j, j + 2, False)
        return carry

    lax.fori_loop(0, i // 2, past_pair, 0)

    @pl.when(i % 2 == 1)
    def _():
        key_block(i - 1, jnp.where(i == 1, i, i - 2), i, False)
    finish_last(jnp.maximum(i - 1, 0))
    for h in range(N_HEADS):
        o = acc_ref[h]
        ot_ref[h * HEAD_DIM:(h + 1) * HEAD_DIM, :] = o[:HEAD_DIM] / o[HEAD_DIM:HEAD_DIM + 1]
    out_ref[...] = (ot_ref[...].T * _silu(ga_ref[...])).astype(out_ref.dtype)


def _sample_start(g, pt_ref, kc_hbm, vc_hbm, pages_ref, sem, *, HP):
    sb = g // 4
    ph = g % 4
    base = (ph % 2) * HP

    def copies(cache_hbm):
        def body(tt, carry):
            for e in range(2):
                t = 2 * tt + e
                pltpu.make_async_copy(cache_hbm.at[pt_ref[sb, base + t]], pages_ref.at[t],
                                      sem.at[0]).start(priority=e)
            return carry

        lax.fori_loop(0, HP // 2, body, 0)

    @pl.when(ph < 2)
    def _():
        copies(kc_hbm)

    @pl.when(ph >= 2)
    def _():
        copies(vc_hbm)


def _sample_stage(g, kc_hbm, q_ref, kn_ref, vn_ref, ga_ref, blast_ref, bnew_ref, out_ref,
                  s_ref, gate_ref, bmax_ref, meff_ref, m_ref, l_ref, acc_ref, qbd_ref, pages_ref, sem,
                  *, HP, NB, TQ):
    ph = g % 4
    HQ = N_HEADS * TQ
    bpst = HP * PAGE_SIZE // MOBA_BLOCK
    ub = math.gcd(bpst, SAMPLE_UNROLL)
    assert ub % 2 == 0
    lane_q = lax.broadcasted_iota(jnp.int32, (HQ, LANES), 1)

    def wait_one(t, carry):
        pltpu.make_async_copy(kc_hbm.at[0], pages_ref.at[t], sem.at[0]).wait()
        return carry

    lax.fori_loop(0, HP, wait_one, 0)

    def block(u):
        return jnp.concatenate([pages_ref[2 * u], pages_ref[2 * u + 1]], axis=1)

    @pl.when(ph == 0)
    def _():
        qt = jnp.concatenate([q_ref[0]] * N_HEADS, axis=0)
        r = lax.broadcasted_iota(jnp.int32, (HQ, D_GRP), 0)
        l = lax.broadcasted_iota(jnp.int32, (HQ, D_GRP), 1)
        qbd_ref[...] = jnp.where(r // TQ == l // HEAD_DIM, qt, 0.0).astype(BF16)
        gate_ref[...] = jnp.zeros(gate_ref.shape, F32)
        bmax_ref[...] = jnp.zeros(bmax_ref.shape, F32)

    @pl.when(ph < 2)
    def _():
        def body(uu, carry):
            gate = gate_ref[...]
            bmax = bmax_ref[...]
            for r in range(0, ub, 2):
                u = ub * uu + r
                kt2 = jnp.concatenate([block(u), block(u + 1)], axis=1).astype(BF16)
                sc2 = jnp.dot(qbd_ref[...], kt2, preferred_element_type=F32)
                for e in range(2):
                    jb = ph * bpst + u + e
                    sc = sc2[:, e * MOBA_BLOCK:(e + 1) * MOBA_BLOCK]
                    gate = jnp.where(lane_q == jb, jnp.sum(sc, axis=1, keepdims=True) * (1.0 / MOBA_BLOCK), gate)
                    sc = sc + jnp.where(jb == NB - 1, blast_ref[...], 0.0)
                    s_ref[jb] = sc
                    bmax = jnp.where(lane_q == jb, jnp.max(sc, axis=1, keepdims=True), bmax)
            gate_ref[...] = gate
            bmax_ref[...] = bmax
            return carry

        lax.fori_loop(0, bpst // ub, body, 0)

    @pl.when(ph == 1)
    def _():
        sel = _top_k_blocks(gate_ref[...], float(NB), lane_q.astype(F32), 1)
        pad = jnp.zeros((LANES - TQ, D_GRP), F32)
        kn = jnp.concatenate([kn_ref[0], pad], axis=0).astype(BF16)
        vn = jnp.concatenate([vn_ref[0], pad], axis=0).astype(BF16)
        sn = lax.dot_general(qbd_ref[...], kn, NT_DIMS, preferred_element_type=F32) + bnew_ref[...]
        m = jnp.maximum(jnp.max(jnp.where(sel > 0.0, bmax_ref[...], NEG), axis=1, keepdims=True),
                        jnp.max(sn, axis=1, keepdims=True))
        m_ref[...] = m
        meff_ref[...] = jnp.where(sel > 0.0, m, BIG)
        pn = jnp.exp2(sn - m)
        l_ref[...] = jnp.sum(pn, axis=1, keepdims=True)
        acc_ref[...] = jnp.dot(pn.astype(BF16), vn, preferred_element_type=F32)

    @pl.when(ph >= 2)
    def _():
        def body(uu, carry):
            meff_all = meff_ref[...]
            l = l_ref[...]
            acc = acc_ref[...]
            for r in range(ub):
                u = ub * uu + r
                jb = (ph - 2) * bpst + u
                meff = jnp.sum(jnp.where(lane_q == jb, meff_all, 0.0), axis=1, keepdims=True)
                p = jnp.exp2(s_ref[jb] - meff)
                l = l + jnp.sum(p, axis=1, keepdims=True)
                acc = acc + lax.dot_general(p.astype(BF16), block(u).astype(BF16), NT_DIMS,
                                            preferred_element_type=F32)
            l_ref[...] = l
            acc_ref[...] = acc
            return carry

        lax.fori_loop(0, bpst // ub, body, 0)

    @pl.when(ph == 3)
    def _():
        o_bd = acc_ref[...] / l_ref[...]
        lane = lax.broadcasted_iota(jnp.int32, (TQ, D_GRP), 1)
        o = jnp.zeros((TQ, D_GRP), F32)
        for h in range(N_HEADS):
            o = jnp.where(lane // HEAD_DIM == h, o_bd[h * TQ:(h + 1) * TQ], o)
        out_ref[0] = o * _silu(ga_ref[0])


N_MOBA_IN, N_MOBA_SCRATCH = 6, 10


def _attention_kernel(pt_ref, *refs, NI, HP, NB, TQ):
    moba_in = refs[:N_MOBA_IN]
    kc_hbm, vc_hbm, qs_ref, kn_ref, vn_ref, gas_ref, blast_ref, bnew_ref, out_ref, outs_ref = refs[N_MOBA_IN:N_MOBA_IN + 10]
    scratch = refs[N_MOBA_IN + 10:]
    moba_scratch = scratch[:N_MOBA_SCRATCH]
    sample_scratch = scratch[N_MOBA_SCRATCH:]
    pages_ref, sem = sample_scratch[-2:]
    g = pl.program_id(0) * NI + pl.program_id(1)
    _sample_start(g, pt_ref, kc_hbm, vc_hbm, pages_ref, sem, HP=HP)
    _moba_step(*moba_in, out_ref, *moba_scratch, NI=NI)
    _sample_stage(g, kc_hbm, qs_ref, kn_ref, vn_ref, gas_ref, blast_ref, bnew_ref, outs_ref,
                  *sample_scratch, HP=HP, NB=NB, TQ=TQ)


def _attention(page_table, rel_bias, qpm, kpm, vt, km_pm, ga, cache_kt, cache_vt,
               q_s, k_new, v_new, ga_s, bias_last, bias_new, *, B, S):
    NI = S // MOBA_BLOCK
    DB, n_pages = page_table.shape
    TQ = q_s.shape[1]
    HQ = N_HEADS * TQ
    HP = n_pages // 2
    NB = n_pages * PAGE_SIZE // MOBA_BLOCK
    assert B * NI == 4 * DB and n_pages % 4 == 0
    assert NB <= LANES and HQ <= LANES and TQ <= LANES
    once = pl.Buffered(1)
    per_sb = pl.BlockSpec((1, TQ, D_GRP), lambda b, i, pt: ((b * NI + i) // 4, 0, 0))
    rows = pl.BlockSpec((MOBA_BLOCK, D_GRP), lambda b, i, pt: (b * NI + i, 0))
    in_specs = [
        pl.BlockSpec(memory_space=pltpu.SMEM),
        pl.BlockSpec((N_PAIRS, MOBA_BLOCK, LANES), lambda b, i, pt: (0, b * NI + i, 0)),
        pl.BlockSpec((N_PAIRS, S, LANES), lambda b, i, pt: (0, b, 0), pipeline_mode=once),
        pl.BlockSpec((NI, N_HEADS * V_ROWS, MOBA_BLOCK), lambda b, i, pt: (b, 0, 0), pipeline_mode=once),
        pl.BlockSpec((1, N_PAIRS, NI, LANES), lambda b, i, pt: (b, 0, 0, 0)),
        rows,
        pl.BlockSpec(memory_space=pl.ANY), pl.BlockSpec(memory_space=pl.ANY),
        per_sb, per_sb, per_sb, per_sb,
        pl.BlockSpec((HQ, MOBA_BLOCK), lambda b, i, pt: (0, 0)),
        pl.BlockSpec((HQ, LANES), lambda b, i, pt: (0, 0)),
    ]
    scratch_shapes = [
        pltpu.VMEM((N_HEADS, LANES, MOBA_BLOCK), BF16),
        pltpu.VMEM((N_HEADS, V_ROWS, MOBA_BLOCK), F32),
        pltpu.VMEM((N_HEADS, 1, MOBA_BLOCK), F32),
        pltpu.VMEM((N_HEADS, NI, MOBA_BLOCK), F32),
        pltpu.VMEM((D_GRP, MOBA_BLOCK), F32),
        pltpu.VMEM((N_HEADS, MOBA_BLOCK, MOBA_BLOCK), F32),
        pltpu.VMEM((N_HEADS, MOBA_BLOCK, MOBA_BLOCK), F32),
        pltpu.VMEM((SCORE_LOOKAHEAD, MOBA_BLOCK, MOBA_BLOCK), F32),
        pltpu.VMEM((MOBA_BLOCK, MOBA_BLOCK), BF16),
        pltpu.VMEM((1, MOBA_BLOCK), F32),
        pltpu.VMEM((NB, HQ, MOBA_BLOCK), F32),
        pltpu.VMEM((HQ, LANES), F32),
        pltpu.VMEM((HQ, LANES), F32),
        pltpu.VMEM((HQ, LANES), F32),
        pltpu.VMEM((HQ, 1), F32),
        pltpu.VMEM((HQ, 1), F32),
        pltpu.VMEM((HQ, D_GRP), F32),
        pltpu.VMEM((HQ, D_GRP), BF16),
        pltpu.VMEM((HP, D_GRP, PAGE_SIZE), F32),
        pltpu.SemaphoreType.DMA((1,)),
    ]
    assert len(scratch_shapes) == N_MOBA_SCRATCH + 10
    grid_spec = pltpu.PrefetchScalarGridSpec(
        num_scalar_prefetch=1, grid=(B, NI), in_specs=in_specs,
        out_specs=(rows, per_sb), scratch_shapes=scratch_shapes)
    return pl.pallas_call(
        functools.partial(_attention_kernel, NI=NI, HP=HP, NB=NB, TQ=TQ),
        out_shape=(jax.ShapeDtypeStruct((B * S, D_GRP), BF16),
                   jax.ShapeDtypeStruct((DB, TQ, D_GRP), F32)),
        grid_spec=grid_spec,
        compiler_params=pltpu.CompilerParams(
            dimension_semantics=("arbitrary", "arbitrary"), vmem_limit_bytes=VMEM_LIMIT),
        name="attention",
    )(page_table, rel_bias, qpm, kpm, vt, km_pm, ga, cache_kt, cache_vt,
      q_s, k_new, v_new, ga_s, bias_last, bias_new)


def _out_kernel(x_ref, a_ref, r_ref, w_ref, y_ref):
    cat = jnp.concatenate([a_ref[...].astype(BF16), r_ref[...].astype(BF16)], axis=1)
    y_ref[...] = x_ref[...] + jnp.dot(cat, w_ref[...], preferred_element_type=F32)


def _out_proj(x, att, ret, w_bf, *, tm):
    R = x.shape[0]
    row = lambda i: (i, 0)
    return pl.pallas_call(
        _out_kernel,
        out_shape=jax.ShapeDtypeStruct((R, D_MODEL), F32),
        grid=(R // tm,),
        in_specs=[pl.BlockSpec((tm, D_MODEL), row), pl.BlockSpec((tm, D_GRP), row),
                  pl.BlockSpec((tm, D_GRP), row), pl.BlockSpec((2 * D_GRP, D_MODEL), lambda i: (0, 0))],
        out_specs=pl.BlockSpec((tm, D_MODEL), row),
        compiler_params=pltpu.CompilerParams(
            dimension_semantics=("arbitrary",), vmem_limit_bytes=VMEM_LIMIT),
        name="out_proj",
    )(x, att, ret, w_bf)


def _rope_tables(pos):
    half = HEAD_DIM // 2
    inv = ROPE_BASE ** (-jnp.arange(half, dtype=F32) / half)
    ang = pos.astype(F32)[:, None] * inv[None, :]
    cos, sin = jnp.cos(ang), jnp.sin(ang)
    return (jnp.tile(jnp.concatenate([cos, cos], axis=1), (1, 2)),
            jnp.tile(jnp.concatenate([-sin, sin], axis=1), (1, 2)))


def _t5_bucket(dist):
    max_exact = N_BUCKETS // 2
    d = jnp.maximum(dist, 1).astype(F32)
    large = max_exact + (jnp.log(d / max_exact) / math.log(MAX_DISTANCE / max_exact)
                         * (N_BUCKETS - max_exact)).astype(jnp.int32)
    large = jnp.minimum(large, N_BUCKETS - 1)
    return jnp.where(dist < max_exact, dist, large)


def _rel_bias_of(dist, rel_bias):
    rb = rel_bias.astype(F32)
    b = (rb[_t5_bucket(jnp.maximum(dist, 0))] - rb[N_BUCKETS - 1]) * LOG2E
    return jnp.where((dist >= 0)[..., None], b, NEG)


def _pair_states_to_blockdiag(state):
    B = state.shape[0]
    s5 = state.reshape(B, N_PAIRS, 2, HEAD_DIM, HEAD_DIM)
    top = jnp.pad(s5[:, :, 0], ((0, 0), (0, 0), (0, 0), (0, HEAD_DIM)))
    bot = jnp.pad(s5[:, :, 1], ((0, 0), (0, 0), (0, 0), (HEAD_DIM, 0)))
    return jnp.concatenate([top, bot], axis=2)


def _blockdiag_to_head_states(bd):
    B = bd.shape[0]
    d = jnp.stack([bd[:, :, :HEAD_DIM, :HEAD_DIM], bd[:, :, HEAD_DIM:, HEAD_DIM:]], axis=2)
    return d.reshape(B, N_HEADS, HEAD_DIM, HEAD_DIM)


def _feature_major_pages(cache):
    n_phys = cache.shape[0]
    return jnp.transpose(cache, (0, 2, 3, 1)).reshape(n_phys, D_GRP, PAGE_SIZE)


def _token_major_5d(xt, B, S):
    return jnp.transpose(xt.reshape(B, N_HEADS, HEAD_DIM, S), (0, 3, 1, 2))[None]


def kernel(x_prompt, x_sample, cache_k, cache_v, state_ret, page_table, ln_g, w_in,
           q_norm_g, k_norm_g, rel_bias, ret_norm_g, w_out):
    depth = ln_g.shape[0]
    assert depth == 1, "single-layer stack"
    B, S, _ = x_prompt.shape
    DB, DS, _ = x_sample.shape
    n_pages = page_table.shape[1]
    past_len = n_pages * PAGE_SIZE
    assert past_len % MOBA_BLOCK == 0 and DS <= MOBA_BLOCK

    w_in_bf = w_in[0].astype(BF16)
    w_out_bf = w_out[0].astype(BF16)
    lng = ln_g[0][None, :]
    qg = jnp.tile(q_norm_g[0], N_HEADS)[None, :]
    kg = jnp.tile(k_norm_g[0], N_HEADS)[None, :]
    gng = ret_norm_g[0][None, :]
    log_g = jnp.log1p(-jnp.exp2(-5.0 - jnp.arange(N_HEADS, dtype=F32)))

    xp = x_prompt.reshape(B * S, D_MODEL)
    cos_p, sin_p = _rope_tables(jnp.arange(S, dtype=jnp.int32))
    (qpm, kpm, kt, vt, vt_blk, km, ga, gr, qr, kr, vr) = _proj(
        xp, cos_p, sin_p, lng, w_in_bf, qg, kg, tm=MOBA_BLOCK, seq_len=S, feature_major=True)
    ret_p, rfin_p = _retention(qr, kr, vr, gr, jnp.zeros((B, N_PAIRS, LANES, LANES), F32),
                               _ret_tables(log_g, float(RET_CHUNK)), gng, B=B, S=S, T=1024, out_dtype=BF16)
    NI = S // MOBA_BLOCK
    km_pm = km.reshape(B, NI, N_PAIRS, LANES).transpose(0, 2, 1, 3).astype(BF16)

    RS = DB * DS
    RP = -(-RS // MOBA_BLOCK) * MOBA_BLOCK
    xs = x_sample.reshape(RS, D_MODEL)
    pos_s = past_len + jnp.arange(DS, dtype=jnp.int32)
    cos_s, sin_s = _rope_tables(jnp.pad(jnp.tile(pos_s, DB), (0, RP - RS)))
    (qpm_s, _, ka_s, va_s, _, _, ga_s, gr_s, qr_s, kr_s, vr_s) = _proj(
        jnp.pad(xs, ((0, RP - RS), (0, 0))), cos_s, sin_s, lng, w_in_bf, qg, kg,
        tm=RP, seq_len=RP, feature_major=False)
    qpm_s = qpm_s[:, :RS]
    ka_s, va_s, ga_s, gr_s, qr_s, kr_s, vr_s = (a[:RS] for a in (ka_s, va_s, ga_s, gr_s, qr_s, kr_s, vr_s))
    ret_s, rfin_s = _retention(qr_s, kr_s, vr_s, gr_s, _pair_states_to_blockdiag(state_ret[0].astype(F32)),
                               _ret_tables(log_g, float(DS)), gng, B=DB, S=DS, T=DS, out_dtype=F32)
    q_s = qpm_s.transpose(1, 0, 2).reshape(DB, DS, D_GRP).astype(F32)
    t = jnp.arange(DS, dtype=jnp.int32)
    kk = jnp.arange(MOBA_BLOCK, dtype=jnp.int32)
    b_last = _rel_bias_of(t[:, None] + (MOBA_BLOCK - kk)[None, :], rel_bias)
    b_last = b_last.transpose(2, 0, 1).reshape(N_HEADS * DS, MOBA_BLOCK)
    b_new = _rel_bias_of(t[:, None] - t[None, :], rel_bias)
    b_new = b_new.transpose(2, 0, 1).reshape(N_HEADS * DS, DS)
    b_new = jnp.pad(b_new, ((0, 0), (0, LANES - DS)), constant_values=NEG)
    att_p, att_s = _attention(page_table, rel_bias.astype(F32), qpm, kpm, vt_blk, km_pm, ga,
                              _feature_major_pages(cache_k[0]), _feature_major_pages(cache_v[0]),
                              q_s, ka_s.reshape(DB, DS, D_GRP), va_s.reshape(DB, DS, D_GRP),
                              ga_s.reshape(DB, DS, D_GRP), b_last, b_new, B=B, S=S)
    y_p = _out_proj(xp, att_p, ret_p, w_out_bf, tm=512).reshape(B, S, D_MODEL)
    y_s = _out_proj(xs, att_s.reshape(RS, D_GRP), ret_s, w_out_bf, tm=RS).reshape(DB, DS, D_MODEL)

    sdt = state_ret.dtype
    return (y_p, y_s,
            _token_major_5d(kt, B, S), _token_major_5d(vt, B, S),
            _blockdiag_to_head_states(rfin_p).astype(sdt)[None],
            ka_s.reshape(1, DB, DS, N_HEADS, HEAD_DIM), va_s.reshape(1, DB, DS, N_HEADS, HEAD_DIM),
            _blockdiag_to_head_states(rfin_s).astype(sdt)[None])
```

```python
import functools
import math

import jax
import jax.numpy as jnp
import numpy as np
from jax import lax
from jax.experimental import pallas as pl
from jax.experimental.pallas import tpu as pltpu

D_MODEL = 1024
HEAD_DIM = 64
N_HEADS = 8
D_GRP = N_HEADS * HEAD_DIM
LANES = 128
N_PAIRS = D_GRP // LANES
MOBA_BLOCK = 256
MOBA_TOPK = 3
SCORE_LOOKAHEAD = 5
RET_CHUNK = 128
PAGE_SIZE = 128
N_BUCKETS = 32
MAX_DISTANCE = 128
ROPE_BASE = 10000.0
EPS = 1e-6
NEG = -1e30
BIG = 1e30
LOWEST = -3e38
LOG2E = math.log2(math.e)
Q_SCALE = HEAD_DIM ** -0.5 * LOG2E
V_ROWS = HEAD_DIM + 16
MOBA_UNROLL = 4
SAMPLE_UNROLL = 16
VMEM_LIMIT = 56 * 1024 * 1024

F32 = jnp.float32
BF16 = jnp.bfloat16
NT_DIMS = (((1,), (1,)), ((), ()))


def _silu(g):
    return g * (1.0 / (1.0 + jnp.exp(-g)))


def _head_mean(z):
    lane = lax.broadcasted_iota(jnp.int32, z.shape, 1)
    lo = lane < HEAD_DIM
    s0 = jnp.sum(jnp.where(lo, z, 0.0), axis=-1, keepdims=True)
    s1 = jnp.sum(jnp.where(lo, 0.0, z), axis=-1, keepdims=True)
    return jnp.where(lo, s0, s1) * (1.0 / HEAD_DIM)


def _top_k_blocks(gate, n_valid, blk, axis):
    n_blk = gate.shape[axis]
    g = jnp.where(blk < n_valid, gate, NEG)
    sel = jnp.zeros(gate.shape, F32)
    for _ in range(MOBA_TOPK):
        mx = jnp.max(g, axis=axis, keepdims=True)
        first = jnp.min(jnp.where(g == mx, blk, float(n_blk)), axis=axis, keepdims=True)
        hit = blk == first
        sel = jnp.where(hit, 1.0, sel)
        g = jnp.where(hit, LOWEST, g)
    return jnp.where(blk < n_valid, sel, 0.0)


def _proj_kernel(x_ref, cos_ref, sin_ref, lng_ref, w_ref, qg_ref, kg_ref,
                 qpm_ref, kpm_ref, k_ref, v_ref, vt_ref, km_ref, ga_ref, gr_ref,
                 qr_ref, kr_ref, vr_ref, *, feature_major):
    x = x_ref[...]
    ms = jnp.mean(x * x, axis=-1, keepdims=True)
    h = (x * lax.rsqrt(ms + EPS) * lng_ref[...]).astype(BF16)

    def seg(s):
        return jnp.dot(h, w_ref[:, s * D_GRP:(s + 1) * D_GRP], preferred_element_type=F32)

    def chunks(z):
        return [z[:, c * LANES:(c + 1) * LANES] for c in range(N_PAIRS)]

    qa = seg(0)
    for c, z in enumerate(chunks(qa)):
        g = qg_ref[:, c * LANES:(c + 1) * LANES]
        zn = z * lax.rsqrt(_head_mean(z * z) + EPS) * g
        qpm_ref[c] = (zn * Q_SCALE).astype(BF16)
    ka = seg(1)
    kn = []
    for c, z in enumerate(chunks(ka)):
        g = kg_ref[:, c * LANES:(c + 1) * LANES]
        zn = z * lax.rsqrt(_head_mean(z * z) + EPS) * g
        kpm_ref[c] = zn.astype(BF16)
        kn.append(zn)
    kn = jnp.concatenate(kn, axis=1)
    for b in range(km_ref.shape[1]):
        km_ref[0, b:b + 1, :] = jnp.mean(kn[b * MOBA_BLOCK:(b + 1) * MOBA_BLOCK], axis=0, keepdims=True)
    va = seg(2)
    vat = va.T
    ones = jnp.ones((V_ROWS - HEAD_DIM, MOBA_BLOCK), F32)
    for b in range(vt_ref.shape[0]):
        vb = vat[:, b * MOBA_BLOCK:(b + 1) * MOBA_BLOCK]
        vt_ref[b] = jnp.concatenate(
            [t for h in range(N_HEADS) for t in (vb[h * HEAD_DIM:(h + 1) * HEAD_DIM], ones)],
            axis=0).astype(BF16)
    if feature_major:
        k_ref[0] = kn.T
        v_ref[0] = vat
    else:
        k_ref[...] = kn
        v_ref[...] = va
    ga_ref[...] = seg(3)

    cos = cos_ref[...]
    sin = sin_ref[...]
    lane = lax.broadcasted_iota(jnp.int32, cos.shape, 1)
    first_half = (lane % HEAD_DIM) < (HEAD_DIM // 2)

    def rope(z):
        swapped = jnp.where(first_half,
                            pltpu.roll(z, LANES - HEAD_DIM // 2, 1),
                            pltpu.roll(z, HEAD_DIM // 2, 1))
        return z * cos + swapped * sin

    qr = seg(4)
    for c, z in enumerate(chunks(qr)):
        qr_ref[:, c * LANES:(c + 1) * LANES] = rope(z)
    kr = seg(5)
    for c, z in enumerate(chunks(kr)):
        kr_ref[:, c * LANES:(c + 1) * LANES] = rope(z) * (HEAD_DIM ** -0.5)
    vr_ref[...] = seg(6)
    gr_ref[...] = seg(7)


def _proj(x, cos, sin, ln_g, w_bf, qg, kg, *, tm, seq_len, feature_major):
    R = x.shape[0]
    assert tm % MOBA_BLOCK == 0 and R % tm == 0 and cos.shape[0] % tm == 0 and seq_len % tm == 0
    n_pos = cos.shape[0] // tm
    nb = tm // MOBA_BLOCK
    nt = seq_len // tm
    row = lambda i: (i, 0)
    full = lambda i: (0, 0)
    wide = pl.BlockSpec((tm, D_GRP), row)
    wide_shape = jax.ShapeDtypeStruct((R, D_GRP), F32)
    if feature_major:
        kv_shape = jax.ShapeDtypeStruct((R // seq_len, D_GRP, seq_len), F32)
        kv_spec = pl.BlockSpec((1, D_GRP, tm), lambda i: (i // nt, 0, i % nt))
    else:
        kv_shape, kv_spec = wide_shape, wide
    out_shape = (
        jax.ShapeDtypeStruct((N_PAIRS, R, LANES), BF16),
        jax.ShapeDtypeStruct((N_PAIRS, R, LANES), BF16),
        kv_shape,
        kv_shape,
        jax.ShapeDtypeStruct((R // MOBA_BLOCK, N_HEADS * V_ROWS, MOBA_BLOCK), BF16),
        jax.ShapeDtypeStruct((R // tm, nb, D_GRP), F32),
        wide_shape,
        wide_shape,
        wide_shape,
        wide_shape,
        wide_shape,
    )
    pm = pl.BlockSpec((N_PAIRS, tm, LANES), lambda i: (0, i, 0))
    out_specs = (
        pm, pm, kv_spec, kv_spec,
        pl.BlockSpec((nb, N_HEADS * V_ROWS, MOBA_BLOCK), lambda i: (i, 0, 0)),
        pl.BlockSpec((1, nb, D_GRP), lambda i: (i, 0, 0)),
        wide, wide, wide, wide, wide,
    )
    in_specs = [
        pl.BlockSpec((tm, D_MODEL), row),
        pl.BlockSpec((tm, LANES), lambda i: (i % n_pos, 0)),
        pl.BlockSpec((tm, LANES), lambda i: (i % n_pos, 0)),
        pl.BlockSpec((1, D_MODEL), full),
        pl.BlockSpec((D_MODEL, 8 * D_GRP), full),
        pl.BlockSpec((1, D_GRP), full),
        pl.BlockSpec((1, D_GRP), full),
    ]
    return pl.pallas_call(
        functools.partial(_proj_kernel, feature_major=feature_major),
        out_shape=out_shape,
        grid=(R // tm,),
        in_specs=in_specs,
        out_specs=out_specs,
        compiler_params=pltpu.CompilerParams(
            dimension_semantics=("arbitrary",), vmem_limit_bytes=VMEM_LIMIT),
        name="proj",
    )(x, cos, sin, ln_g, w_bf, qg, kg)


def _ret_kernel(q_ref, k_ref, v_ref, g_ref, r0_ref, d2_ref, qd_ref, kd_ref, gl_ref, gng_ref,
                out_ref, rfin_ref, r_scr, *, T, L):
    t = pl.program_id(1)

    @pl.when(t == 0)
    def _():
        r_scr[...] = r0_ref[0]

    lane = lax.broadcasted_iota(jnp.int32, (L, LANES), 1)
    lo = lane < HEAD_DIM
    ri = lax.broadcasted_iota(jnp.int32, (LANES, LANES), 0)
    ci = lax.broadcasted_iota(jnp.int32, (LANES, LANES), 1)
    same_head = (ri < HEAD_DIM) == (ci < HEAD_DIM)
    rows_c = min(T, L)

    def load(ref, rows, cols):
        z = ref[rows, cols]
        if rows_c < L:
            z = jnp.concatenate([z, jnp.zeros((L - rows_c, LANES), F32)], axis=0)
        return z

    def chunk(ci_, carry):
        if T > L:
            rows = pl.ds(pl.multiple_of(ci_ * L, L), L)
        else:
            rows = pl.ds(0, rows_c)
        staged = []
        for c in range(N_PAIRS):
            cols = slice(c * LANES, (c + 1) * LANES)
            q = load(q_ref, rows, cols)
            k = load(k_ref, rows, cols)
            vc = load(v_ref, rows, cols).astype(BF16)
            R = r_scr[c]
            qs = jnp.concatenate([jnp.where(lo, q, 0.0), jnp.where(lo, 0.0, q)], axis=0).astype(BF16)
            s = lax.dot_general(qs, k.astype(BF16), NT_DIMS, preferred_element_type=F32)
            cross = jnp.dot(q.astype(BF16), R.astype(BF16), preferred_element_type=F32)
            kdk = (k * kd_ref[:, cols]).T.astype(BF16)
            upd = jnp.dot(kdk, vc, preferred_element_type=F32)
            r_scr[c] = gl_ref[:, cols] * R + jnp.where(same_head, upd, 0.0)
            staged.append((cols, s, cross, vc))
        for c, (cols, s, cross, vc) in enumerate(staged):
            inner2 = jnp.dot((s * d2_ref[c]).astype(BF16), vc, preferred_element_type=F32)
            o = jnp.where(lo, inner2[:L], inner2[L:]) + cross * qd_ref[:, cols]
            mu = _head_mean(o)
            d = o - mu
            var = _head_mean(d * d)
            y = d * lax.rsqrt(var + EPS) * gng_ref[:, cols]
            res = (y * _silu(load(g_ref, rows, cols))).astype(out_ref.dtype)
            out_ref[rows, cols] = res[:rows_c]
        return carry

    lax.fori_loop(0, max(T // L, 1), chunk, 0)

    @pl.when(t == pl.num_programs(1) - 1)
    def _():
        rfin_ref[0] = r_scr[...]


def _retention(q, k, v, g, r0, tabs, gn_g, *, B, S, T, out_dtype):
    d2, qd, kd, gl = tabs
    L = RET_CHUNK
    nt = S // T
    tok = pl.BlockSpec((T, D_GRP), lambda b, t: (b * nt + t, 0))
    state = pl.BlockSpec((1, N_PAIRS, LANES, LANES), lambda b, t: (b, 0, 0, 0))
    const2 = lambda b, t: (0, 0)
    return pl.pallas_call(
        functools.partial(_ret_kernel, T=T, L=L),
        out_shape=(jax.ShapeDtypeStruct((B * S, D_GRP), out_dtype),
                   jax.ShapeDtypeStruct((B, N_PAIRS, LANES, LANES), F32)),
        grid=(B, nt),
        in_specs=[tok, tok, tok, tok, state,
                  pl.BlockSpec((N_PAIRS, 2 * L, L), lambda b, t: (0, 0, 0)),
                  pl.BlockSpec((L, D_GRP), const2), pl.BlockSpec((L, D_GRP), const2),
                  pl.BlockSpec((1, D_GRP), const2), pl.BlockSpec((1, D_GRP), const2)],
        out_specs=(tok, state),
        scratch_shapes=[pltpu.VMEM((N_PAIRS, LANES, LANES), F32)],
        compiler_params=pltpu.CompilerParams(
            dimension_semantics=("arbitrary", "arbitrary"), vmem_limit_bytes=VMEM_LIMIT),
        name="retention",
    )(q, k, v, g, r0, d2, qd, kd, gl, gn_g)


def _ret_tables(log_g, l_true):
    L = RET_CHUNK
    n = jnp.arange(L, dtype=F32)
    diff = n[:, None] - n[None, :]
    decay = jnp.where(diff >= 0, jnp.exp(log_g[:, None, None] * jnp.maximum(diff, 0.0)), 0.0)
    d2 = decay.reshape(N_PAIRS, 2 * L, L)
    per_lane = lambda t: jnp.repeat(t, HEAD_DIM, axis=1)
    qd = per_lane(jnp.exp(log_g[None, :] * (n[:, None] + 1.0)))
    kd = per_lane(jnp.exp(log_g[None, :] * (l_true - 1.0 - n)[:, None]))
    gl = per_lane(jnp.exp(log_g * l_true)[None, :])
    return d2, qd, kd, gl


def _t5_bucket_starts(max_dist):
    d = np.arange(max_dist)
    max_exact = N_BUCKETS // 2
    scaled = (np.log(np.maximum(d, 1).astype(np.float32) / np.float32(max_exact))
              / np.float32(math.log(MAX_DISTANCE / max_exact)) * np.float32(N_BUCKETS - max_exact))
    bucket = np.where(d < max_exact, d, np.minimum(max_exact + scaled.astype(np.int32), N_BUCKETS - 1))
    assert np.all(np.diff(bucket) >= 0)
    return [(int(b), int(d[bucket == b][0])) for b in np.unique(bucket)]


def _bias_tables(rb_ref, bo_ref, bp_ref):
    shape = (MOBA_BLOCK, MOBA_BLOCK)
    d_own = lax.broadcasted_iota(jnp.int32, shape, 1) - lax.broadcasted_iota(jnp.int32, shape, 0)
    starts = _t5_bucket_starts(2 * MOBA_BLOCK)
    assert starts[0] == (0, 0) and starts[-1][0] == N_BUCKETS - 1

    def head(h, carry):
        far = rb_ref[N_BUCKETS - 1, h]
        for ref, d in ((bo_ref, d_own), (bp_ref, d_own + MOBA_BLOCK)):
            val = jnp.full(shape, (rb_ref[0, h] - far) * LOG2E, F32)
            for bkt, lo in starts[1:]:
                val = jnp.where(d >= lo, (rb_ref[bkt, h] - far) * LOG2E, val)
            ref[h] = jnp.where(d >= 0, val, NEG)
        return carry

    lax.fori_loop(0, N_HEADS, head, 0)


def _moba_step(rb_ref, q_ref, k_ref, vt_ref, km_ref, ga_ref, out_ref,
               qm_ref, acc_ref, m_ref, sel_ref, ot_ref, bo_ref, bp_ref, s_scr, p_scr, alpha_scr, *, NI):
    i = pl.program_id(1)

    @pl.when((pl.program_id(0) == 0) & (i == 0))
    def _():
        _bias_tables(rb_ref, bo_ref, bp_ref)

    lane = lax.broadcasted_iota(jnp.int32, (MOBA_BLOCK, LANES), 1)
    blk = lax.broadcasted_iota(jnp.int32, (NI, MOBA_BLOCK), 0).astype(F32)
    i_f = i.astype(F32)
    for c in range(N_PAIRS):
        qc = q_ref[c].astype(F32)
        qm_ref[2 * c] = jnp.where(lane < HEAD_DIM, qc, 0.0).T.astype(BF16)
        qm_ref[2 * c + 1] = jnp.where(lane < HEAD_DIM, 0.0, qc).T.astype(BF16)

    def scores(j, h, bias_ref):
        kj = k_ref[h // 2, pl.ds(pl.multiple_of(j * MOBA_BLOCK, MOBA_BLOCK), MOBA_BLOCK), :]
        s = jnp.dot(kj, qm_ref[h], preferred_element_type=F32)
        return s if bias_ref is None else s + bias_ref[h]

    def values(j, h):
        return vt_ref[j, h * V_ROWS:(h + 1) * V_ROWS, :]

    for h in range(N_HEADS):
        gate = jnp.dot(km_ref[0, h // 2], qm_ref[h], preferred_element_type=F32)
        sel_ref[h] = _top_k_blocks(gate, i_f, blk, 0)

    d_far = _t5_bucket_starts(2 * MOBA_BLOCK)[-1][1]
    k0 = (MOBA_BLOCK - d_far + 1) // 8 * 8
    assert d_far - 1 <= LANES
    LA = SCORE_LOOKAHEAD

    LAST = N_HEADS - 1

    def finish_last(j_prev):
        acc_ref[LAST] = alpha_scr[...] * acc_ref[LAST] + jnp.dot(values(j_prev, LAST), p_scr[...],
                                                                preferred_element_type=F32)

    def key_block(j, j_prev, j_next, own):
        local = {h: scores(j, h, bo_ref) for h in range(LA)} if own else {}
        near = None if own else (j == i - 1).astype(F32)
        if not own:
            finish_last(j_prev)
        for h in range(N_HEADS):
            t = h + LA
            if t < N_HEADS:
                local[t] = scores(j, t, bo_ref if own else None)
            else:
                s_scr[t - N_HEADS] = scores(j_next, t - N_HEADS, None)
            s = local.pop(h) if h in local else s_scr[h]
            if own:
                m_new = jnp.max(s, axis=0, keepdims=True)
                alpha = None
                p = jnp.exp2(s - m_new)
            else:
                corner = s[k0:, :LANES] + near * bp_ref[h, k0:, :LANES]
                s = jnp.concatenate([s[:k0], jnp.concatenate([corner, s[k0:, LANES:]], axis=1)], axis=0)
                selj = sel_ref[h, pl.ds(j, 1), :] > 0.0
                m_old = m_ref[h]
                m_new = jnp.where(selj, jnp.maximum(m_old, jnp.max(s, axis=0, keepdims=True)), m_old)
                alpha = jnp.exp2(m_old - m_new)
                p = jnp.exp2(s - jnp.where(selj, m_new, BIG))
            m_ref[h] = m_new
            pb = p.astype(BF16)
            if h == LAST:
                p_scr[...] = pb
                alpha_scr[...] = jnp.zeros_like(m_new) if own else alpha
                if own:
                    acc_ref[h] = jnp.zeros(acc_ref.shape[1:], F32)
            elif own:
                acc_ref[h] = jnp.dot(values(j, h), pb, preferred_element_type=F32)
            else:
                acc_ref[h] = alpha * acc_ref[h] + jnp.dot(values(j, h), pb, preferred_element_type=F32)

    key_block(i, None, 0, True)

    def past_blocks(n):
        def trip(jj, carry):
            j0 = n * jj
            for r in range(n):
                j = j0 + r
                key_block(j, jnp.where(j == 0, i, j - 1) if r == 0 else j - 1, j + 1, False)
            return carry
        return trip

    n_full = i // MOBA_UNROLL
    lax.fori_loop(0, n_full, past_blocks(MOBA_UNROLL), 0)
    lax.fori_loop(n_full * MOBA_UNROLL, i, past_blocks(1), 0)
    finish_last(jnp.maximum(i - 1, 0))
    for h in range(N_HEADS):
        o = acc_ref[h]
        ot_ref[h * HEAD_DIM:(h + 1) * HEAD_DIM, :] = o[:HEAD_DIM] / o[HEAD_DIM:HEAD_DIM + 1]
    out_ref[...] = (ot_ref[...].T * _silu(ga_ref[...])).astype(out_ref.dtype)


def _sample_start(g, pt_ref, kc_hbm, vc_hbm, pages_ref, sem, *, HP):
    sb = g // 4
    ph = g % 4
    base = (ph % 2) * HP

    def copies(cache_hbm):
        def body(tt, carry):
            for e in range(2):
                t = 2 * tt + e
                pltpu.make_async_copy(cache_hbm.at[pt_ref[sb, base + t]], pages_ref.at[t],
                                      sem.at[0]).start(priority=e)
            return carry

        lax.fori_loop(0, HP // 2, body, 0)

    @pl.when(ph < 2)
    def _():
        copies(kc_hbm)

    @pl.when(ph >= 2)
    def _():
        copies(vc_hbm)


def _sample_stage(g, kc_hbm, q_ref, kn_ref, vn_ref, ga_ref, blast_ref, bnew_ref, out_ref,
                  s_ref, gate_ref, bmax_ref, meff_ref, m_ref, l_ref, acc_ref, qbd_ref, pages_ref, sem,
                  *, HP, NB, TQ):
    ph = g % 4
    HQ = N_HEADS * TQ
    bpst = HP * PAGE_SIZE // MOBA_BLOCK
    ub = math.gcd(bpst, SAMPLE_UNROLL)
    assert ub % 2 == 0
    lane_q = lax.broadcasted_iota(jnp.int32, (HQ, LANES), 1)

    def wait_one(t, carry):
        pltpu.make_async_copy(kc_hbm.at[0], pages_ref.at[t], sem.at[0]).wait()
        return carry

    lax.fori_loop(0, HP, wait_one, 0)

    def block(u):
        return jnp.concatenate([pages_ref[2 * u], pages_ref[2 * u + 1]], axis=1)

    @pl.when(ph == 0)
    def _():
        qt = jnp.concatenate([q_ref[0]] * N_HEADS, axis=0)
        r = lax.broadcasted_iota(jnp.int32, (HQ, D_GRP), 0)
        l = lax.broadcasted_iota(jnp.int32, (HQ, D_GRP), 1)
        qbd_ref[...] = jnp.where(r // TQ == l // HEAD_DIM, qt, 0.0).astype(BF16)
        gate_ref[...] = jnp.zeros(gate_ref.shape, F32)
        bmax_ref[...] = jnp.zeros(bmax_ref.shape, F32)

    @pl.when(ph < 2)
    def _():
        def body(uu, carry):
            gate = gate_ref[...]
            bmax = bmax_ref[...]
            for r in range(0, ub, 2):
                u = ub * uu + r
                kt2 = jnp.concatenate([block(u), block(u + 1)], axis=1).astype(BF16)
                sc2 = jnp.dot(qbd_ref[...], kt2, preferred_element_type=F32)
                for e in range(2):
                    jb = ph * bpst + u + e
                    sc = sc2[:, e * MOBA_BLOCK:(e + 1) * MOBA_BLOCK]
                    gate = jnp.where(lane_q == jb, jnp.sum(sc, axis=1, keepdims=True) * (1.0 / MOBA_BLOCK), gate)
                    sc = sc + jnp.where(jb == NB - 1, blast_ref[...], 0.0)
                    s_ref[jb] = sc
                    bmax = jnp.where(lane_q == jb, jnp.max(sc, axis=1, keepdims=True), bmax)
            gate_ref[...] = gate
            bmax_ref[...] = bmax
            return carry

        lax.fori_loop(0, bpst // ub, body, 0)

    @pl.when(ph == 1)
    def _():
        sel = _top_k_blocks(gate_ref[...], float(NB), lane_q.astype(F32), 1)
        pad = jnp.zeros((LANES - TQ, D_GRP), F32)
        kn = jnp.concatenate([kn_ref[0], pad], axis=0).astype(BF16)
        vn = jnp.concatenate([vn_ref[0], pad], axis=0).astype(BF16)
        sn = lax.dot_general(qbd_ref[...], kn, NT_DIMS, preferred_element_type=F32) + bnew_ref[...]
        m = jnp.maximum(jnp.max(jnp.where(sel > 0.0, bmax_ref[...], NEG), axis=1, keepdims=True),
                        jnp.max(sn, axis=1, keepdims=True))
        m_ref[...] = m
        meff_ref[...] = jnp.where(sel > 0.0, m, BIG)
        pn = jnp.exp2(sn - m)
        l_ref[...] = jnp.sum(pn, axis=1, keepdims=True)
        acc_ref[...] = jnp.dot(pn.astype(BF16), vn, preferred_element_type=F32)

    @pl.when(ph >= 2)
    def _():
        def body(uu, carry):
            meff_all = meff_ref[...]
            l = l_ref[...]
            acc = acc_ref[...]
            for r in range(ub):
                u = ub * uu + r
                jb = (ph - 2) * bpst + u
                meff = jnp.sum(jnp.where(lane_q == jb, meff_all, 0.0), axis=1, keepdims=True)
                p = jnp.exp2(s_ref[jb] - meff)
                l = l + jnp.sum(p, axis=1, keepdims=True)
                acc = acc + lax.dot_general(p.astype(BF16), block(u).astype(BF16), NT_DIMS,
                                            preferred_element_type=F32)
            l_ref[...] = l
            acc_ref[...] = acc
            return carry

        lax.fori_loop(0, bpst // ub, body, 0)

    @pl.when(ph == 3)
    def _():
        o_bd = acc_ref[...] / l_ref[...]
        lane = lax.broadcasted_iota(jnp.int32, (TQ, D_GRP), 1)
        o = jnp.zeros((TQ, D_GRP), F32)
        for h in range(N_HEADS):
            o = jnp.where(lane // HEAD_DIM == h, o_bd[h * TQ:(h + 1) * TQ], o)
        out_ref[0] = o * _silu(ga_ref[0])


N_MOBA_IN, N_MOBA_SCRATCH = 6, 10


def _attention_kernel(pt_ref, *refs, NI, HP, NB, TQ):
    moba_in = refs[:N_MOBA_IN]
    kc_hbm, vc_hbm, qs_ref, kn_ref, vn_ref, gas_ref, blast_ref, bnew_ref, out_ref, outs_ref = refs[N_MOBA_IN:N_MOBA_IN + 10]
    scratch = refs[N_MOBA_IN + 10:]
    moba_scratch = scratch[:N_MOBA_SCRATCH]
    sample_scratch = scratch[N_MOBA_SCRATCH:]
    pages_ref, sem = sample_scratch[-2:]
    g = pl.program_id(0) * NI + pl.program_id(1)
    _sample_start(g, pt_ref, kc_hbm, vc_hbm, pages_ref, sem, HP=HP)
    _moba_step(*moba_in, out_ref, *moba_scratch, NI=NI)
    _sample_stage(g, kc_hbm, qs_ref, kn_ref, vn_ref, gas_ref, blast_ref, bnew_ref, outs_ref,
                  *sample_scratch, HP=HP, NB=NB, TQ=TQ)


def _attention(page_table, rel_bias, qpm, kpm, vt, km_pm, ga, cache_kt, cache_vt,
               q_s, k_new, v_new, ga_s, bias_last, bias_new, *, B, S):
    NI = S // MOBA_BLOCK
    DB, n_pages = page_table.shape
    TQ = q_s.shape[1]
    HQ = N_HEADS * TQ
    HP = n_pages // 2
    NB = n_pages * PAGE_SIZE // MOBA_BLOCK
    assert B * NI == 4 * DB and n_pages % 4 == 0
    assert NB <= LANES and HQ <= LANES and TQ <= LANES
    once = pl.Buffered(1)
    per_sb = pl.BlockSpec((1, TQ, D_GRP), lambda b, i, pt: ((b * NI + i) // 4, 0, 0))
    rows = pl.BlockSpec((MOBA_BLOCK, D_GRP), lambda b, i, pt: (b * NI + i, 0))
    in_specs = [
        pl.BlockSpec(memory_space=pltpu.SMEM),
        pl.BlockSpec((N_PAIRS, MOBA_BLOCK, LANES), lambda b, i, pt: (0, b * NI + i, 0)),
        pl.BlockSpec((N_PAIRS, S, LANES), lambda b, i, pt: (0, b, 0), pipeline_mode=once),
        pl.BlockSpec((NI, N_HEADS * V_ROWS, MOBA_BLOCK), lambda b, i, pt: (b, 0, 0), pipeline_mode=once),
        pl.BlockSpec((1, N_PAIRS, NI, LANES), lambda b, i, pt: (b, 0, 0, 0)),
        rows,
        pl.BlockSpec(memory_space=pl.ANY), pl.BlockSpec(memory_space=pl.ANY),
        per_sb, per_sb, per_sb, per_sb,
        pl.BlockSpec((HQ, MOBA_BLOCK), lambda b, i, pt: (0, 0)),
        pl.BlockSpec((HQ, LANES), lambda b, i, pt: (0, 0)),
    ]
    scratch_shapes = [
        pltpu.VMEM((N_HEADS, LANES, MOBA_BLOCK), BF16),
        pltpu.VMEM((N_HEADS, V_ROWS, MOBA_BLOCK), F32),
        pltpu.VMEM((N_HEADS, 1, MOBA_BLOCK), F32),
        pltpu.VMEM((N_HEADS, NI, MOBA_BLOCK), F32),
        pltpu.VMEM((D_GRP, MOBA_BLOCK), F32),
        pltpu.VMEM((N_HEADS, MOBA_BLOCK, MOBA_BLOCK), F32),
        pltpu.VMEM((N_HEADS, MOBA_BLOCK, MOBA_BLOCK), F32),
        pltpu.VMEM((SCORE_LOOKAHEAD, MOBA_BLOCK, MOBA_BLOCK), F32),
        pltpu.VMEM((MOBA_BLOCK, MOBA_BLOCK), BF16),
        pltpu.VMEM((1, MOBA_BLOCK), F32),
        pltpu.VMEM((NB, HQ, MOBA_BLOCK), F32),
        pltpu.VMEM((HQ, LANES), F32),
        pltpu.VMEM((HQ, LANES), F32),
        pltpu.VMEM((HQ, LANES), F32),
        pltpu.VMEM((HQ, 1), F32),
        pltpu.VMEM((HQ, 1), F32),
        pltpu.VMEM((HQ, D_GRP), F32),
        pltpu.VMEM((HQ, D_GRP), BF16),
        pltpu.VMEM((HP, D_GRP, PAGE_SIZE), F32),
        pltpu.SemaphoreType.DMA((1,)),
    ]
    assert len(scratch_shapes) == N_MOBA_SCRATCH + 10
    grid_spec = pltpu.PrefetchScalarGridSpec(
        num_scalar_prefetch=1, grid=(B, NI), in_specs=in_specs,
        out_specs=(rows, per_sb), scratch_shapes=scratch_shapes)
    return pl.pallas_call(
        functools.partial(_attention_kernel, NI=NI, HP=HP, NB=NB, TQ=TQ),
        out_shape=(jax.ShapeDtypeStruct((B * S, D_GRP), BF16),
                   jax.ShapeDtypeStruct((DB, TQ, D_GRP), F32)),
        grid_spec=grid_spec,
        compiler_params=pltpu.CompilerParams(
            dimension_semantics=("arbitrary", "arbitrary"), vmem_limit_bytes=VMEM_LIMIT),
        name="attention",
    )(page_table, rel_bias, qpm, kpm, vt, km_pm, ga, cache_kt, cache_vt,
      q_s, k_new, v_new, ga_s, bias_last, bias_new)


def _out_kernel(x_ref, a_ref, r_ref, w_ref, y_ref):
    cat = jnp.concatenate([a_ref[...].astype(BF16), r_ref[...].astype(BF16)], axis=1)
    y_ref[...] = x_ref[...] + jnp.dot(cat, w_ref[...], preferred_element_type=F32)


def _out_proj(x, att, ret, w_bf, *, tm):
    R = x.shape[0]
    row = lambda i: (i, 0)
    return pl.pallas_call(
        _out_kernel,
        out_shape=jax.ShapeDtypeStruct((R, D_MODEL), F32),
        grid=(R // tm,),
        in_specs=[pl.BlockSpec((tm, D_MODEL), row), pl.BlockSpec((tm, D_GRP), row),
                  pl.BlockSpec((tm, D_GRP), row), pl.BlockSpec((2 * D_GRP, D_MODEL), lambda i: (0, 0))],
        out_specs=pl.BlockSpec((tm, D_MODEL), row),
        compiler_params=pltpu.CompilerParams(
            dimension_semantics=("arbitrary",), vmem_limit_bytes=VMEM_LIMIT),
        name="out_proj",
    )(x, att, ret, w_bf)


def _rope_tables(pos):
    half = HEAD_DIM // 2
    inv = ROPE_BASE ** (-jnp.arange(half, dtype=F32) / half)
    ang = pos.astype(F32)[:, None] * inv[None, :]
    cos, sin = jnp.cos(ang), jnp.sin(ang)
    return (jnp.tile(jnp.concatenate([cos, cos], axis=1), (1, 2)),
            jnp.tile(jnp.concatenate([-sin, sin], axis=1), (1, 2)))


def _t5_bucket(dist):
    max_exact = N_BUCKETS // 2
    d = jnp.maximum(dist, 1).astype(F32)
    large = max_exact + (jnp.log(d / max_exact) / math.log(MAX_DISTANCE / max_exact)
                         * (N_BUCKETS - max_exact)).astype(jnp.int32)
    large = jnp.minimum(large, N_BUCKETS - 1)
    return jnp.where(dist < max_exact, dist, large)


def _rel_bias_of(dist, rel_bias):
    rb = rel_bias.astype(F32)
    b = (rb[_t5_bucket(jnp.maximum(dist, 0))] - rb[N_BUCKETS - 1]) * LOG2E
    return jnp.where((dist >= 0)[..., None], b, NEG)


def _pair_states_to_blockdiag(state):
    B = state.shape[0]
    s5 = state.reshape(B, N_PAIRS, 2, HEAD_DIM, HEAD_DIM)
    top = jnp.pad(s5[:, :, 0], ((0, 0), (0, 0), (0, 0), (0, HEAD_DIM)))
    bot = jnp.pad(s5[:, :, 1], ((0, 0), (0, 0), (0, 0), (HEAD_DIM, 0)))
    return jnp.concatenate([top, bot], axis=2)


def _blockdiag_to_head_states(bd):
    B = bd.shape[0]
    d = jnp.stack([bd[:, :, :HEAD_DIM, :HEAD_DIM], bd[:, :, HEAD_DIM:, HEAD_DIM:]], axis=2)
    return d.reshape(B, N_HEADS, HEAD_DIM, HEAD_DIM)


def _feature_major_pages(cache):
    n_phys = cache.shape[0]
    return jnp.transpose(cache, (0, 2, 3, 1)).reshape(n_phys, D_GRP, PAGE_SIZE)


def _token_major_5d(xt, B, S):
    return jnp.transpose(xt.reshape(B, N_HEADS, HEAD_DIM, S), (0, 3, 1, 2))[None]


def kernel(x_prompt, x_sample, cache_k, cache_v, state_ret, page_table, ln_g, w_in,
           q_norm_g, k_norm_g, rel_bias, ret_norm_g, w_out):
    depth = ln_g.shape[0]
    assert depth == 1, "single-layer stack"
    B, S, _ = x_prompt.shape
    DB, DS, _ = x_sample.shape
    n_pages = page_table.shape[1]
    past_len = n_pages * PAGE_SIZE
    assert past_len % MOBA_BLOCK == 0 and DS <= MOBA_BLOCK

    w_in_bf = w_in[0].astype(BF16)
    w_out_bf = w_out[0].astype(BF16)
    lng = ln_g[0][None, :]
    qg = jnp.tile(q_norm_g[0], N_HEADS)[None, :]
    kg = jnp.tile(k_norm_g[0], N_HEADS)[None, :]
    gng = ret_norm_g[0][None, :]
    log_g = jnp.log1p(-jnp.exp2(-5.0 - jnp.arange(N_HEADS, dtype=F32)))

    xp = x_prompt.reshape(B * S, D_MODEL)
    cos_p, sin_p = _rope_tables(jnp.arange(S, dtype=jnp.int32))
    (qpm, kpm, kt, vt, vt_blk, km, ga, gr, qr, kr, vr) = _proj(
        xp, cos_p, sin_p, lng, w_in_bf, qg, kg, tm=2 * MOBA_BLOCK, seq_len=S, feature_major=True)
    ret_p, rfin_p = _retention(qr, kr, vr, gr, jnp.zeros((B, N_PAIRS, LANES, LANES), F32),
                               _ret_tables(log_g, float(RET_CHUNK)), gng, B=B, S=S, T=1024, out_dtype=BF16)
    NI = S // MOBA_BLOCK
    km_pm = km.reshape(B, NI, N_PAIRS, LANES).transpose(0, 2, 1, 3).astype(BF16)

    RS = DB * DS
    RP = -(-RS // MOBA_BLOCK) * MOBA_BLOCK
    xs = x_sample.reshape(RS, D_MODEL)
    pos_s = past_len + jnp.arange(DS, dtype=jnp.int32)
    cos_s, sin_s = _rope_tables(jnp.pad(jnp.tile(pos_s, DB), (0, RP - RS)))
    (qpm_s, _, ka_s, va_s, _, _, ga_s, gr_s, qr_s, kr_s, vr_s) = _proj(
        jnp.pad(xs, ((0, RP - RS), (0, 0))), cos_s, sin_s, lng, w_in_bf, qg, kg,
        tm=RP, seq_len=RP, feature_major=False)
    qpm_s = qpm_s[:, :RS]
    ka_s, va_s, ga_s, gr_s, qr_s, kr_s, vr_s = (a[:RS] for a in (ka_s, va_s, ga_s, gr_s, qr_s, kr_s, vr_s))
    ret_s, rfin_s = _retention(qr_s, kr_s, vr_s, gr_s, _pair_states_to_blockdiag(state_ret[0].astype(F32)),
                               _ret_tables(log_g, float(DS)), gng, B=DB, S=DS, T=DS, out_dtype=F32)
    q_s = qpm_s.transpose(1, 0, 2).reshape(DB, DS, D_GRP).astype(F32)
    t = jnp.arange(DS, dtype=jnp.int32)
    kk = jnp.arange(MOBA_BLOCK, dtype=jnp.int32)
    b_last = _rel_bias_of(t[:, None] + (MOBA_BLOCK - kk)[None, :], rel_bias)
    b_last = b_last.transpose(2, 0, 1).reshape(N_HEADS * DS, MOBA_BLOCK)
    b_new = _rel_bias_of(t[:, None] - t[None, :], rel_bias)
    b_new = b_new.transpose(2, 0, 1).reshape(N_HEADS * DS, DS)
    b_new = jnp.pad(b_new, ((0, 0), (0, LANES - DS)), constant_values=NEG)
    att_p, att_s = _attention(page_table, rel_bias.astype(F32), qpm, kpm, vt_blk, km_pm, ga,
                              _feature_major_pages(cache_k[0]), _feature_major_pages(cache_v[0]),
                              q_s, ka_s.reshape(DB, DS, D_GRP), va_s.reshape(DB, DS, D_GRP),
                              ga_s.reshape(DB, DS, D_GRP), b_last, b_new, B=B, S=S)
    y_p = _out_proj(xp, att_p, ret_p, w_out_bf, tm=1024).reshape(B, S, D_MODEL)
    y_s = _out_proj(xs, att_s.reshape(RS, D_GRP), ret_s, w_out_bf, tm=RS).reshape(DB, DS, D_MODEL)

    sdt = state_ret.dtype
    return (y_p, y_s,
            _token_major_5d(kt, B, S), _token_major_5d(vt, B, S),
            _blockdiag_to_head_states(rfin_p).astype(sdt)[None],
            ka_s.reshape(1, DB, DS, N_HEADS, HEAD_DIM), va_s.reshape(1, DB, DS, N_HEADS, HEAD_DIM),
            _blockdiag_to_head_states(rfin_s).astype(sdt)[None])
```

```python
import functools
import math

import jax
import jax.numpy as jnp
import numpy as np
from jax import lax
from jax.experimental import pallas as pl
from jax.experimental.pallas import tpu as pltpu

D_MODEL = 1024
HEAD_DIM = 64
N_HEADS = 8
D_GRP = N_HEADS * HEAD_DIM
LANES = 128
N_PAIRS = D_GRP // LANES
MOBA_BLOCK = 256
MOBA_TOPK = 3
SCORE_LOOKAHEAD = 5
RET_CHUNK = 128
PAGE_SIZE = 128
N_BUCKETS = 32
MAX_DISTANCE = 128
ROPE_BASE = 10000.0
EPS = 1e-6
NEG = -1e30
BIG = 1e30
LOWEST = -3e38
LOG2E = math.log2(math.e)
Q_SCALE = HEAD_DIM ** -0.5 * LOG2E
V_ROWS = HEAD_DIM + 16
RET_UNROLL = 2
MOBA_UNROLL = 4
SAMPLE_UNROLL = 16
VMEM_LIMIT = 56 * 1024 * 1024

F32 = jnp.float32
BF16 = jnp.bfloat16
NT_DIMS = (((1,), (1,)), ((), ()))


def _silu(g):
    return g * (1.0 / (1.0 + jnp.exp(-g)))


def _head_mean(z):
    lane = lax.broadcasted_iota(jnp.int32, z.shape, 1)
    lo = lane < HEAD_DIM
    s0 = jnp.sum(jnp.where(lo, z, 0.0), axis=-1, keepdims=True)
    s1 = jnp.sum(jnp.where(lo, 0.0, z), axis=-1, keepdims=True)
    return jnp.where(lo, s0, s1) * (1.0 / HEAD_DIM)


def _top_k_blocks(gate, n_valid, blk, axis):
    n_blk = gate.shape[axis]
    g = jnp.where(blk < n_valid, gate, NEG)
    sel = jnp.zeros(gate.shape, F32)
    for _ in range(MOBA_TOPK):
        mx = jnp.max(g, axis=axis, keepdims=True)
        first = jnp.min(jnp.where(g == mx, blk, float(n_blk)), axis=axis, keepdims=True)
        hit = blk == first
        sel = jnp.where(hit, 1.0, sel)
        g = jnp.where(hit, LOWEST, g)
    return jnp.where(blk < n_valid, sel, 0.0)


def _proj_kernel(x_ref, cos_ref, sin_ref, lng_ref, w_ref, qg_ref, kg_ref,
                 qpm_ref, kpm_ref, k_ref, v_ref, vt_ref, km_ref, ga_ref, gr_ref,
                 qr_ref, kr_ref, vr_ref, *, feature_major):
    x = x_ref[...]
    ms = jnp.mean(x * x, axis=-1, keepdims=True)
    h = (x * lax.rsqrt(ms + EPS) * lng_ref[...]).astype(BF16)

    def seg(s):
        return jnp.dot(h, w_ref[:, s * D_GRP:(s + 1) * D_GRP], preferred_element_type=F32)

    def chunks(z):
        return [z[:, c * LANES:(c + 1) * LANES] for c in range(N_PAIRS)]

    qa = seg(0)
    for c, z in enumerate(chunks(qa)):
        g = qg_ref[:, c * LANES:(c + 1) * LANES]
        zn = z * lax.rsqrt(_head_mean(z * z) + EPS) * g
        qpm_ref[c] = (zn * Q_SCALE).astype(BF16)
    ka = seg(1)
    kn = []
    for c, z in enumerate(chunks(ka)):
        g = kg_ref[:, c * LANES:(c + 1) * LANES]
        zn = z * lax.rsqrt(_head_mean(z * z) + EPS) * g
        kpm_ref[c] = zn.astype(BF16)
        kn.append(zn)
    kn = jnp.concatenate(kn, axis=1)
    for b in range(km_ref.shape[1]):
        km_ref[0, b:b + 1, :] = jnp.mean(kn[b * MOBA_BLOCK:(b + 1) * MOBA_BLOCK], axis=0, keepdims=True)
    va = seg(2)
    vat = va.T
    ones = jnp.ones((V_ROWS - HEAD_DIM, MOBA_BLOCK), F32)
    for b in range(vt_ref.shape[0]):
        vb = vat[:, b * MOBA_BLOCK:(b + 1) * MOBA_BLOCK]
        vt_ref[b] = jnp.concatenate(
            [t for h in range(N_HEADS) for t in (vb[h * HEAD_DIM:(h + 1) * HEAD_DIM], ones)],
            axis=0).astype(BF16)
    if feature_major:
        k_ref[0] = kn.T
        v_ref[0] = vat
    else:
        k_ref[...] = kn
        v_ref[...] = va
    ga_ref[...] = seg(3)

    cos = cos_ref[...]
    sin = sin_ref[...]
    lane = lax.broadcasted_iota(jnp.int32, cos.shape, 1)
    first_half = (lane % HEAD_DIM) < (HEAD_DIM // 2)

    def rope(z):
        swapped = jnp.where(first_half,
                            pltpu.roll(z, LANES - HEAD_DIM // 2, 1),
                            pltpu.roll(z, HEAD_DIM // 2, 1))
        return z * cos + swapped * sin

    qr = seg(4)
    for c, z in enumerate(chunks(qr)):
        qr_ref[:, c * LANES:(c + 1) * LANES] = rope(z)
    kr = seg(5)
    for c, z in enumerate(chunks(kr)):
        kr_ref[:, c * LANES:(c + 1) * LANES] = rope(z) * (HEAD_DIM ** -0.5)
    vr_ref[...] = seg(6)
    gr_ref[...] = seg(7)


def _proj(x, cos, sin, ln_g, w_bf, qg, kg, *, tm, seq_len, feature_major):
    R = x.shape[0]
    assert tm % MOBA_BLOCK == 0 and R % tm == 0 and cos.shape[0] % tm == 0 and seq_len % tm == 0
    n_pos = cos.shape[0] // tm
    nb = tm // MOBA_BLOCK
    nt = seq_len // tm
    row = lambda i: (i, 0)
    full = lambda i: (0, 0)
    wide = pl.BlockSpec((tm, D_GRP), row)
    wide_shape = jax.ShapeDtypeStruct((R, D_GRP), F32)
    if feature_major:
        kv_shape = jax.ShapeDtypeStruct((R // seq_len, D_GRP, seq_len), F32)
        kv_spec = pl.BlockSpec((1, D_GRP, tm), lambda i: (i // nt, 0, i % nt))
    else:
        kv_shape, kv_spec = wide_shape, wide
    out_shape = (
        jax.ShapeDtypeStruct((N_PAIRS, R, LANES), BF16),
        jax.ShapeDtypeStruct((N_PAIRS, R, LANES), BF16),
        kv_shape,
        kv_shape,
        jax.ShapeDtypeStruct((R // MOBA_BLOCK, N_HEADS * V_ROWS, MOBA_BLOCK), BF16),
        jax.ShapeDtypeStruct((R // tm, nb, D_GRP), F32),
        wide_shape,
        wide_shape,
        wide_shape,
        wide_shape,
        wide_shape,
    )
    pm = pl.BlockSpec((N_PAIRS, tm, LANES), lambda i: (0, i, 0))
    out_specs = (
        pm, pm, kv_spec, kv_spec,
        pl.BlockSpec((nb, N_HEADS * V_ROWS, MOBA_BLOCK), lambda i: (i, 0, 0)),
        pl.BlockSpec((1, nb, D_GRP), lambda i: (i, 0, 0)),
        wide, wide, wide, wide, wide,
    )
    in_specs = [
        pl.BlockSpec((tm, D_MODEL), row),
        pl.BlockSpec((tm, LANES), lambda i: (i % n_pos, 0)),
        pl.BlockSpec((tm, LANES), lambda i: (i % n_pos, 0)),
        pl.BlockSpec((1, D_MODEL), full),
        pl.BlockSpec((D_MODEL, 8 * D_GRP), full),
        pl.BlockSpec((1, D_GRP), full),
        pl.BlockSpec((1, D_GRP), full),
    ]
    return pl.pallas_call(
        functools.partial(_proj_kernel, feature_major=feature_major),
        out_shape=out_shape,
        grid=(R // tm,),
        in_specs=in_specs,
        out_specs=out_specs,
        compiler_params=pltpu.CompilerParams(
            dimension_semantics=("arbitrary",), vmem_limit_bytes=VMEM_LIMIT),
        name="proj",
    )(x, cos, sin, ln_g, w_bf, qg, kg)


def _ret_kernel(q_ref, k_ref, v_ref, g_ref, r0_ref, d2_ref, qd_ref, kd_ref, gl_ref, gng_ref,
                out_ref, rfin_ref, r_scr, *, T, L, NSEQ):
    t = pl.program_id(1)
    batched = T <= L
    assert batched or NSEQ == 1

    if not batched:
        @pl.when(t == 0)
        def _():
            r_scr[...] = r0_ref[0]

    lane = lax.broadcasted_iota(jnp.int32, (L, LANES), 1)
    lo = lane < HEAD_DIM
    ri = lax.broadcasted_iota(jnp.int32, (LANES, LANES), 0)
    ci = lax.broadcasted_iota(jnp.int32, (LANES, LANES), 1)
    same_head = (ri < HEAD_DIM) == (ci < HEAD_DIM)
    rows_c = min(T, L)

    def load(ref, rows, cols):
        z = ref[rows, cols]
        if rows_c < L:
            z = jnp.concatenate([z, jnp.zeros((L - rows_c, LANES), F32)], axis=0)
        return z

    n_chunks = max(T // L, 1)
    per_trip = NSEQ if batched else math.gcd(n_chunks, RET_UNROLL)

    def trip(tt, carry):
        staged = []
        for r in range(per_trip):
            if batched:
                rows = pl.ds(r * rows_c, rows_c)
            else:
                rows = pl.ds(pl.multiple_of((tt * per_trip + r) * L, L), L)
            for c in range(N_PAIRS):
                cols = slice(c * LANES, (c + 1) * LANES)
                q = load(q_ref, rows, cols)
                k = load(k_ref, rows, cols)
                vc = load(v_ref, rows, cols).astype(BF16)
                qs = jnp.concatenate([jnp.where(lo, q, 0.0), jnp.where(lo, 0.0, q)], axis=0).astype(BF16)
                s = lax.dot_general(qs, k.astype(BF16), NT_DIMS, preferred_element_type=F32)
                kdk = (k * kd_ref[:, cols]).T.astype(BF16)
                upd = jnp.dot(kdk, vc, preferred_element_type=F32)
                staged.append([rows, c, cols, s, upd, vc, q.astype(BF16)])
        for c in range(N_PAIRS):
            R = None if batched else r_scr[c]
            for r in range(per_trip):
                item = staged[r * N_PAIRS + c]
                if batched:
                    R = r0_ref[r, c]
                item.append(jnp.dot(item[6], R.astype(BF16), preferred_element_type=F32))
                R = gl_ref[:, item[2]] * R + jnp.where(same_head, item[4], 0.0)
                if batched:
                    rfin_ref[r, c] = R
            if not batched:
                r_scr[c] = R
        for rows, c, cols, s, _, vc, _, cross in staged:
            inner2 = jnp.dot((s * d2_ref[c]).astype(BF16), vc, preferred_element_type=F32)
            o = jnp.where(lo, inner2[:L], inner2[L:]) + cross * qd_ref[:, cols]
            mu = _head_mean(o)
            d = o - mu
            var = _head_mean(d * d)
            y = d * lax.rsqrt(var + EPS) * gng_ref[:, cols]
            res = (y * _silu(load(g_ref, rows, cols))).astype(out_ref.dtype)
            out_ref[rows, cols] = res[:rows_c]
        return carry

    if batched:
        trip(0, 0)
    else:
        lax.fori_loop(0, n_chunks // per_trip, trip, 0)

        @pl.when(t == pl.num_programs(1) - 1)
        def _():
            rfin_ref[0] = r_scr[...]


def _retention(q, k, v, g, r0, tabs, gn_g, *, B, S, T, out_dtype, nseq=1):
    d2, qd, kd, gl = tabs
    L = RET_CHUNK
    nt = S // T
    assert B % nseq == 0 and (nseq == 1 or (nt == 1 and T <= L))
    tok = pl.BlockSpec((nseq * T, D_GRP), lambda b, t: (b * nt + t, 0))
    state = pl.BlockSpec((nseq, N_PAIRS, LANES, LANES), lambda b, t: (b, 0, 0, 0))
    const2 = lambda b, t: (0, 0)
    return pl.pallas_call(
        functools.partial(_ret_kernel, T=T, L=L, NSEQ=nseq),
        out_shape=(jax.ShapeDtypeStruct((B * S, D_GRP), out_dtype),
                   jax.ShapeDtypeStruct((B, N_PAIRS, LANES, LANES), F32)),
        grid=(B // nseq, nt),
        in_specs=[tok, tok, tok, tok, state,
                  pl.BlockSpec((N_PAIRS, 2 * L, L), lambda b, t: (0, 0, 0)),
                  pl.BlockSpec((L, D_GRP), const2), pl.BlockSpec((L, D_GRP), const2),
                  pl.BlockSpec((1, D_GRP), const2), pl.BlockSpec((1, D_GRP), const2)],
        out_specs=(tok, state),
        scratch_shapes=[pltpu.VMEM((N_PAIRS, LANES, LANES), F32)],
        compiler_params=pltpu.CompilerParams(
            dimension_semantics=("arbitrary", "arbitrary"), vmem_limit_bytes=VMEM_LIMIT),
        name="retention",
    )(q, k, v, g, r0, d2, qd, kd, gl, gn_g)


def _ret_tables(log_g, l_true):
    L = RET_CHUNK
    n = jnp.arange(L, dtype=F32)
    diff = n[:, None] - n[None, :]
    decay = jnp.where(diff >= 0, jnp.exp(log_g[:, None, None] * jnp.maximum(diff, 0.0)), 0.0)
    d2 = decay.reshape(N_PAIRS, 2 * L, L)
    per_lane = lambda t: jnp.repeat(t, HEAD_DIM, axis=1)
    qd = per_lane(jnp.exp(log_g[None, :] * (n[:, None] + 1.0)))
    kd = per_lane(jnp.exp(log_g[None, :] * (l_true - 1.0 - n)[:, None]))
    gl = per_lane(jnp.exp(log_g * l_true)[None, :])
    return d2, qd, kd, gl


def _t5_bucket_starts(max_dist):
    d = np.arange(max_dist)
    max_exact = N_BUCKETS // 2
    scaled = (np.log(np.maximum(d, 1).astype(np.float32) / np.float32(max_exact))
              / np.float32(math.log(MAX_DISTANCE / max_exact)) * np.float32(N_BUCKETS - max_exact))
    bucket = np.where(d < max_exact, d, np.minimum(max_exact + scaled.astype(np.int32), N_BUCKETS - 1))
    assert np.all(np.diff(bucket) >= 0)
    return [(int(b), int(d[bucket == b][0])) for b in np.unique(bucket)]


def _bias_tables(rb_ref, bo_ref, bp_ref):
    shape = (MOBA_BLOCK, MOBA_BLOCK)
    d_own = lax.broadcasted_iota(jnp.int32, shape, 1) - lax.broadcasted_iota(jnp.int32, shape, 0)
    starts = _t5_bucket_starts(2 * MOBA_BLOCK)
    assert starts[0] == (0, 0) and starts[-1][0] == N_BUCKETS - 1

    def head(h, carry):
        far = rb_ref[N_BUCKETS - 1, h]
        for ref, d in ((bo_ref, d_own), (bp_ref, d_own + MOBA_BLOCK)):
            val = jnp.full(shape, (rb_ref[0, h] - far) * LOG2E, F32)
            for bkt, lo in starts[1:]:
                val = jnp.where(d >= lo, (rb_ref[bkt, h] - far) * LOG2E, val)
            ref[h] = jnp.where(d >= 0, val, NEG)
        return carry

    lax.fori_loop(0, N_HEADS, head, 0)


def _moba_step(rb_ref, q_ref, k_ref, vt_ref, km_ref, ga_ref, out_ref,
               qm_ref, acc_ref, m_ref, sel_ref, ot_ref, bo_ref, bp_ref, s_scr, p_scr, alpha_scr, *, NI):
    i = pl.program_id(1)

    @pl.when((pl.program_id(0) == 0) & (i == 0))
    def _():
        _bias_tables(rb_ref, bo_ref, bp_ref)

    lane = lax.broadcasted_iota(jnp.int32, (MOBA_BLOCK, LANES), 1)
    blk = lax.broadcasted_iota(jnp.int32, (NI, MOBA_BLOCK), 0).astype(F32)
    i_f = i.astype(F32)
    for c in range(N_PAIRS):
        qc = q_ref[c].astype(F32)
        qm_ref[2 * c] = jnp.where(lane < HEAD_DIM, qc, 0.0).T.astype(BF16)
        qm_ref[2 * c + 1] = jnp.where(lane < HEAD_DIM, 0.0, qc).T.astype(BF16)

    def scores(j, h, bias_ref):
        kj = k_ref[h // 2, pl.ds(pl.multiple_of(j * MOBA_BLOCK, MOBA_BLOCK), MOBA_BLOCK), :]
        s = jnp.dot(kj, qm_ref[h], preferred_element_type=F32)
        return s if bias_ref is None else s + bias_ref[h]

    def values(j, h):
        return vt_ref[j, h * V_ROWS:(h + 1) * V_ROWS, :]

    for h in range(N_HEADS):
        gate = jnp.dot(km_ref[0, h // 2], qm_ref[h], preferred_element_type=F32)
        sel_ref[h] = _top_k_blocks(gate, i_f, blk, 0)

    d_far = _t5_bucket_starts(2 * MOBA_BLOCK)[-1][1]
    k0 = (MOBA_BLOCK - d_far + 1) // 8 * 8
    assert d_far - 1 <= LANES
    LA = SCORE_LOOKAHEAD

    LAST = N_HEADS - 1

    def finish_last(j_prev):
        acc_ref[LAST] = alpha_scr[...] * acc_ref[LAST] + jnp.dot(values(j_prev, LAST), p_scr[...],
                                                                preferred_element_type=F32)

    def key_block(j, j_prev, j_next, own):
        local = {h: scores(j, h, bo_ref) for h in range(LA)} if own else {}
        near = None if own else (j == i - 1).astype(F32)
        if not own:
            finish_last(j_prev)
        for h in range(N_HEADS):
            t = h + LA
            if t < N_HEADS:
                local[t] = scores(j, t, bo_ref if own else None)
            else:
                s_scr[t - N_HEADS] = scores(j_next, t - N_HEADS, None)
            s = local.pop(h) if h in local else s_scr[h]
            if own:
                m_new = jnp.max(s, axis=0, keepdims=True)
                alpha = None
                p = jnp.exp2(s - m_new)
            else:
                corner = s[k0:, :LANES] + near * bp_ref[h, k0:, :LANES]
                s = jnp.concatenate([s[:k0], jnp.concatenate([corner, s[k0:, LANES:]], axis=1)], axis=0)
                selj = sel_ref[h, pl.ds(j, 1), :] > 0.0
                m_old = m_ref[h]
                m_new = jnp.where(selj, jnp.maximum(m_old, jnp.max(s, axis=0, keepdims=True)), m_old)
                alpha = jnp.exp2(m_old - m_new)
                p = jnp.exp2(s - jnp.where(selj, m_new, BIG))
            m_ref[h] = m_new
            pb = p.astype(BF16)
            if h == LAST:
                p_scr[...] = pb
                alpha_scr[...] = jnp.zeros_like(m_new) if own else alpha
                if own:
                    acc_ref[h] = jnp.zeros(acc_ref.shape[1:], F32)
            elif own:
                acc_ref[h] = jnp.dot(values(j, h), pb, preferred_element_type=F32)
            else:
                acc_ref[h] = alpha * acc_ref[h] + jnp.dot(values(j, h), pb, preferred_element_type=F32)

    key_block(i, None, 0, True)

    def past_blocks(n):
        def trip(jj, carry):
            j0 = n * jj
            for r in range(n):
                j = j0 + r
                key_block(j, jnp.where(j == 0, i, j - 1) if r == 0 else j - 1, j + 1, False)
            return carry
        return trip

    n_full = i // MOBA_UNROLL
    lax.fori_loop(0, n_full, past_blocks(MOBA_UNROLL), 0)
    lax.fori_loop(n_full * MOBA_UNROLL, i, past_blocks(1), 0)
    finish_last(jnp.maximum(i - 1, 0))
    for h in range(N_HEADS):
        o = acc_ref[h]
        ot_ref[h * HEAD_DIM:(h + 1) * HEAD_DIM, :] = o[:HEAD_DIM] / o[HEAD_DIM:HEAD_DIM + 1]
    out_ref[...] = (ot_ref[...].T * _silu(ga_ref[...])).astype(out_ref.dtype)


def _sample_start(g, pt_ref, kc_hbm, vc_hbm, pages_ref, sem, *, HP):
    sb = g // 4
    ph = g % 4
    base = (ph % 2) * HP

    def copies(cache_hbm):
        def body(tt, carry):
            for e in range(2):
                t = 2 * tt + e
                pltpu.make_async_copy(cache_hbm.at[pt_ref[sb, base + t]], pages_ref.at[t],
                                      sem.at[0]).start(priority=e)
            return carry

        lax.fori_loop(0, HP // 2, body, 0)

    @pl.when(ph < 2)
    def _():
        copies(kc_hbm)

    @pl.when(ph >= 2)
    def _():
        copies(vc_hbm)


def _sample_stage(g, kc_hbm, q_ref, kn_ref, vn_ref, ga_ref, blast_ref, bnew_ref, out_ref,
                  s_ref, gate_ref, bmax_ref, meff_ref, m_ref, l_ref, acc_ref, qbd_ref, pages_ref, sem,
                  *, HP, NB, TQ):
    ph = g % 4
    HQ = N_HEADS * TQ
    bpst = HP * PAGE_SIZE // MOBA_BLOCK
    ub = math.gcd(bpst, SAMPLE_UNROLL)
    assert ub % 2 == 0
    lane_q = lax.broadcasted_iota(jnp.int32, (HQ, LANES), 1)

    def wait_one(t, carry):
        pltpu.make_async_copy(kc_hbm.at[0], pages_ref.at[t], sem.at[0]).wait()
        return carry

    lax.fori_loop(0, HP, wait_one, 0)

    def block(u):
        return jnp.concatenate([pages_ref[2 * u], pages_ref[2 * u + 1]], axis=1)

    @pl.when(ph == 0)
    def _():
        qt = jnp.concatenate([q_ref[0]] * N_HEADS, axis=0)
        r = lax.broadcasted_iota(jnp.int32, (HQ, D_GRP), 0)
        l = lax.broadcasted_iota(jnp.int32, (HQ, D_GRP), 1)
        qbd_ref[...] = jnp.where(r // TQ == l // HEAD_DIM, qt, 0.0).astype(BF16)
        gate_ref[...] = jnp.zeros(gate_ref.shape, F32)
        bmax_ref[...] = jnp.zeros(bmax_ref.shape, F32)

    @pl.when(ph < 2)
    def _():
        def body(uu, carry):
            gate = gate_ref[...]
            bmax = bmax_ref[...]
            for r in range(0, ub, 2):
                u = ub * uu + r
                kt2 = jnp.concatenate([block(u), block(u + 1)], axis=1).astype(BF16)
                sc2 = jnp.dot(qbd_ref[...], kt2, preferred_element_type=F32)
                for e in range(2):
                    jb = ph * bpst + u + e
                    sc = sc2[:, e * MOBA_BLOCK:(e + 1) * MOBA_BLOCK]
                    gate = jnp.where(lane_q == jb, jnp.sum(sc, axis=1, keepdims=True) * (1.0 / MOBA_BLOCK), gate)
                    sc = sc + jnp.where(jb == NB - 1, blast_ref[...], 0.0)
                    s_ref[jb] = sc
                    bmax = jnp.where(lane_q == jb, jnp.max(sc, axis=1, keepdims=True), bmax)
            gate_ref[...] = gate
            bmax_ref[...] = bmax
            return carry

        lax.fori_loop(0, bpst // ub, body, 0)

    @pl.when(ph == 1)
    def _():
        sel = _top_k_blocks(gate_ref[...], float(NB), lane_q.astype(F32), 1)
        pad = jnp.zeros((LANES - TQ, D_GRP), F32)
        kn = jnp.concatenate([kn_ref[0], pad], axis=0).astype(BF16)
        vn = jnp.concatenate([vn_ref[0], pad], axis=0).astype(BF16)
        sn = lax.dot_general(qbd_ref[...], kn, NT_DIMS, preferred_element_type=F32) + bnew_ref[...]
        m = jnp.maximum(jnp.max(jnp.where(sel > 0.0, bmax_ref[...], NEG), axis=1, keepdims=True),
                        jnp.max(sn, axis=1, keepdims=True))
        m_ref[...] = m
        meff_ref[...] = jnp.where(sel > 0.0, m, BIG)
        pn = jnp.exp2(sn - m)
        l_ref[...] = jnp.sum(pn, axis=1, keepdims=True)
        acc_ref[...] = jnp.dot(pn.astype(BF16), vn, preferred_element_type=F32)

    @pl.when(ph >= 2)
    def _():
        def body(uu, carry):
            meff_all = meff_ref[...]
            l = l_ref[...]
            acc = acc_ref[...]
            for r in range(ub):
                u = ub * uu + r
                jb = (ph - 2) * bpst + u
                meff = jnp.sum(jnp.where(lane_q == jb, meff_all, 0.0), axis=1, keepdims=True)
                p = jnp.exp2(s_ref[jb] - meff)
                l = l + jnp.sum(p, axis=1, keepdims=True)
                acc = acc + lax.dot_general(p.astype(BF16), block(u).astype(BF16), NT_DIMS,
                                            preferred_element_type=F32)
            l_ref[...] = l
            acc_ref[...] = acc
            return carry

        lax.fori_loop(0, bpst // ub, body, 0)

    @pl.when(ph == 3)
    def _():
        o_bd = acc_ref[...] / l_ref[...]
        lane = lax.broadcasted_iota(jnp.int32, (TQ, D_GRP), 1)
        o = jnp.zeros((TQ, D_GRP), F32)
        for h in range(N_HEADS):
            o = jnp.where(lane // HEAD_DIM == h, o_bd[h * TQ:(h + 1) * TQ], o)
        out_ref[0] = o * _silu(ga_ref[0])


N_MOBA_IN, N_MOBA_SCRATCH = 6, 10


def _attention_kernel(pt_ref, *refs, NI, HP, NB, TQ):
    moba_in = refs[:N_MOBA_IN]
    kc_hbm, vc_hbm, qs_ref, kn_ref, vn_ref, gas_ref, blast_ref, bnew_ref, out_ref, outs_ref = refs[N_MOBA_IN:N_MOBA_IN + 10]
    scratch = refs[N_MOBA_IN + 10:]
    moba_scratch = scratch[:N_MOBA_SCRATCH]
    sample_scratch = scratch[N_MOBA_SCRATCH:]
    pages_ref, sem = sample_scratch[-2:]
    g = pl.program_id(0) * NI + pl.program_id(1)
    _sample_start(g, pt_ref, kc_hbm, vc_hbm, pages_ref, sem, HP=HP)
    _moba_step(*moba_in, out_ref, *moba_scratch, NI=NI)
    _sample_stage(g, kc_hbm, qs_ref, kn_ref, vn_ref, gas_ref, blast_ref, bnew_ref, outs_ref,
                  *sample_scratch, HP=HP, NB=NB, TQ=TQ)


def _attention(page_table, rel_bias, qpm, kpm, vt, km_pm, ga, cache_kt, cache_vt,
               q_s, k_new, v_new, ga_s, bias_last, bias_new, *, B, S):
    NI = S // MOBA_BLOCK
    DB, n_pages = page_table.shape
    TQ = q_s.shape[1]
    HQ = N_HEADS * TQ
    HP = n_pages // 2
    NB = n_pages * PAGE_SIZE // MOBA_BLOCK
    assert B * NI == 4 * DB and n_pages % 4 == 0
    assert NB <= LANES and HQ <= LANES and TQ <= LANES
    once = pl.Buffered(1)
    per_sb = pl.BlockSpec((1, TQ, D_GRP), lambda b, i, pt: ((b * NI + i) // 4, 0, 0))
    rows = pl.BlockSpec((MOBA_BLOCK, D_GRP), lambda b, i, pt: (b * NI + i, 0))
    in_specs = [
        pl.BlockSpec(memory_space=pltpu.SMEM),
        pl.BlockSpec((N_PAIRS, MOBA_BLOCK, LANES), lambda b, i, pt: (0, b * NI + i, 0)),
        pl.BlockSpec((N_PAIRS, S, LANES), lambda b, i, pt: (0, b, 0), pipeline_mode=once),
        pl.BlockSpec((NI, N_HEADS * V_ROWS, MOBA_BLOCK), lambda b, i, pt: (b, 0, 0), pipeline_mode=once),
        pl.BlockSpec((1, N_PAIRS, NI, LANES), lambda b, i, pt: (b, 0, 0, 0)),
        rows,
        pl.BlockSpec(memory_space=pl.ANY), pl.BlockSpec(memory_space=pl.ANY),
        per_sb, per_sb, per_sb, per_sb,
        pl.BlockSpec((HQ, MOBA_BLOCK), lambda b, i, pt: (0, 0)),
        pl.BlockSpec((HQ, LANES), lambda b, i, pt: (0, 0)),
    ]
    scratch_shapes = [
        pltpu.VMEM((N_HEADS, LANES, MOBA_BLOCK), BF16),
        pltpu.VMEM((N_HEADS, V_ROWS, MOBA_BLOCK), F32),
        pltpu.VMEM((N_HEADS, 1, MOBA_BLOCK), F32),
        pltpu.VMEM((N_HEADS, NI, MOBA_BLOCK), F32),
        pltpu.VMEM((D_GRP, MOBA_BLOCK), F32),
        pltpu.VMEM((N_HEADS, MOBA_BLOCK, MOBA_BLOCK), F32),
        pltpu.VMEM((N_HEADS, MOBA_BLOCK, MOBA_BLOCK), F32),
        pltpu.VMEM((SCORE_LOOKAHEAD, MOBA_BLOCK, MOBA_BLOCK), F32),
        pltpu.VMEM((MOBA_BLOCK, MOBA_BLOCK), BF16),
        pltpu.VMEM((1, MOBA_BLOCK), F32),
        pltpu.VMEM((NB, HQ, MOBA_BLOCK), F32),
        pltpu.VMEM((HQ, LANES), F32),
        pltpu.VMEM((HQ, LANES), F32),
        pltpu.VMEM((HQ, LANES), F32),
        pltpu.VMEM((HQ, 1), F32),
        pltpu.VMEM((HQ, 1), F32),
        pltpu.VMEM((HQ, D_GRP), F32),
        pltpu.VMEM((HQ, D_GRP), BF16),
        pltpu.VMEM((HP, D_GRP, PAGE_SIZE), F32),
        pltpu.SemaphoreType.DMA((1,)),
    ]
    assert len(scratch_shapes) == N_MOBA_SCRATCH + 10
    grid_spec = pltpu.PrefetchScalarGridSpec(
        num_scalar_prefetch=1, grid=(B, NI), in_specs=in_specs,
        out_specs=(rows, per_sb), scratch_shapes=scratch_shapes)
    return pl.pallas_call(
        functools.partial(_attention_kernel, NI=NI, HP=HP, NB=NB, TQ=TQ),
        out_shape=(jax.ShapeDtypeStruct((B * S, D_GRP), BF16),
                   jax.ShapeDtypeStruct((DB, TQ, D_GRP), F32)),
        grid_spec=grid_spec,
        compiler_params=pltpu.CompilerParams(
            dimension_semantics=("arbitrary", "arbitrary"), vmem_limit_bytes=VMEM_LIMIT),
        name="attention",
    )(page_table, rel_bias, qpm, kpm, vt, km_pm, ga, cache_kt, cache_vt,
      q_s, k_new, v_new, ga_s, bias_last, bias_new)


def _out_kernel(x_ref, a_ref, r_ref, w_ref, y_ref):
    cat = jnp.concatenate([a_ref[...].astype(BF16), r_ref[...].astype(BF16)], axis=1)
    y_ref[...] = x_ref[...] + jnp.dot(cat, w_ref[...], preferred_element_type=F32)


def _out_proj(x, att, ret, w_bf, *, tm):
    R = x.shape[0]
    row = lambda i: (i, 0)
    return pl.pallas_call(
        _out_kernel,
        out_shape=jax.ShapeDtypeStruct((R, D_MODEL), F32),
        grid=(R // tm,),
        in_specs=[pl.BlockSpec((tm, D_MODEL), row), pl.BlockSpec((tm, D_GRP), row),
                  pl.BlockSpec((tm, D_GRP), row), pl.BlockSpec((2 * D_GRP, D_MODEL), lambda i: (0, 0))],
        out_specs=pl.BlockSpec((tm, D_MODEL), row),
        compiler_params=pltpu.CompilerParams(
            dimension_semantics=("arbitrary",), vmem_limit_bytes=VMEM_LIMIT),
        name="out_proj",
    )(x, att, ret, w_bf)


def _rope_tables(pos):
    half = HEAD_DIM // 2
    inv = ROPE_BASE ** (-jnp.arange(half, dtype=F32) / half)
    ang = pos.astype(F32)[:, None] * inv[None, :]
    cos, sin = jnp.cos(ang), jnp.sin(ang)
    return (jnp.tile(jnp.concatenate([cos, cos], axis=1), (1, 2)),
            jnp.tile(jnp.concatenate([-sin, sin], axis=1), (1, 2)))


def _t5_bucket(dist):
    max_exact = N_BUCKETS // 2
    d = jnp.maximum(dist, 1).astype(F32)
    large = max_exact + (jnp.log(d / max_exact) / math.log(MAX_DISTANCE / max_exact)
                         * (N_BUCKETS - max_exact)).astype(jnp.int32)
    large = jnp.minimum(large, N_BUCKETS - 1)
    return jnp.where(dist < max_exact, dist, large)


def _rel_bias_of(dist, rel_bias):
    rb = rel_bias.astype(F32)
    b = (rb[_t5_bucket(jnp.maximum(dist, 0))] - rb[N_BUCKETS - 1]) * LOG2E
    return jnp.where((dist >= 0)[..., None], b, NEG)


def _pair_states_to_blockdiag(state):
    B = state.shape[0]
    s5 = state.reshape(B, N_PAIRS, 2, HEAD_DIM, HEAD_DIM)
    top = jnp.pad(s5[:, :, 0], ((0, 0), (0, 0), (0, 0), (0, HEAD_DIM)))
    bot = jnp.pad(s5[:, :, 1], ((0, 0), (0, 0), (0, 0), (HEAD_DIM, 0)))
    return jnp.concatenate([top, bot], axis=2)


def _blockdiag_to_head_states(bd):
    B = bd.shape[0]
    d = jnp.stack([bd[:, :, :HEAD_DIM, :HEAD_DIM], bd[:, :, HEAD_DIM:, HEAD_DIM:]], axis=2)
    return d.reshape(B, N_HEADS, HEAD_DIM, HEAD_DIM)


def _feature_major_pages(cache):
    n_phys = cache.shape[0]
    return jnp.transpose(cache, (0, 2, 3, 1)).reshape(n_phys, D_GRP, PAGE_SIZE)


def _token_major_5d(xt, B, S):
    return jnp.transpose(xt.reshape(B, N_HEADS, HEAD_DIM, S), (0, 3, 1, 2))[None]


def kernel(x_prompt, x_sample, cache_k, cache_v, state_ret, page_table, ln_g, w_in,
           q_norm_g, k_norm_g, rel_bias, ret_norm_g, w_out):
    depth = ln_g.shape[0]
    assert depth == 1, "single-layer stack"
    B, S, _ = x_prompt.shape
    DB, DS, _ = x_sample.shape
    n_pages = page_table.shape[1]
    past_len = n_pages * PAGE_SIZE
    assert past_len % MOBA_BLOCK == 0 and DS <= MOBA_BLOCK

    w_in_bf = w_in[0].astype(BF16)
    w_out_bf = w_out[0].astype(BF16)
    lng = ln_g[0][None, :]
    qg = jnp.tile(q_norm_g[0], N_HEADS)[None, :]
    kg = jnp.tile(k_norm_g[0], N_HEADS)[None, :]
    gng = ret_norm_g[0][None, :]
    log_g = jnp.log1p(-jnp.exp2(-5.0 - jnp.arange(N_HEADS, dtype=F32)))

    xp = x_prompt.reshape(B * S, D_MODEL)
    cos_p, sin_p = _rope_tables(jnp.arange(S, dtype=jnp.int32))
    (qpm, kpm, kt, vt, vt_blk, km, ga, gr, qr, kr, vr) = _proj(
        xp, cos_p, sin_p, lng, w_in_bf, qg, kg, tm=2 * MOBA_BLOCK, seq_len=S, feature_major=True)
    ret_p, rfin_p = _retention(qr, kr, vr, gr, jnp.zeros((B, N_PAIRS, LANES, LANES), F32),
                               _ret_tables(log_g, float(RET_CHUNK)), gng, B=B, S=S, T=1024, out_dtype=BF16)
    NI = S // MOBA_BLOCK
    km_pm = km.reshape(B, NI, N_PAIRS, LANES).transpose(0, 2, 1, 3).astype(BF16)

    RS = DB * DS
    RP = -(-RS // MOBA_BLOCK) * MOBA_BLOCK
    xs = x_sample.reshape(RS, D_MODEL)
    pos_s = past_len + jnp.arange(DS, dtype=jnp.int32)
    cos_s, sin_s = _rope_tables(jnp.pad(jnp.tile(pos_s, DB), (0, RP - RS)))
    (qpm_s, _, ka_s, va_s, _, _, ga_s, gr_s, qr_s, kr_s, vr_s) = _proj(
        jnp.pad(xs, ((0, RP - RS), (0, 0))), cos_s, sin_s, lng, w_in_bf, qg, kg,
        tm=RP, seq_len=RP, feature_major=False)
    qpm_s = qpm_s[:, :RS]
    ka_s, va_s, ga_s, gr_s, qr_s, kr_s, vr_s = (a[:RS] for a in (ka_s, va_s, ga_s, gr_s, qr_s, kr_s, vr_s))
    ret_s, rfin_s = _retention(qr_s, kr_s, vr_s, gr_s, _pair_states_to_blockdiag(state_ret[0].astype(F32)),
                               _ret_tables(log_g, float(DS)), gng, B=DB, S=DS, T=DS, out_dtype=F32,
                               nseq=math.gcd(DB, 8) if DS <= RET_CHUNK else 1)
    q_s = qpm_s.transpose(1, 0, 2).reshape(DB, DS, D_GRP).astype(F32)
    t = jnp.arange(DS, dtype=jnp.int32)
    kk = jnp.arange(MOBA_BLOCK, dtype=jnp.int32)
    b_last = _rel_bias_of(t[:, None] + (MOBA_BLOCK - kk)[None, :], rel_bias)
    b_last = b_last.transpose(2, 0, 1).reshape(N_HEADS * DS, MOBA_BLOCK)
    b_new = _rel_bias_of(t[:, None] - t[None, :], rel_bias)
    b_new = b_new.transpose(2, 0, 1).reshape(N_HEADS * DS, DS)
    b_new = jnp.pad(b_new, ((0, 0), (0, LANES - DS)), constant_values=NEG)
    att_p, att_s = _attention(page_table, rel_bias.astype(F32), qpm, kpm, vt_blk, km_pm, ga,
                              _feature_major_pages(cache_k[0]), _feature_major_pages(cache_v[0]),
                              q_s, ka_s.reshape(DB, DS, D_GRP), va_s.reshape(DB, DS, D_GRP),
                              ga_s.reshape(DB, DS, D_GRP), b_last, b_new, B=B, S=S)
    y_p = _out_proj(xp, att_p, ret_p, w_out_bf, tm=1024).reshape(B, S, D_MODEL)
    y_s = _out_proj(xs, att_s.reshape(RS, D_GRP), ret_s, w_out_bf, tm=RS).reshape(DB, DS, D_MODEL)

    sdt = state_ret.dtype
    return (y_p, y_s,
            _token_major_5d(kt, B, S), _token_major_5d(vt, B, S),
            _blockdiag_to_head_states(rfin_p).astype(sdt)[None],
            ka_s.reshape(1, DB, DS, N_HEADS, HEAD_DIM), va_s.reshape(1, DB, DS, N_HEADS, HEAD_DIM),
            _blockdiag_to_head_states(rfin_s).astype(sdt)[None])
```

```python
import functools
import math

import jax
import jax.numpy as jnp
import numpy as np
from jax import lax
from jax.experimental import pallas as pl
from jax.experimental.pallas import tpu as pltpu

D_MODEL = 1024
HEAD_DIM = 64
N_HEADS = 8
D_GRP = N_HEADS * HEAD_DIM
LANES = 128
N_PAIRS = D_GRP // LANES
MOBA_BLOCK = 256
MOBA_TOPK = 3
SCORE_LOOKAHEAD = 5
RET_CHUNK = 128
PAGE_SIZE = 128
N_BUCKETS = 32
MAX_DISTANCE = 128
ROPE_BASE = 10000.0
EPS = 1e-6
NEG = -1e30
BIG = 1e30
LOWEST = -3e38
LOG2E = math.log2(math.e)
Q_SCALE = HEAD_DIM ** -0.5 * LOG2E
V_ROWS = HEAD_DIM + 16
RET_UNROLL = 4
MOBA_UNROLL = 4
SAMPLE_UNROLL = 16
VMEM_LIMIT = 56 * 1024 * 1024

F32 = jnp.float32
BF16 = jnp.bfloat16
NT_DIMS = (((1,), (1,)), ((), ()))


def _silu(g):
    return g * (1.0 / (1.0 + jnp.exp(-g)))


def _head_mean(z):
    lane = lax.broadcasted_iota(jnp.int32, z.shape, 1)
    lo = lane < HEAD_DIM
    s0 = jnp.sum(jnp.where(lo, z, 0.0), axis=-1, keepdims=True)
    s1 = jnp.sum(jnp.where(lo, 0.0, z), axis=-1, keepdims=True)
    return jnp.where(lo, s0, s1) * (1.0 / HEAD_DIM)


def _top_k_blocks(gate, n_valid, blk, axis):
    n_blk = gate.shape[axis]
    g = jnp.where(blk < n_valid, gate, NEG)
    sel = jnp.zeros(gate.shape, F32)
    for _ in range(MOBA_TOPK):
        mx = jnp.max(g, axis=axis, keepdims=True)
        first = jnp.min(jnp.where(g == mx, blk, float(n_blk)), axis=axis, keepdims=True)
        hit = blk == first
        sel = jnp.where(hit, 1.0, sel)
        g = jnp.where(hit, LOWEST, g)
    return jnp.where(blk < n_valid, sel, 0.0)


def _proj_kernel(x_ref, cos_ref, sin_ref, lng_ref, w_ref, qg_ref, kg_ref,
                 qpm_ref, kpm_ref, k_ref, v_ref, vt_ref, km_ref, ga_ref, gr_ref,
                 qr_ref, kr_ref, vr_ref, *, feature_major):
    x = x_ref[...]
    ms = jnp.mean(x * x, axis=-1, keepdims=True)
    h = (x * lax.rsqrt(ms + EPS) * lng_ref[...]).astype(BF16)

    def seg(s):
        return jnp.dot(h, w_ref[:, s * D_GRP:(s + 1) * D_GRP], preferred_element_type=F32)

    def chunks(z):
        return [z[:, c * LANES:(c + 1) * LANES] for c in range(N_PAIRS)]

    qa = seg(0)
    for c, z in enumerate(chunks(qa)):
        g = qg_ref[:, c * LANES:(c + 1) * LANES]
        zn = z * lax.rsqrt(_head_mean(z * z) + EPS) * g
        qpm_ref[c] = (zn * Q_SCALE).astype(BF16)
    ka = seg(1)
    kn = []
    for c, z in enumerate(chunks(ka)):
        g = kg_ref[:, c * LANES:(c + 1) * LANES]
        zn = z * lax.rsqrt(_head_mean(z * z) + EPS) * g
        kpm_ref[c] = zn.astype(BF16)
        kn.append(zn)
    kn = jnp.concatenate(kn, axis=1)
    for b in range(km_ref.shape[1]):
        km_ref[0, b:b + 1, :] = jnp.mean(kn[b * MOBA_BLOCK:(b + 1) * MOBA_BLOCK], axis=0, keepdims=True)
    va = seg(2)
    vat = va.T
    ones = jnp.ones((V_ROWS - HEAD_DIM, MOBA_BLOCK), F32)
    for b in range(vt_ref.shape[0]):
        vb = vat[:, b * MOBA_BLOCK:(b + 1) * MOBA_BLOCK]
        vt_ref[b] = jnp.concatenate(
            [t for h in range(N_HEADS) for t in (vb[h * HEAD_DIM:(h + 1) * HEAD_DIM], ones)],
            axis=0).astype(BF16)
    if feature_major:
        k_ref[0] = kn.T
        v_ref[0] = vat
    else:
        k_ref[...] = kn
        v_ref[...] = va
    ga_ref[...] = seg(3)

    cos = cos_ref[...]
    sin = sin_ref[...]
    lane = lax.broadcasted_iota(jnp.int32, cos.shape, 1)
    first_half = (lane % HEAD_DIM) < (HEAD_DIM // 2)

    def rope(z):
        swapped = jnp.where(first_half,
                            pltpu.roll(z, LANES - HEAD_DIM // 2, 1),
                            pltpu.roll(z, HEAD_DIM // 2, 1))
        return z * cos + swapped * sin

    qr = seg(4)
    for c, z in enumerate(chunks(qr)):
        qr_ref[:, c * LANES:(c + 1) * LANES] = rope(z)
    kr = seg(5)
    for c, z in enumerate(chunks(kr)):
        kr_ref[:, c * LANES:(c + 1) * LANES] = rope(z) * (HEAD_DIM ** -0.5)
    vr_ref[...] = seg(6)
    gr_ref[...] = seg(7)


def _proj(x, cos, sin, ln_g, w_bf, qg, kg, *, tm, seq_len, feature_major):
    R = x.shape[0]
    assert tm % MOBA_BLOCK == 0 and R % tm == 0 and cos.shape[0] % tm == 0 and seq_len % tm == 0
    n_pos = cos.shape[0] // tm
    nb = tm // MOBA_BLOCK
    nt = seq_len // tm
    row = lambda i: (i, 0)
    full = lambda i: (0, 0)
    wide = pl.BlockSpec((tm, D_GRP), row)
    wide_shape = jax.ShapeDtypeStruct((R, D_GRP), F32)
    if feature_major:
        kv_shape = jax.ShapeDtypeStruct((R // seq_len, D_GRP, seq_len), F32)
        kv_spec = pl.BlockSpec((1, D_GRP, tm), lambda i: (i // nt, 0, i % nt))
    else:
        kv_shape, kv_spec = wide_shape, wide
    out_shape = (
        jax.ShapeDtypeStruct((N_PAIRS, R, LANES), BF16),
        jax.ShapeDtypeStruct((N_PAIRS, R, LANES), BF16),
        kv_shape,
        kv_shape,
        jax.ShapeDtypeStruct((R // MOBA_BLOCK, N_HEADS * V_ROWS, MOBA_BLOCK), BF16),
        jax.ShapeDtypeStruct((R // tm, nb, D_GRP), F32),
        wide_shape,
        wide_shape,
        wide_shape,
        wide_shape,
        wide_shape,
    )
    pm = pl.BlockSpec((N_PAIRS, tm, LANES), lambda i: (0, i, 0))
    out_specs = (
        pm, pm, kv_spec, kv_spec,
        pl.BlockSpec((nb, N_HEADS * V_ROWS, MOBA_BLOCK), lambda i: (i, 0, 0)),
        pl.BlockSpec((1, nb, D_GRP), lambda i: (i, 0, 0)),
        wide, wide, wide, wide, wide,
    )
    in_specs = [
        pl.BlockSpec((tm, D_MODEL), row),
        pl.BlockSpec((tm, LANES), lambda i: (i % n_pos, 0)),
        pl.BlockSpec((tm, LANES), lambda i: (i % n_pos, 0)),
        pl.BlockSpec((1, D_MODEL), full),
        pl.BlockSpec((D_MODEL, 8 * D_GRP), full),
        pl.BlockSpec((1, D_GRP), full),
        pl.BlockSpec((1, D_GRP), full),
    ]
    return pl.pallas_call(
        functools.partial(_proj_kernel, feature_major=feature_major),
        out_shape=out_shape,
        grid=(R // tm,),
        in_specs=in_specs,
        out_specs=out_specs,
        compiler_params=pltpu.CompilerParams(
            dimension_semantics=("arbitrary",), vmem_limit_bytes=VMEM_LIMIT),
        name="proj",
    )(x, cos, sin, ln_g, w_bf, qg, kg)


def _ret_kernel(q_ref, k_ref, v_ref, g_ref, r0_ref, d2_ref, qd_ref, kd_ref, gl_ref, gng_ref,
                out_ref, rfin_ref, r_scr, *, T, L, NSEQ):
    t = pl.program_id(1)
    batched = T <= L
    assert batched or NSEQ == 1

    if not batched:
        @pl.when(t == 0)
        def _():
            r_scr[...] = r0_ref[0]

    lane = lax.broadcasted_iota(jnp.int32, (L, LANES), 1)
    lo = lane < HEAD_DIM
    ri = lax.broadcasted_iota(jnp.int32, (LANES, LANES), 0)
    ci = lax.broadcasted_iota(jnp.int32, (LANES, LANES), 1)
    same_head = (ri < HEAD_DIM) == (ci < HEAD_DIM)
    rows_c = min(T, L)

    def load(ref, rows, cols):
        z = ref[rows, cols]
        if rows_c < L:
            z = jnp.concatenate([z, jnp.zeros((L - rows_c, LANES), F32)], axis=0)
        return z

    n_chunks = max(T // L, 1)
    per_trip = NSEQ if batched else math.gcd(n_chunks, RET_UNROLL)

    def trip(tt, carry):
        staged = []
        for r in range(per_trip):
            if batched:
                rows = pl.ds(r * rows_c, rows_c)
            else:
                rows = pl.ds(pl.multiple_of((tt * per_trip + r) * L, L), L)
            for c in range(N_PAIRS):
                cols = slice(c * LANES, (c + 1) * LANES)
                q = load(q_ref, rows, cols)
                k = load(k_ref, rows, cols)
                vc = load(v_ref, rows, cols).astype(BF16)
                qs = jnp.concatenate([jnp.where(lo, q, 0.0), jnp.where(lo, 0.0, q)], axis=0).astype(BF16)
                s = lax.dot_general(qs, k.astype(BF16), NT_DIMS, preferred_element_type=F32)
                kdk = (k * kd_ref[:, cols]).T.astype(BF16)
                upd = jnp.dot(kdk, vc, preferred_element_type=F32)
                staged.append([rows, c, cols, s, upd, vc, q.astype(BF16)])
        for c in range(N_PAIRS):
            R = None if batched else r_scr[c]
            for r in range(per_trip):
                item = staged[r * N_PAIRS + c]
                if batched:
                    R = r0_ref[r, c]
                item.append(jnp.dot(item[6], R.astype(BF16), preferred_element_type=F32))
                R = gl_ref[:, item[2]] * R + jnp.where(same_head, item[4], 0.0)
                if batched:
                    rfin_ref[r, c] = R
            if not batched:
                r_scr[c] = R
        for rows, c, cols, s, _, vc, _, cross in staged:
            inner2 = jnp.dot((s * d2_ref[c]).astype(BF16), vc, preferred_element_type=F32)
            o = jnp.where(lo, inner2[:L], inner2[L:]) + cross * qd_ref[:, cols]
            mu = _head_mean(o)
            d = o - mu
            var = _head_mean(d * d)
            y = d * lax.rsqrt(var + EPS) * gng_ref[:, cols]
            res = (y * _silu(load(g_ref, rows, cols))).astype(out_ref.dtype)
            out_ref[rows, cols] = res[:rows_c]
        return carry

    if batched:
        trip(0, 0)
    else:
        lax.fori_loop(0, n_chunks // per_trip, trip, 0)

        @pl.when(t == pl.num_programs(1) - 1)
        def _():
            rfin_ref[0] = r_scr[...]


def _retention(q, k, v, g, r0, tabs, gn_g, *, B, S, T, out_dtype, nseq=1):
    d2, qd, kd, gl = tabs
    L = RET_CHUNK
    nt = S // T
    assert B % nseq == 0 and (nseq == 1 or (nt == 1 and T <= L))
    tok = pl.BlockSpec((nseq * T, D_GRP), lambda b, t: (b * nt + t, 0))
    state = pl.BlockSpec((nseq, N_PAIRS, LANES, LANES), lambda b, t: (b, 0, 0, 0))
    const2 = lambda b, t: (0, 0)
    return pl.pallas_call(
        functools.partial(_ret_kernel, T=T, L=L, NSEQ=nseq),
        out_shape=(jax.ShapeDtypeStruct((B * S, D_GRP), out_dtype),
                   jax.ShapeDtypeStruct((B, N_PAIRS, LANES, LANES), F32)),
        grid=(B // nseq, nt),
        in_specs=[tok, tok, tok, tok, state,
                  pl.BlockSpec((N_PAIRS, 2 * L, L), lambda b, t: (0, 0, 0)),
                  pl.BlockSpec((L, D_GRP), const2), pl.BlockSpec((L, D_GRP), const2),
                  pl.BlockSpec((1, D_GRP), const2), pl.BlockSpec((1, D_GRP), const2)],
        out_specs=(tok, state),
        scratch_shapes=[pltpu.VMEM((N_PAIRS, LANES, LANES), F32)],
        compiler_params=pltpu.CompilerParams(
            dimension_semantics=("arbitrary", "arbitrary"), vmem_limit_bytes=VMEM_LIMIT),
        name="retention",
    )(q, k, v, g, r0, d2, qd, kd, gl, gn_g)


def _ret_tables(log_g, l_true):
    L = RET_CHUNK
    n = jnp.arange(L, dtype=F32)
    diff = n[:, None] - n[None, :]
    decay = jnp.where(diff >= 0, jnp.exp(log_g[:, None, None] * jnp.maximum(diff, 0.0)), 0.0)
    d2 = decay.reshape(N_PAIRS, 2 * L, L)
    per_lane = lambda t: jnp.repeat(t, HEAD_DIM, axis=1)
    qd = per_lane(jnp.exp(log_g[None, :] * (n[:, None] + 1.0)))
    kd = per_lane(jnp.exp(log_g[None, :] * (l_true - 1.0 - n)[:, None]))
    gl = per_lane(jnp.exp(log_g * l_true)[None, :])
    return d2, qd, kd, gl


def _t5_bucket_starts(max_dist):
    d = np.arange(max_dist)
    max_exact = N_BUCKETS // 2
    scaled = (np.log(np.maximum(d, 1).astype(np.float32) / np.float32(max_exact))
              / np.float32(math.log(MAX_DISTANCE / max_exact)) * np.float32(N_BUCKETS - max_exact))
    bucket = np.where(d < max_exact, d, np.minimum(max_exact + scaled.astype(np.int32), N_BUCKETS - 1))
    assert np.all(np.diff(bucket) >= 0)
    return [(int(b), int(d[bucket == b][0])) for b in np.unique(bucket)]


def _bias_tables(rb_ref, bo_ref, bp_ref):
    shape = (MOBA_BLOCK, MOBA_BLOCK)
    d_own = lax.broadcasted_iota(jnp.int32, shape, 1) - lax.broadcasted_iota(jnp.int32, shape, 0)
    starts = _t5_bucket_starts(2 * MOBA_BLOCK)
    assert starts[0] == (0, 0) and starts[-1][0] == N_BUCKETS - 1

    def head(h, carry):
        far = rb_ref[N_BUCKETS - 1, h]
        for ref, d in ((bo_ref, d_own), (bp_ref, d_own + MOBA_BLOCK)):
            val = jnp.full(shape, (rb_ref[0, h] - far) * LOG2E, F32)
            for bkt, lo in starts[1:]:
                val = jnp.where(d >= lo, (rb_ref[bkt, h] - far) * LOG2E, val)
            ref[h] = jnp.where(d >= 0, val, NEG)
        return carry

    lax.fori_loop(0, N_HEADS, head, 0)


def _moba_step(rb_ref, q_ref, k_ref, vt_ref, km_ref, ga_ref, out_ref,
               qm_ref, acc_ref, m_ref, sel_ref, ot_ref, bo_ref, bp_ref, s_scr, p_scr, alpha_scr, *, NI):
    i = pl.program_id(1)

    @pl.when((pl.program_id(0) == 0) & (i == 0))
    def _():
        _bias_tables(rb_ref, bo_ref, bp_ref)

    lane = lax.broadcasted_iota(jnp.int32, (MOBA_BLOCK, LANES), 1)
    blk = lax.broadcasted_iota(jnp.int32, (NI, MOBA_BLOCK), 0).astype(F32)
    i_f = i.astype(F32)
    for c in range(N_PAIRS):
        qc = q_ref[c].astype(F32)
        qm_ref[2 * c] = jnp.where(lane < HEAD_DIM, qc, 0.0).T.astype(BF16)
        qm_ref[2 * c + 1] = jnp.where(lane < HEAD_DIM, 0.0, qc).T.astype(BF16)

    def scores(j, h, bias_ref):
        kj = k_ref[h // 2, pl.ds(pl.multiple_of(j * MOBA_BLOCK, MOBA_BLOCK), MOBA_BLOCK), :]
        s = jnp.dot(kj, qm_ref[h], preferred_element_type=F32)
        return s if bias_ref is None else s + bias_ref[h]

    def values(j, h):
        return vt_ref[j, h * V_ROWS:(h + 1) * V_ROWS, :]

    for h in range(N_HEADS):
        gate = jnp.dot(km_ref[0, h // 2], qm_ref[h], preferred_element_type=F32)
        sel_ref[h] = _top_k_blocks(gate, i_f, blk, 0)

    d_far = _t5_bucket_starts(2 * MOBA_BLOCK)[-1][1]
    k0 = (MOBA_BLOCK - d_far + 1) // 8 * 8
    assert d_far - 1 <= LANES
    LA = SCORE_LOOKAHEAD

    LAST = N_HEADS - 1

    def finish_last(j_prev):
        acc_ref[LAST] = alpha_scr[...] * acc_ref[LAST] + jnp.dot(values(j_prev, LAST), p_scr[...],
                                                                preferred_element_type=F32)

    def key_block(j, j_prev, j_next, own):
        local = {h: scores(j, h, bo_ref) for h in range(LA)} if own else {}
        near = None if own else (j == i - 1).astype(F32)
        if not own:
            finish_last(j_prev)
        for h in range(N_HEADS):
            t = h + LA
            if t < N_HEADS:
                local[t] = scores(j, t, bo_ref if own else None)
            else:
                s_scr[t - N_HEADS] = scores(j_next, t - N_HEADS, None)
            s = local.pop(h) if h in local else s_scr[h]
            if own:
                m_new = jnp.max(s, axis=0, keepdims=True)
                alpha = None
                p = jnp.exp2(s - m_new)
            else:
                corner = s[k0:, :LANES] + near * bp_ref[h, k0:, :LANES]
                s = jnp.concatenate([s[:k0], jnp.concatenate([corner, s[k0:, LANES:]], axis=1)], axis=0)
                selj = sel_ref[h, pl.ds(j, 1), :] > 0.0
                m_old = m_ref[h]
                m_new = jnp.where(selj, jnp.maximum(m_old, jnp.max(s, axis=0, keepdims=True)), m_old)
                alpha = jnp.exp2(m_old - m_new)
                p = jnp.exp2(s - jnp.where(selj, m_new, BIG))
            m_ref[h] = m_new
            pb = p.astype(BF16)
            if h == LAST:
                p_scr[...] = pb
                alpha_scr[...] = jnp.zeros_like(m_new) if own else alpha
                if own:
                    acc_ref[h] = jnp.zeros(acc_ref.shape[1:], F32)
            elif own:
                acc_ref[h] = jnp.dot(values(j, h), pb, preferred_element_type=F32)
            else:
                acc_ref[h] = alpha * acc_ref[h] + jnp.dot(values(j, h), pb, preferred_element_type=F32)

    key_block(i, None, 0, True)

    def past_blocks(n, first):
        def trip(jj, carry):
            j0 = first + n * jj
            for r in range(n):
                j = j0 + r
                key_block(j, jnp.where(j == 0, i, j - 1) if r == 0 else j - 1, j + 1, False)
            return carry
        return trip

    done = 0
    n = MOBA_UNROLL
    while n >= 1:
        trips = (i - done) // n
        lax.fori_loop(0, trips, past_blocks(n, done), 0)
        done = done + trips * n
        n //= 2
    finish_last(jnp.maximum(i - 1, 0))
    for h in range(N_HEADS):
        o = acc_ref[h]
        ot_ref[h * HEAD_DIM:(h + 1) * HEAD_DIM, :] = o[:HEAD_DIM] / o[HEAD_DIM:HEAD_DIM + 1]
    out_ref[...] = (ot_ref[...].T * _silu(ga_ref[...])).astype(out_ref.dtype)


def _sample_start(g, pt_ref, kc_hbm, vc_hbm, pages_ref, sem, *, HP):
    sb = g // 4
    ph = g % 4
    base = (ph % 2) * HP

    def copies(cache_hbm):
        def body(tt, carry):
            for e in range(2):
                t = 2 * tt + e
                pltpu.make_async_copy(cache_hbm.at[pt_ref[sb, base + t]], pages_ref.at[t],
                                      sem.at[0]).start(priority=e)
            return carry

        lax.fori_loop(0, HP // 2, body, 0)

    @pl.when(ph < 2)
    def _():
        copies(kc_hbm)

    @pl.when(ph >= 2)
    def _():
        copies(vc_hbm)


def _sample_stage(g, kc_hbm, q_ref, kn_ref, vn_ref, ga_ref, blast_ref, bnew_ref, out_ref,
                  s_ref, gate_ref, bmax_ref, meff_ref, m_ref, l_ref, acc_ref, qbd_ref, pages_ref, sem,
                  *, HP, NB, TQ):
    ph = g % 4
    HQ = N_HEADS * TQ
    bpst = HP * PAGE_SIZE // MOBA_BLOCK
    ub = math.gcd(bpst, SAMPLE_UNROLL)
    assert ub % 2 == 0
    lane_q = lax.broadcasted_iota(jnp.int32, (HQ, LANES), 1)

    def wait_one(t, carry):
        pltpu.make_async_copy(kc_hbm.at[0], pages_ref.at[t], sem.at[0]).wait()
        return carry

    lax.fori_loop(0, HP, wait_one, 0)

    def block(u):
        return jnp.concatenate([pages_ref[2 * u], pages_ref[2 * u + 1]], axis=1)

    @pl.when(ph == 0)
    def _():
        qt = jnp.concatenate([q_ref[0]] * N_HEADS, axis=0)
        r = lax.broadcasted_iota(jnp.int32, (HQ, D_GRP), 0)
        l = lax.broadcasted_iota(jnp.int32, (HQ, D_GRP), 1)
        qbd_ref[...] = jnp.where(r // TQ == l // HEAD_DIM, qt, 0.0).astype(BF16)
        gate_ref[...] = jnp.zeros(gate_ref.shape, F32)
        bmax_ref[...] = jnp.zeros(bmax_ref.shape, F32)

    @pl.when(ph < 2)
    def _():
        def body(uu, carry):
            gate = gate_ref[...]
            bmax = bmax_ref[...]
            for r in range(0, ub, 2):
                u = ub * uu + r
                kt2 = jnp.concatenate([block(u), block(u + 1)], axis=1).astype(BF16)
                sc2 = jnp.dot(qbd_ref[...], kt2, preferred_element_type=F32)
                for e in range(2):
                    jb = ph * bpst + u + e
                    sc = sc2[:, e * MOBA_BLOCK:(e + 1) * MOBA_BLOCK]
                    gate = jnp.where(lane_q == jb, jnp.sum(sc, axis=1, keepdims=True) * (1.0 / MOBA_BLOCK), gate)
                    sc = sc + jnp.where(jb == NB - 1, blast_ref[...], 0.0)
                    s_ref[jb] = sc
                    bmax = jnp.where(lane_q == jb, jnp.max(sc, axis=1, keepdims=True), bmax)
            gate_ref[...] = gate
            bmax_ref[...] = bmax
            return carry

        lax.fori_loop(0, bpst // ub, body, 0)

    @pl.when(ph == 1)
    def _():
        sel = _top_k_blocks(gate_ref[...], float(NB), lane_q.astype(F32), 1)
        pad = jnp.zeros((LANES - TQ, D_GRP), F32)
        kn = jnp.concatenate([kn_ref[0], pad], axis=0).astype(BF16)
        vn = jnp.concatenate([vn_ref[0], pad], axis=0).astype(BF16)
        sn = lax.dot_general(qbd_ref[...], kn, NT_DIMS, preferred_element_type=F32) + bnew_ref[...]
        m = jnp.maximum(jnp.max(jnp.where(sel > 0.0, bmax_ref[...], NEG), axis=1, keepdims=True),
                        jnp.max(sn, axis=1, keepdims=True))
        m_ref[...] = m
        meff_ref[...] = jnp.where(sel > 0.0, m, BIG)
        pn = jnp.exp2(sn - m)
        l_ref[...] = jnp.sum(pn, axis=1, keepdims=True)
        acc_ref[...] = jnp.dot(pn.astype(BF16), vn, preferred_element_type=F32)

    @pl.when(ph >= 2)
    def _():
        def body(uu, carry):
            meff_all = meff_ref[...]
            l = l_ref[...]
            acc = acc_ref[...]
            for r in range(ub):
                u = ub * uu + r
                jb = (ph - 2) * bpst + u
                meff = jnp.sum(jnp.where(lane_q == jb, meff_all, 0.0), axis=1, keepdims=True)
                p = jnp.exp2(s_ref[jb] - meff)
                l = l + jnp.sum(p, axis=1, keepdims=True)
                acc = acc + lax.dot_general(p.astype(BF16), block(u).astype(BF16), NT_DIMS,
                                            preferred_element_type=F32)
            l_ref[...] = l
            acc_ref[...] = acc
            return carry

        lax.fori_loop(0, bpst // ub, body, 0)

    @pl.when(ph == 3)
    def _():
        o_bd = acc_ref[...] / l_ref[...]
        lane = lax.broadcasted_iota(jnp.int32, (TQ, D_GRP), 1)
        o = jnp.zeros((TQ, D_GRP), F32)
        for h in range(N_HEADS):
            o = jnp.where(lane // HEAD_DIM == h, o_bd[h * TQ:(h + 1) * TQ], o)
        out_ref[0] = o * _silu(ga_ref[0])


N_MOBA_IN, N_MOBA_SCRATCH = 6, 10


def _attention_kernel(pt_ref, *refs, NI, HP, NB, TQ):
    moba_in = refs[:N_MOBA_IN]
    kc_hbm, vc_hbm, qs_ref, kn_ref, vn_ref, gas_ref, blast_ref, bnew_ref, out_ref, outs_ref = refs[N_MOBA_IN:N_MOBA_IN + 10]
    scratch = refs[N_MOBA_IN + 10:]
    moba_scratch = scratch[:N_MOBA_SCRATCH]
    sample_scratch = scratch[N_MOBA_SCRATCH:]
    pages_ref, sem = sample_scratch[-2:]
    g = pl.program_id(0) * NI + pl.program_id(1)
    _sample_start(g, pt_ref, kc_hbm, vc_hbm, pages_ref, sem, HP=HP)
    _moba_step(*moba_in, out_ref, *moba_scratch, NI=NI)
    _sample_stage(g, kc_hbm, qs_ref, kn_ref, vn_ref, gas_ref, blast_ref, bnew_ref, outs_ref,
                  *sample_scratch, HP=HP, NB=NB, TQ=TQ)


def _attention(page_table, rel_bias, qpm, kpm, vt, km_pm, ga, cache_kt, cache_vt,
               q_s, k_new, v_new, ga_s, bias_last, bias_new, *, B, S):
    NI = S // MOBA_BLOCK
    DB, n_pages = page_table.shape
    TQ = q_s.shape[1]
    HQ = N_HEADS * TQ
    HP = n_pages // 2
    NB = n_pages * PAGE_SIZE // MOBA_BLOCK
    assert B * NI == 4 * DB and n_pages % 4 == 0
    assert NB <= LANES and HQ <= LANES and TQ <= LANES
    once = pl.Buffered(1)
    per_sb = pl.BlockSpec((1, TQ, D_GRP), lambda b, i, pt: ((b * NI + i) // 4, 0, 0))
    rows = pl.BlockSpec((MOBA_BLOCK, D_GRP), lambda b, i, pt: (b * NI + i, 0))
    in_specs = [
        pl.BlockSpec(memory_space=pltpu.SMEM),
        pl.BlockSpec((N_PAIRS, MOBA_BLOCK, LANES), lambda b, i, pt: (0, b * NI + i, 0)),
        pl.BlockSpec((N_PAIRS, S, LANES), lambda b, i, pt: (0, b, 0), pipeline_mode=once),
        pl.BlockSpec((NI, N_HEADS * V_ROWS, MOBA_BLOCK), lambda b, i, pt: (b, 0, 0), pipeline_mode=once),
        pl.BlockSpec((1, N_PAIRS, NI, LANES), lambda b, i, pt: (b, 0, 0, 0)),
        rows,
        pl.BlockSpec(memory_space=pl.ANY), pl.BlockSpec(memory_space=pl.ANY),
        per_sb, per_sb, per_sb, per_sb,
        pl.BlockSpec((HQ, MOBA_BLOCK), lambda b, i, pt: (0, 0)),
        pl.BlockSpec((HQ, LANES), lambda b, i, pt: (0, 0)),
    ]
    scratch_shapes = [
        pltpu.VMEM((N_HEADS, LANES, MOBA_BLOCK), BF16),
        pltpu.VMEM((N_HEADS, V_ROWS, MOBA_BLOCK), F32),
        pltpu.VMEM((N_HEADS, 1, MOBA_BLOCK), F32),
        pltpu.VMEM((N_HEADS, NI, MOBA_BLOCK), F32),
        pltpu.VMEM((D_GRP, MOBA_BLOCK), F32),
        pltpu.VMEM((N_HEADS, MOBA_BLOCK, MOBA_BLOCK), F32),
        pltpu.VMEM((N_HEADS, MOBA_BLOCK, MOBA_BLOCK), F32),
        pltpu.VMEM((SCORE_LOOKAHEAD, MOBA_BLOCK, MOBA_BLOCK), F32),
        pltpu.VMEM((MOBA_BLOCK, MOBA_BLOCK), BF16),
        pltpu.VMEM((1, MOBA_BLOCK), F32),
        pltpu.VMEM((NB, HQ, MOBA_BLOCK), F32),
        pltpu.VMEM((HQ, LANES), F32),
        pltpu.VMEM((HQ, LANES), F32),
        pltpu.VMEM((HQ, LANES), F32),
        pltpu.VMEM((HQ, 1), F32),
        pltpu.VMEM((HQ, 1), F32),
        pltpu.VMEM((HQ, D_GRP), F32),
        pltpu.VMEM((HQ, D_GRP), BF16),
        pltpu.VMEM((HP, D_GRP, PAGE_SIZE), F32),
        pltpu.SemaphoreType.DMA((1,)),
    ]
    assert len(scratch_shapes) == N_MOBA_SCRATCH + 10
    grid_spec = pltpu.PrefetchScalarGridSpec(
        num_scalar_prefetch=1, grid=(B, NI), in_specs=in_specs,
        out_specs=(rows, per_sb), scratch_shapes=scratch_shapes)
    return pl.pallas_call(
        functools.partial(_attention_kernel, NI=NI, HP=HP, NB=NB, TQ=TQ),
        out_shape=(jax.ShapeDtypeStruct((B * S, D_GRP), BF16),
                   jax.ShapeDtypeStruct((DB, TQ, D_GRP), F32)),
        grid_spec=grid_spec,
        compiler_params=pltpu.CompilerParams(
            dimension_semantics=("arbitrary", "arbitrary"), vmem_limit_bytes=VMEM_LIMIT),
        name="attention",
    )(page_table, rel_bias, qpm, kpm, vt, km_pm, ga, cache_kt, cache_vt,
      q_s, k_new, v_new, ga_s, bias_last, bias_new)


def _out_kernel(x_ref, a_ref, r_ref, w_ref, y_ref):
    cat = jnp.concatenate([a_ref[...].astype(BF16), r_ref[...].astype(BF16)], axis=1)
    y_ref[...] = x_ref[...] + jnp.dot(cat, w_ref[...], preferred_element_type=F32)


def _out_proj(x, att, ret, w_bf, *, tm):
    R = x.shape[0]
    row = lambda i: (i, 0)
    return pl.pallas_call(
        _out_kernel,
        out_shape=jax.ShapeDtypeStruct((R, D_MODEL), F32),
        grid=(R // tm,),
        in_specs=[pl.BlockSpec((tm, D_MODEL), row), pl.BlockSpec((tm, D_GRP), row),
                  pl.BlockSpec((tm, D_GRP), row), pl.BlockSpec((2 * D_GRP, D_MODEL), lambda i: (0, 0))],
        out_specs=pl.BlockSpec((tm, D_MODEL), row),
        compiler_params=pltpu.CompilerParams(
            dimension_semantics=("arbitrary",), vmem_limit_bytes=VMEM_LIMIT),
        name="out_proj",
    )(x, att, ret, w_bf)


def _rope_tables(pos):
    half = HEAD_DIM // 2
    inv = ROPE_BASE ** (-jnp.arange(half, dtype=F32) / half)
    ang = pos.astype(F32)[:, None] * inv[None, :]
    cos, sin = jnp.cos(ang), jnp.sin(ang)
    return (jnp.tile(jnp.concatenate([cos, cos], axis=1), (1, 2)),
            jnp.tile(jnp.concatenate([-sin, sin], axis=1), (1, 2)))


def _t5_bucket(dist):
    max_exact = N_BUCKETS // 2
    d = jnp.maximum(dist, 1).astype(F32)
    large = max_exact + (jnp.log(d / max_exact) / math.log(MAX_DISTANCE / max_exact)
                         * (N_BUCKETS - max_exact)).astype(jnp.int32)
    large = jnp.minimum(large, N_BUCKETS - 1)
    return jnp.where(dist < max_exact, dist, large)


def _rel_bias_of(dist, rel_bias):
    rb = rel_bias.astype(F32)
    b = (rb[_t5_bucket(jnp.maximum(dist, 0))] - rb[N_BUCKETS - 1]) * LOG2E
    return jnp.where((dist >= 0)[..., None], b, NEG)


def _pair_states_to_blockdiag(state):
    B = state.shape[0]
    s5 = state.reshape(B, N_PAIRS, 2, HEAD_DIM, HEAD_DIM)
    top = jnp.pad(s5[:, :, 0], ((0, 0), (0, 0), (0, 0), (0, HEAD_DIM)))
    bot = jnp.pad(s5[:, :, 1], ((0, 0), (0, 0), (0, 0), (HEAD_DIM, 0)))
    return jnp.concatenate([top, bot], axis=2)


def _blockdiag_to_head_states(bd):
    B = bd.shape[0]
    d = jnp.stack([bd[:, :, :HEAD_DIM, :HEAD_DIM], bd[:, :, HEAD_DIM:, HEAD_DIM:]], axis=2)
    return d.reshape(B, N_HEADS, HEAD_DIM, HEAD_DIM)


def _feature_major_pages(cache):
    n_phys = cache.shape[0]
    return jnp.transpose(cache, (0, 2, 3, 1)).reshape(n_phys, D_GRP, PAGE_SIZE)


def _token_major_5d(xt, B, S):
    return jnp.transpose(xt.reshape(B, N_HEADS, HEAD_DIM, S), (0, 3, 1, 2))[None]


def kernel(x_prompt, x_sample, cache_k, cache_v, state_ret, page_table, ln_g, w_in,
           q_norm_g, k_norm_g, rel_bias, ret_norm_g, w_out):
    depth = ln_g.shape[0]
    assert depth == 1, "single-layer stack"
    B, S, _ = x_prompt.shape
    DB, DS, _ = x_sample.shape
    n_pages = page_table.shape[1]
    past_len = n_pages * PAGE_SIZE
    assert past_len % MOBA_BLOCK == 0 and DS <= MOBA_BLOCK

    w_in_bf = w_in[0].astype(BF16)
    w_out_bf = w_out[0].astype(BF16)
    lng = ln_g[0][None, :]
    qg = jnp.tile(q_norm_g[0], N_HEADS)[None, :]
    kg = jnp.tile(k_norm_g[0], N_HEADS)[None, :]
    gng = ret_norm_g[0][None, :]
    log_g = jnp.log1p(-jnp.exp2(-5.0 - jnp.arange(N_HEADS, dtype=F32)))

    xp = x_prompt.reshape(B * S, D_MODEL)
    cos_p, sin_p = _rope_tables(jnp.arange(S, dtype=jnp.int32))
    (qpm, kpm, kt, vt, vt_blk, km, ga, gr, qr, kr, vr) = _proj(
        xp, cos_p, sin_p, lng, w_in_bf, qg, kg, tm=2 * MOBA_BLOCK, seq_len=S, feature_major=True)
    ret_p, rfin_p = _retention(qr, kr, vr, gr, jnp.zeros((B, N_PAIRS, LANES, LANES), F32),
                               _ret_tables(log_g, float(RET_CHUNK)), gng, B=B, S=S, T=1024, out_dtype=BF16)
    NI = S // MOBA_BLOCK
    km_pm = km.reshape(B, NI, N_PAIRS, LANES).transpose(0, 2, 1, 3).astype(BF16)

    RS = DB * DS
    RP = -(-RS // MOBA_BLOCK) * MOBA_BLOCK
    xs = x_sample.reshape(RS, D_MODEL)
    pos_s = past_len + jnp.arange(DS, dtype=jnp.int32)
    cos_s, sin_s = _rope_tables(jnp.pad(jnp.tile(pos_s, DB), (0, RP - RS)))
    (qpm_s, _, ka_s, va_s, _, _, ga_s, gr_s, qr_s, kr_s, vr_s) = _proj(
        jnp.pad(xs, ((0, RP - RS), (0, 0))), cos_s, sin_s, lng, w_in_bf, qg, kg,
        tm=RP, seq_len=RP, feature_major=False)
    qpm_s = qpm_s[:, :RS]
    ka_s, va_s, ga_s, gr_s, qr_s, kr_s, vr_s = (a[:RS] for a in (ka_s, va_s, ga_s, gr_s, qr_s, kr_s, vr_s))
    ret_s, rfin_s = _retention(qr_s, kr_s, vr_s, gr_s, _pair_states_to_blockdiag(state_ret[0].astype(F32)),
                               _ret_tables(log_g, float(DS)), gng, B=DB, S=DS, T=DS, out_dtype=F32,
                               nseq=math.gcd(DB, 8) if DS <= RET_CHUNK else 1)
    q_s = qpm_s.transpose(1, 0, 2).reshape(DB, DS, D_GRP).astype(F32)
    t = jnp.arange(DS, dtype=jnp.int32)
    kk = jnp.arange(MOBA_BLOCK, dtype=jnp.int32)
    b_last = _rel_bias_of(t[:, None] + (MOBA_BLOCK - kk)[None, :], rel_bias)
    b_last = b_last.transpose(2, 0, 1).reshape(N_HEADS * DS, MOBA_BLOCK)
    b_new = _rel_bias_of(t[:, None] - t[None, :], rel_bias)
    b_new = b_new.transpose(2, 0, 1).reshape(N_HEADS * DS, DS)
    b_new = jnp.pad(b_new, ((0, 0), (0, LANES - DS)), constant_values=NEG)
    att_p, att_s = _attention(page_table, rel_bias.astype(F32), qpm, kpm, vt_blk, km_pm, ga,
                              _feature_major_pages(cache_k[0]), _feature_major_pages(cache_v[0]),
                              q_s, ka_s.reshape(DB, DS, D_GRP), va_s.reshape(DB, DS, D_GRP),
                              ga_s.reshape(DB, DS, D_GRP), b_last, b_new, B=B, S=S)
    y_p = _out_proj(xp, att_p, ret_p, w_out_bf, tm=1024).reshape(B, S, D_MODEL)
    y_s = _out_proj(xs, att_s.reshape(RS, D_GRP), ret_s, w_out_bf, tm=RS).reshape(DB, DS, D_MODEL)

    sdt = state_ret.dtype
    return (y_p, y_s,
            _token_major_5d(kt, B, S), _token_major_5d(vt, B, S),
            _blockdiag_to_head_states(rfin_p).astype(sdt)[None],
            ka_s.reshape(1, DB, DS, N_HEADS, HEAD_DIM), va_s.reshape(1, DB, DS, N_HEADS, HEAD_DIM),
            _blockdiag_to_head_states(rfin_s).astype(sdt)[None])
```

```python
import functools
import math

import jax
import jax.numpy as jnp
import numpy as np
from jax import lax
from jax.experimental import pallas as pl
from jax.experimental.pallas import tpu as pltpu

D_MODEL = 1024
HEAD_DIM = 64
N_HEADS = 8
D_GRP = N_HEADS * HEAD_DIM
LANES = 128
N_PAIRS = D_GRP // LANES
MOBA_BLOCK = 256
MOBA_TOPK = 3
SCORE_LOOKAHEAD = 5
RET_CHUNK = 128
PAGE_SIZE = 128
N_BUCKETS = 32
MAX_DISTANCE = 128
ROPE_BASE = 10000.0
EPS = 1e-6
NEG = -1e30
BIG = 1e30
LOWEST = -3e38
LOG2E = math.log2(math.e)
Q_SCALE = HEAD_DIM ** -0.5 * LOG2E
V_ROWS = HEAD_DIM + 16
RET_UNROLL = 4
MOBA_UNROLL = 8
SAMPLE_UNROLL = 16
VMEM_LIMIT = 56 * 1024 * 1024

F32 = jnp.float32
BF16 = jnp.bfloat16
NT_DIMS = (((1,), (1,)), ((), ()))


def _silu(g):
    return g * (1.0 / (1.0 + jnp.exp(-g)))


def _head_mean(z):
    lane = lax.broadcasted_iota(jnp.int32, z.shape, 1)
    lo = lane < HEAD_DIM
    s0 = jnp.sum(jnp.where(lo, z, 0.0), axis=-1, keepdims=True)
    s1 = jnp.sum(jnp.where(lo, 0.0, z), axis=-1, keepdims=True)
    return jnp.where(lo, s0, s1) * (1.0 / HEAD_DIM)


def _top_k_blocks(gate, n_valid, blk, axis):
    n_blk = gate.shape[axis]
    g = jnp.where(blk < n_valid, gate, NEG)
    sel = jnp.zeros(gate.shape, F32)
    for _ in range(MOBA_TOPK):
        mx = jnp.max(g, axis=axis, keepdims=True)
        first = jnp.min(jnp.where(g == mx, blk, float(n_blk)), axis=axis, keepdims=True)
        hit = blk == first
        sel = jnp.where(hit, 1.0, sel)
        g = jnp.where(hit, LOWEST, g)
    return jnp.where(blk < n_valid, sel, 0.0)


def _proj_kernel(x_ref, cos_ref, sin_ref, lng_ref, w_ref, qg_ref, kg_ref,
                 qpm_ref, kpm_ref, k_ref, v_ref, vt_ref, km_ref, ga_ref, gr_ref,
                 qr_ref, kr_ref, vr_ref, *, feature_major):
    x = x_ref[...]
    ms = jnp.mean(x * x, axis=-1, keepdims=True)
    h = (x * lax.rsqrt(ms + EPS) * lng_ref[...]).astype(BF16)

    def seg(s):
        return jnp.dot(h, w_ref[:, s * D_GRP:(s + 1) * D_GRP], preferred_element_type=F32)

    def chunks(z):
        return [z[:, c * LANES:(c + 1) * LANES] for c in range(N_PAIRS)]

    qa = seg(0)
    for c, z in enumerate(chunks(qa)):
        g = qg_ref[:, c * LANES:(c + 1) * LANES]
        zn = z * lax.rsqrt(_head_mean(z * z) + EPS) * g
        qpm_ref[c] = (zn * Q_SCALE).astype(BF16)
    ka = seg(1)
    kn = []
    for c, z in enumerate(chunks(ka)):
        g = kg_ref[:, c * LANES:(c + 1) * LANES]
        zn = z * lax.rsqrt(_head_mean(z * z) + EPS) * g
        kpm_ref[c] = zn.astype(BF16)
        kn.append(zn)
    kn = jnp.concatenate(kn, axis=1)
    for b in range(km_ref.shape[1]):
        km_ref[0, b:b + 1, :] = jnp.mean(kn[b * MOBA_BLOCK:(b + 1) * MOBA_BLOCK], axis=0, keepdims=True)
    va = seg(2)
    vat = va.T
    ones = jnp.ones((V_ROWS - HEAD_DIM, MOBA_BLOCK), F32)
    for b in range(vt_ref.shape[0]):
        vb = vat[:, b * MOBA_BLOCK:(b + 1) * MOBA_BLOCK]
        vt_ref[b] = jnp.concatenate(
            [t for h in range(N_HEADS) for t in (vb[h * HEAD_DIM:(h + 1) * HEAD_DIM], ones)],
            axis=0).astype(BF16)
    if feature_major:
        k_ref[0] = kn.T
        v_ref[0] = vat
    else:
        k_ref[...] = kn
        v_ref[...] = va
    ga_ref[...] = seg(3)

    cos = cos_ref[...]
    sin = sin_ref[...]
    lane = lax.broadcasted_iota(jnp.int32, cos.shape, 1)
    first_half = (lane % HEAD_DIM) < (HEAD_DIM // 2)

    def rope(z):
        swapped = jnp.where(first_half,
                            pltpu.roll(z, LANES - HEAD_DIM // 2, 1),
                            pltpu.roll(z, HEAD_DIM // 2, 1))
        return z * cos + swapped * sin

    qr = seg(4)
    for c, z in enumerate(chunks(qr)):
        qr_ref[:, c * LANES:(c + 1) * LANES] = rope(z)
    kr = seg(5)
    for c, z in enumerate(chunks(kr)):
        kr_ref[:, c * LANES:(c + 1) * LANES] = rope(z) * (HEAD_DIM ** -0.5)
    vr_ref[...] = seg(6)
    gr_ref[...] = seg(7)


def _proj(x, cos, sin, ln_g, w_bf, qg, kg, *, tm, seq_len, feature_major):
    R = x.shape[0]
    assert tm % MOBA_BLOCK == 0 and R % tm == 0 and cos.shape[0] % tm == 0 and seq_len % tm == 0
    n_pos = cos.shape[0] // tm
    nb = tm // MOBA_BLOCK
    nt = seq_len // tm
    row = lambda i: (i, 0)
    full = lambda i: (0, 0)
    wide = pl.BlockSpec((tm, D_GRP), row)
    wide_shape = jax.ShapeDtypeStruct((R, D_GRP), F32)
    if feature_major:
        kv_shape = jax.ShapeDtypeStruct((R // seq_len, D_GRP, seq_len), F32)
        kv_spec = pl.BlockSpec((1, D_GRP, tm), lambda i: (i // nt, 0, i % nt))
    else:
        kv_shape, kv_spec = wide_shape, wide
    out_shape = (
        jax.ShapeDtypeStruct((N_PAIRS, R, LANES), BF16),
        jax.ShapeDtypeStruct((N_PAIRS, R, LANES), BF16),
        kv_shape,
        kv_shape,
        jax.ShapeDtypeStruct((R // MOBA_BLOCK, N_HEADS * V_ROWS, MOBA_BLOCK), BF16),
        jax.ShapeDtypeStruct((R // tm, nb, D_GRP), F32),
        wide_shape,
        wide_shape,
        wide_shape,
        wide_shape,
        wide_shape,
    )
    pm = pl.BlockSpec((N_PAIRS, tm, LANES), lambda i: (0, i, 0))
    out_specs = (
        pm, pm, kv_spec, kv_spec,
        pl.BlockSpec((nb, N_HEADS * V_ROWS, MOBA_BLOCK), lambda i: (i, 0, 0)),
        pl.BlockSpec((1, nb, D_GRP), lambda i: (i, 0, 0)),
        wide, wide, wide, wide, wide,
    )
    in_specs = [
        pl.BlockSpec((tm, D_MODEL), row),
        pl.BlockSpec((tm, LANES), lambda i: (i % n_pos, 0)),
        pl.BlockSpec((tm, LANES), lambda i: (i % n_pos, 0)),
        pl.BlockSpec((1, D_MODEL), full),
        pl.BlockSpec((D_MODEL, 8 * D_GRP), full),
        pl.BlockSpec((1, D_GRP), full),
        pl.BlockSpec((1, D_GRP), full),
    ]
    return pl.pallas_call(
        functools.partial(_proj_kernel, feature_major=feature_major),
        out_shape=out_shape,
        grid=(R // tm,),
        in_specs=in_specs,
        out_specs=out_specs,
        compiler_params=pltpu.CompilerParams(
            dimension_semantics=("arbitrary",), vmem_limit_bytes=VMEM_LIMIT),
        name="proj",
    )(x, cos, sin, ln_g, w_bf, qg, kg)


def _ret_kernel(q_ref, k_ref, v_ref, g_ref, r0_ref, d2_ref, qd_ref, kd_ref, gl_ref, gng_ref,
                out_ref, rfin_ref, r_scr, *, T, L, NSEQ):
    t = pl.program_id(1)
    batched = T <= L
    assert batched or NSEQ == 1

    if not batched:
        @pl.when(t == 0)
        def _():
            r_scr[...] = r0_ref[0]

    lane = lax.broadcasted_iota(jnp.int32, (L, LANES), 1)
    lo = lane < HEAD_DIM
    ri = lax.broadcasted_iota(jnp.int32, (LANES, LANES), 0)
    ci = lax.broadcasted_iota(jnp.int32, (LANES, LANES), 1)
    same_head = (ri < HEAD_DIM) == (ci < HEAD_DIM)
    rows_c = min(T, L)

    def load(ref, rows, cols):
        z = ref[rows, cols]
        if rows_c < L:
            z = jnp.concatenate([z, jnp.zeros((L - rows_c, LANES), F32)], axis=0)
        return z

    n_chunks = max(T // L, 1)
    per_trip = NSEQ if batched else math.gcd(n_chunks, RET_UNROLL)

    def trip(tt, carry):
        staged = []
        for r in range(per_trip):
            if batched:
                rows = pl.ds(r * rows_c, rows_c)
            else:
                rows = pl.ds(pl.multiple_of((tt * per_trip + r) * L, L), L)
            for c in range(N_PAIRS):
                cols = slice(c * LANES, (c + 1) * LANES)
                q = load(q_ref, rows, cols)
                k = load(k_ref, rows, cols)
                vc = load(v_ref, rows, cols).astype(BF16)
                qs = jnp.concatenate([jnp.where(lo, q, 0.0), jnp.where(lo, 0.0, q)], axis=0).astype(BF16)
                s = lax.dot_general(qs, k.astype(BF16), NT_DIMS, preferred_element_type=F32)
                kdk = (k * kd_ref[:, cols]).T.astype(BF16)
                upd = jnp.dot(kdk, vc, preferred_element_type=F32)
                staged.append([rows, c, cols, s, upd, vc, q.astype(BF16)])
        for c in range(N_PAIRS):
            R = None if batched else r_scr[c]
            for r in range(per_trip):
                item = staged[r * N_PAIRS + c]
                if batched:
                    R = r0_ref[r, c]
                item.append(jnp.dot(item[6], R.astype(BF16), preferred_element_type=F32))
                R = gl_ref[:, item[2]] * R + jnp.where(same_head, item[4], 0.0)
                if batched:
                    rfin_ref[r, c] = R
            if not batched:
                r_scr[c] = R
        for rows, c, cols, s, _, vc, _, cross in staged:
            inner2 = jnp.dot((s * d2_ref[c]).astype(BF16), vc, preferred_element_type=F32)
            o = jnp.where(lo, inner2[:L], inner2[L:]) + cross * qd_ref[:, cols]
            mu = _head_mean(o)
            d = o - mu
            var = _head_mean(d * d)
            y = d * lax.rsqrt(var + EPS) * gng_ref[:, cols]
            res = (y * _silu(load(g_ref, rows, cols))).astype(out_ref.dtype)
            out_ref[rows, cols] = res[:rows_c]
        return carry

    if batched:
        trip(0, 0)
    else:
        lax.fori_loop(0, n_chunks // per_trip, trip, 0)

        @pl.when(t == pl.num_programs(1) - 1)
        def _():
            rfin_ref[0] = r_scr[...]


def _retention(q, k, v, g, r0, tabs, gn_g, *, B, S, T, out_dtype, nseq=1):
    d2, qd, kd, gl = tabs
    L = RET_CHUNK
    nt = S // T
    assert B % nseq == 0 and (nseq == 1 or (nt == 1 and T <= L))
    tok = pl.BlockSpec((nseq * T, D_GRP), lambda b, t: (b * nt + t, 0))
    state = pl.BlockSpec((nseq, N_PAIRS, LANES, LANES), lambda b, t: (b, 0, 0, 0))
    const2 = lambda b, t: (0, 0)
    return pl.pallas_call(
        functools.partial(_ret_kernel, T=T, L=L, NSEQ=nseq),
        out_shape=(jax.ShapeDtypeStruct((B * S, D_GRP), out_dtype),
                   jax.ShapeDtypeStruct((B, N_PAIRS, LANES, LANES), F32)),
        grid=(B // nseq, nt),
        in_specs=[tok, tok, tok, tok, state,
                  pl.BlockSpec((N_PAIRS, 2 * L, L), lambda b, t: (0, 0, 0)),
                  pl.BlockSpec((L, D_GRP), const2), pl.BlockSpec((L, D_GRP), const2),
                  pl.BlockSpec((1, D_GRP), const2), pl.BlockSpec((1, D_GRP), const2)],
        out_specs=(tok, state),
        scratch_shapes=[pltpu.VMEM((N_PAIRS, LANES, LANES), F32)],
        compiler_params=pltpu.CompilerParams(
            dimension_semantics=("arbitrary", "arbitrary"), vmem_limit_bytes=VMEM_LIMIT),
        name="retention",
    )(q, k, v, g, r0, d2, qd, kd, gl, gn_g)


def _ret_tables(log_g, l_true):
    L = RET_CHUNK
    n = jnp.arange(L, dtype=F32)
    diff = n[:, None] - n[None, :]
    decay = jnp.where(diff >= 0, jnp.exp(log_g[:, None, None] * jnp.maximum(diff, 0.0)), 0.0)
    d2 = decay.reshape(N_PAIRS, 2 * L, L)
    per_lane = lambda t: jnp.repeat(t, HEAD_DIM, axis=1)
    qd = per_lane(jnp.exp(log_g[None, :] * (n[:, None] + 1.0)))
    kd = per_lane(jnp.exp(log_g[None, :] * (l_true - 1.0 - n)[:, None]))
    gl = per_lane(jnp.exp(log_g * l_true)[None, :])
    return d2, qd, kd, gl


def _t5_bucket_starts(max_dist):
    d = np.arange(max_dist)
    max_exact = N_BUCKETS // 2
    scaled = (np.log(np.maximum(d, 1).astype(np.float32) / np.float32(max_exact))
              / np.float32(math.log(MAX_DISTANCE / max_exact)) * np.float32(N_BUCKETS - max_exact))
    bucket = np.where(d < max_exact, d, np.minimum(max_exact + scaled.astype(np.int32), N_BUCKETS - 1))
    assert np.all(np.diff(bucket) >= 0)
    return [(int(b), int(d[bucket == b][0])) for b in np.unique(bucket)]


def _bias_tables(rb_ref, bo_ref, bp_ref):
    shape = (MOBA_BLOCK, MOBA_BLOCK)
    d_own = lax.broadcasted_iota(jnp.int32, shape, 1) - lax.broadcasted_iota(jnp.int32, shape, 0)
    starts = _t5_bucket_starts(2 * MOBA_BLOCK)
    assert starts[0] == (0, 0) and starts[-1][0] == N_BUCKETS - 1

    def head(h, carry):
        far = rb_ref[N_BUCKETS - 1, h]
        for ref, d in ((bo_ref, d_own), (bp_ref, d_own + MOBA_BLOCK)):
            val = jnp.full(shape, (rb_ref[0, h] - far) * LOG2E, F32)
            for bkt, lo in starts[1:]:
                val = jnp.where(d >= lo, (rb_ref[bkt, h] - far) * LOG2E, val)
            ref[h] = jnp.where(d >= 0, val, NEG)
        return carry

    lax.fori_loop(0, N_HEADS, head, 0)


def _moba_step(rb_ref, q_ref, k_ref, vt_ref, km_ref, ga_ref, out_ref,
               qm_ref, acc_ref, m_ref, sel_ref, ot_ref, bo_ref, bp_ref, s_scr, p_scr, alpha_scr, *, NI):
    i = pl.program_id(1)

    @pl.when((pl.program_id(0) == 0) & (i == 0))
    def _():
        _bias_tables(rb_ref, bo_ref, bp_ref)

    lane = lax.broadcasted_iota(jnp.int32, (MOBA_BLOCK, LANES), 1)
    blk = lax.broadcasted_iota(jnp.int32, (NI, MOBA_BLOCK), 0).astype(F32)
    i_f = i.astype(F32)
    for c in range(N_PAIRS):
        qc = q_ref[c].astype(F32)
        qm_ref[2 * c] = jnp.where(lane < HEAD_DIM, qc, 0.0).T.astype(BF16)
        qm_ref[2 * c + 1] = jnp.where(lane < HEAD_DIM, 0.0, qc).T.astype(BF16)

    def scores(j, h, bias_ref):
        kj = k_ref[h // 2, pl.ds(pl.multiple_of(j * MOBA_BLOCK, MOBA_BLOCK), MOBA_BLOCK), :]
        s = jnp.dot(kj, qm_ref[h], preferred_element_type=F32)
        return s if bias_ref is None else s + bias_ref[h]

    def values(j, h):
        return vt_ref[j, h * V_ROWS:(h + 1) * V_ROWS, :]

    for h in range(N_HEADS):
        gate = jnp.dot(km_ref[0, h // 2], qm_ref[h], preferred_element_type=F32)
        sel_ref[h] = _top_k_blocks(gate, i_f, blk, 0)

    d_far = _t5_bucket_starts(2 * MOBA_BLOCK)[-1][1]
    k0 = (MOBA_BLOCK - d_far + 1) // 8 * 8
    assert d_far - 1 <= LANES
    LA = SCORE_LOOKAHEAD

    LAST = N_HEADS - 1

    def finish_last(j_prev):
        acc_ref[LAST] = alpha_scr[...] * acc_ref[LAST] + jnp.dot(values(j_prev, LAST), p_scr[...],
                                                                preferred_element_type=F32)

    def key_block(j, j_prev, j_next, own):
        local = {h: scores(j, h, bo_ref) for h in range(LA)} if own else {}
        near = None if own else (j == i - 1).astype(F32)
        if not own:
            finish_last(j_prev)
        for h in range(N_HEADS):
            t = h + LA
            if t < N_HEADS:
                local[t] = scores(j, t, bo_ref if own else None)
            else:
                s_scr[t - N_HEADS] = scores(j_next, t - N_HEADS, None)
            s = local.pop(h) if h in local else s_scr[h]
            if own:
                m_new = jnp.max(s, axis=0, keepdims=True)
                alpha = None
                p = jnp.exp2(s - m_new)
            else:
                corner = s[k0:, :LANES] + near * bp_ref[h, k0:, :LANES]
                s = jnp.concatenate([s[:k0], jnp.concatenate([corner, s[k0:, LANES:]], axis=1)], axis=0)
                selj = sel_ref[h, pl.ds(j, 1), :] > 0.0
                m_old = m_ref[h]
                m_new = jnp.where(selj, jnp.maximum(m_old, jnp.max(s, axis=0, keepdims=True)), m_old)
                alpha = jnp.exp2(m_old - m_new)
                p = jnp.exp2(s - jnp.where(selj, m_new, BIG))
            m_ref[h] = m_new
            pb = p.astype(BF16)
            if h == LAST:
                p_scr[...] = pb
                alpha_scr[...] = jnp.zeros_like(m_new) if own else alpha
                if own:
                    acc_ref[h] = jnp.zeros(acc_ref.shape[1:], F32)
            elif own:
                acc_ref[h] = jnp.dot(values(j, h), pb, preferred_element_type=F32)
            else:
                acc_ref[h] = alpha * acc_ref[h] + jnp.dot(values(j, h), pb, preferred_element_type=F32)

    key_block(i, None, 0, True)

    def past_blocks(n, first):
        def trip(jj, carry):
            j0 = first + n * jj
            for r in range(n):
                j = j0 + r
                key_block(j, jnp.where(j == 0, i, j - 1) if r == 0 else j - 1, j + 1, False)
            return carry
        return trip

    done = 0
    n = MOBA_UNROLL
    while n >= 1:
        trips = (i - done) // n
        lax.fori_loop(0, trips, past_blocks(n, done), 0)
        done = done + trips * n
        n //= 2
    finish_last(jnp.maximum(i - 1, 0))
    for h in range(N_HEADS):
        o = acc_ref[h]
        ot_ref[h * HEAD_DIM:(h + 1) * HEAD_DIM, :] = o[:HEAD_DIM] / o[HEAD_DIM:HEAD_DIM + 1]
    out_ref[...] = (ot_ref[...].T * _silu(ga_ref[...])).astype(out_ref.dtype)


def _sample_start(g, pt_ref, kc_hbm, vc_hbm, pages_ref, sem, *, HP):
    sb = g // 4
    ph = g % 4
    base = (ph % 2) * HP

    def copies(cache_hbm):
        def body(tt, carry):
            for e in range(2):
                t = 2 * tt + e
                pltpu.make_async_copy(cache_hbm.at[pt_ref[sb, base + t]], pages_ref.at[t],
                                      sem.at[0]).start(priority=e)
            return carry

        lax.fori_loop(0, HP // 2, body, 0)

    @pl.when(ph < 2)
    def _():
        copies(kc_hbm)

    @pl.when(ph >= 2)
    def _():
        copies(vc_hbm)


def _sample_stage(g, kc_hbm, q_ref, kn_ref, vn_ref, ga_ref, blast_ref, bnew_ref, out_ref,
                  s_ref, gate_ref, bmax_ref, meff_ref, m_ref, l_ref, acc_ref, qbd_ref, pages_ref, sem,
                  *, HP, NB, TQ):
    ph = g % 4
    HQ = N_HEADS * TQ
    bpst = HP * PAGE_SIZE // MOBA_BLOCK
    ub = math.gcd(bpst, SAMPLE_UNROLL)
    assert ub % 2 == 0
    lane_q = lax.broadcasted_iota(jnp.int32, (HQ, LANES), 1)

    def wait_one(t, carry):
        pltpu.make_async_copy(kc_hbm.at[0], pages_ref.at[t], sem.at[0]).wait()
        return carry

    lax.fori_loop(0, HP, wait_one, 0)

    def block(u):
        return jnp.concatenate([pages_ref[2 * u], pages_ref[2 * u + 1]], axis=1)

    @pl.when(ph == 0)
    def _():
        qt = jnp.concatenate([q_ref[0]] * N_HEADS, axis=0)
        r = lax.broadcasted_iota(jnp.int32, (HQ, D_GRP), 0)
        l = lax.broadcasted_iota(jnp.int32, (HQ, D_GRP), 1)
        qbd_ref[...] = jnp.where(r // TQ == l // HEAD_DIM, qt, 0.0).astype(BF16)
        gate_ref[...] = jnp.zeros(gate_ref.shape, F32)
        bmax_ref[...] = jnp.zeros(bmax_ref.shape, F32)

    @pl.when(ph < 2)
    def _():
        def body(uu, carry):
            gate = gate_ref[...]
            bmax = bmax_ref[...]
            for r in range(0, ub, 2):
                u = ub * uu + r
                kt2 = jnp.concatenate([block(u), block(u + 1)], axis=1).astype(BF16)
                sc2 = jnp.dot(qbd_ref[...], kt2, preferred_element_type=F32)
                for e in range(2):
                    jb = ph * bpst + u + e
                    sc = sc2[:, e * MOBA_BLOCK:(e + 1) * MOBA_BLOCK]
                    gate = jnp.where(lane_q == jb, jnp.sum(sc, axis=1, keepdims=True) * (1.0 / MOBA_BLOCK), gate)
                    sc = sc + jnp.where(jb == NB - 1, blast_ref[...], 0.0)
                    s_ref[jb] = sc
                    bmax = jnp.where(lane_q == jb, jnp.max(sc, axis=1, keepdims=True), bmax)
            gate_ref[...] = gate
            bmax_ref[...] = bmax
            return carry

        lax.fori_loop(0, bpst // ub, body, 0)

    @pl.when(ph == 1)
    def _():
        sel = _top_k_blocks(gate_ref[...], float(NB), lane_q.astype(F32), 1)
        pad = jnp.zeros((LANES - TQ, D_GRP), F32)
        kn = jnp.concatenate([kn_ref[0], pad], axis=0).astype(BF16)
        vn = jnp.concatenate([vn_ref[0], pad], axis=0).astype(BF16)
        sn = lax.dot_general(qbd_ref[...], kn, NT_DIMS, preferred_element_type=F32) + bnew_ref[...]
        m = jnp.maximum(jnp.max(jnp.where(sel > 0.0, bmax_ref[...], NEG), axis=1, keepdims=True),
                        jnp.max(sn, axis=1, keepdims=True))
        m_ref[...] = m
        meff_ref[...] = jnp.where(sel > 0.0, m, BIG)
        pn = jnp.exp2(sn - m)
        l_ref[...] = jnp.sum(pn, axis=1, keepdims=True)
        acc_ref[...] = jnp.dot(pn.astype(BF16), vn, preferred_element_type=F32)

    @pl.when(ph >= 2)
    def _():
        def body(uu, carry):
            meff_all = meff_ref[...]
            l = l_ref[...]
            acc = acc_ref[...]
            for r in range(ub):
                u = ub * uu + r
                jb = (ph - 2) * bpst + u
                meff = jnp.sum(jnp.where(lane_q == jb, meff_all, 0.0), axis=1, keepdims=True)
                p = jnp.exp2(s_ref[jb] - meff)
                l = l + jnp.sum(p, axis=1, keepdims=True)
                acc = acc + lax.dot_general(p.astype(BF16), block(u).astype(BF16), NT_DIMS,
                                            preferred_element_type=F32)
            l_ref[...] = l
            acc_ref[...] = acc
            return carry

        lax.fori_loop(0, bpst // ub, body, 0)

    @pl.when(ph == 3)
    def _():
        o_bd = acc_ref[...] / l_ref[...]
        lane = lax.broadcasted_iota(jnp.int32, (TQ, D_GRP), 1)
        o = jnp.zeros((TQ, D_GRP), F32)
        for h in range(N_HEADS):
            o = jnp.where(lane // HEAD_DIM == h, o_bd[h * TQ:(h + 1) * TQ], o)
        out_ref[0] = o * _silu(ga_ref[0])


N_MOBA_IN, N_MOBA_SCRATCH = 6, 10


def _attention_kernel(pt_ref, *refs, NI, HP, NB, TQ):
    moba_in = refs[:N_MOBA_IN]
    kc_hbm, vc_hbm, qs_ref, kn_ref, vn_ref, gas_ref, blast_ref, bnew_ref, out_ref, outs_ref = refs[N_MOBA_IN:N_MOBA_IN + 10]
    scratch = refs[N_MOBA_IN + 10:]
    moba_scratch = scratch[:N_MOBA_SCRATCH]
    sample_scratch = scratch[N_MOBA_SCRATCH:]
    pages_ref, sem = sample_scratch[-2:]
    g = pl.program_id(0) * NI + pl.program_id(1)
    _sample_start(g, pt_ref, kc_hbm, vc_hbm, pages_ref, sem, HP=HP)
    _moba_step(*moba_in, out_ref, *moba_scratch, NI=NI)
    _sample_stage(g, kc_hbm, qs_ref, kn_ref, vn_ref, gas_ref, blast_ref, bnew_ref, outs_ref,
                  *sample_scratch, HP=HP, NB=NB, TQ=TQ)


def _attention(page_table, rel_bias, qpm, kpm, vt, km_pm, ga, cache_kt, cache_vt,
               q_s, k_new, v_new, ga_s, bias_last, bias_new, *, B, S):
    NI = S // MOBA_BLOCK
    DB, n_pages = page_table.shape
    TQ = q_s.shape[1]
    HQ = N_HEADS * TQ
    HP = n_pages // 2
    NB = n_pages * PAGE_SIZE // MOBA_BLOCK
    assert B * NI == 4 * DB and n_pages % 4 == 0
    assert NB <= LANES and HQ <= LANES and TQ <= LANES
    once = pl.Buffered(1)
    per_sb = pl.BlockSpec((1, TQ, D_GRP), lambda b, i, pt: ((b * NI + i) // 4, 0, 0))
    rows = pl.BlockSpec((MOBA_BLOCK, D_GRP), lambda b, i, pt: (b * NI + i, 0))
    in_specs = [
        pl.BlockSpec(memory_space=pltpu.SMEM),
        pl.BlockSpec((N_PAIRS, MOBA_BLOCK, LANES), lambda b, i, pt: (0, b * NI + i, 0)),
        pl.BlockSpec((N_PAIRS, S, LANES), lambda b, i, pt: (0, b, 0), pipeline_mode=once),
        pl.BlockSpec((NI, N_HEADS * V_ROWS, MOBA_BLOCK), lambda b, i, pt: (b, 0, 0), pipeline_mode=once),
        pl.BlockSpec((1, N_PAIRS, NI, LANES), lambda b, i, pt: (b, 0, 0, 0)),
        rows,
        pl.BlockSpec(memory_space=pl.ANY), pl.BlockSpec(memory_space=pl.ANY),
        per_sb, per_sb, per_sb, per_sb,
        pl.BlockSpec((HQ, MOBA_BLOCK), lambda b, i, pt: (0, 0)),
        pl.BlockSpec((HQ, LANES), lambda b, i, pt: (0, 0)),
    ]
    scratch_shapes = [
        pltpu.VMEM((N_HEADS, LANES, MOBA_BLOCK), BF16),
        pltpu.VMEM((N_HEADS, V_ROWS, MOBA_BLOCK), F32),
        pltpu.VMEM((N_HEADS, 1, MOBA_BLOCK), F32),
        pltpu.VMEM((N_HEADS, NI, MOBA_BLOCK), F32),
        pltpu.VMEM((D_GRP, MOBA_BLOCK), F32),
        pltpu.VMEM((N_HEADS, MOBA_BLOCK, MOBA_BLOCK), F32),
        pltpu.VMEM((N_HEADS, MOBA_BLOCK, MOBA_BLOCK), F32),
        pltpu.VMEM((SCORE_LOOKAHEAD, MOBA_BLOCK, MOBA_BLOCK), F32),
        pltpu.VMEM((MOBA_BLOCK, MOBA_BLOCK), BF16),
        pltpu.VMEM((1, MOBA_BLOCK), F32),
        pltpu.VMEM((NB, HQ, MOBA_BLOCK), F32),
        pltpu.VMEM((HQ, LANES), F32),
        pltpu.VMEM((HQ, LANES), F32),
        pltpu.VMEM((HQ, LANES), F32),
        pltpu.VMEM((HQ, 1), F32),
        pltpu.VMEM((HQ, 1), F32),
        pltpu.VMEM((HQ, D_GRP), F32),
        pltpu.VMEM((HQ, D_GRP), BF16),
        pltpu.VMEM((HP, D_GRP, PAGE_SIZE), F32),
        pltpu.SemaphoreType.DMA((1,)),
    ]
    assert len(scratch_shapes) == N_MOBA_SCRATCH + 10
    grid_spec = pltpu.PrefetchScalarGridSpec(
        num_scalar_prefetch=1, grid=(B, NI), in_specs=in_specs,
        out_specs=(rows, per_sb), scratch_shapes=scratch_shapes)
    return pl.pallas_call(
        functools.partial(_attention_kernel, NI=NI, HP=HP, NB=NB, TQ=TQ),
        out_shape=(jax.ShapeDtypeStruct((B * S, D_GRP), BF16),
                   jax.ShapeDtypeStruct((DB, TQ, D_GRP), F32)),
        grid_spec=grid_spec,
        compiler_params=pltpu.CompilerParams(
            dimension_semantics=("arbitrary", "arbitrary"), vmem_limit_bytes=VMEM_LIMIT),
        name="attention",
    )(page_table, rel_bias, qpm, kpm, vt, km_pm, ga, cache_kt, cache_vt,
      q_s, k_new, v_new, ga_s, bias_last, bias_new)


def _out_kernel(x_ref, a_ref, r_ref, w_ref, y_ref):
    cat = jnp.concatenate([a_ref[...].astype(BF16), r_ref[...].astype(BF16)], axis=1)
    y_ref[...] = x_ref[...] + jnp.dot(cat, w_ref[...], preferred_element_type=F32)


def _out_proj(x, att, ret, w_bf, *, tm):
    R = x.shape[0]
    row = lambda i: (i, 0)
    return pl.pallas_call(
        _out_kernel,
        out_shape=jax.ShapeDtypeStruct((R, D_MODEL), F32),
        grid=(R // tm,),
        in_specs=[pl.BlockSpec((tm, D_MODEL), row), pl.BlockSpec((tm, D_GRP), row),
                  pl.BlockSpec((tm, D_GRP), row), pl.BlockSpec((2 * D_GRP, D_MODEL), lambda i: (0, 0))],
        out_specs=pl.BlockSpec((tm, D_MODEL), row),
        compiler_params=pltpu.CompilerParams(
            dimension_semantics=("arbitrary",), vmem_limit_bytes=VMEM_LIMIT),
        name="out_proj",
    )(x, att, ret, w_bf)


def _rope_tables(pos):
    half = HEAD_DIM // 2
    inv = ROPE_BASE ** (-jnp.arange(half, dtype=F32) / half)
    ang = pos.astype(F32)[:, None] * inv[None, :]
    cos, sin = jnp.cos(ang), jnp.sin(ang)
    return (jnp.tile(jnp.concatenate([cos, cos], axis=1), (1, 2)),
            jnp.tile(jnp.concatenate([-sin, sin], axis=1), (1, 2)))


def _t5_bucket(dist):
    max_exact = N_BUCKETS // 2
    d = jnp.maximum(dist, 1).astype(F32)
    large = max_exact + (jnp.log(d / max_exact) / math.log(MAX_DISTANCE / max_exact)
                         * (N_BUCKETS - max_exact)).astype(jnp.int32)
    large = jnp.minimum(large, N_BUCKETS - 1)
    return jnp.where(dist < max_exact, dist, large)


def _rel_bias_of(dist, rel_bias):
    rb = rel_bias.astype(F32)
    b = (rb[_t5_bucket(jnp.maximum(dist, 0))] - rb[N_BUCKETS - 1]) * LOG2E
    return jnp.where((dist >= 0)[..., None], b, NEG)


def _pair_states_to_blockdiag(state):
    B = state.shape[0]
    s5 = state.reshape(B, N_PAIRS, 2, HEAD_DIM, HEAD_DIM)
    top = jnp.pad(s5[:, :, 0], ((0, 0), (0, 0), (0, 0), (0, HEAD_DIM)))
    bot = jnp.pad(s5[:, :, 1], ((0, 0), (0, 0), (0, 0), (HEAD_DIM, 0)))
    return jnp.concatenate([top, bot], axis=2)


def _blockdiag_to_head_states(bd):
    B = bd.shape[0]
    d = jnp.stack([bd[:, :, :HEAD_DIM, :HEAD_DIM], bd[:, :, HEAD_DIM:, HEAD_DIM:]], axis=2)
    return d.reshape(B, N_HEADS, HEAD_DIM, HEAD_DIM)


def _feature_major_pages(cache):
    n_phys = cache.shape[0]
    return jnp.transpose(cache, (0, 2, 3, 1)).reshape(n_phys, D_GRP, PAGE_SIZE)


def _token_major_5d(xt, B, S):
    return jnp.transpose(xt.reshape(B, N_HEADS, HEAD_DIM, S), (0, 3, 1, 2))[None]


def kernel(x_prompt, x_sample, cache_k, cache_v, state_ret, page_table, ln_g, w_in,
           q_norm_g, k_norm_g, rel_bias, ret_norm_g, w_out):
    depth = ln_g.shape[0]
    assert depth == 1, "single-layer stack"
    B, S, _ = x_prompt.shape
    DB, DS, _ = x_sample.shape
    n_pages = page_table.shape[1]
    past_len = n_pages * PAGE_SIZE
    assert past_len % MOBA_BLOCK == 0 and DS <= MOBA_BLOCK

    w_in_bf = w_in[0].astype(BF16)
    w_out_bf = w_out[0].astype(BF16)
    lng = ln_g[0][None, :]
    qg = jnp.tile(q_norm_g[0], N_HEADS)[None, :]
    kg = jnp.tile(k_norm_g[0], N_HEADS)[None, :]
    gng = ret_norm_g[0][None, :]
    log_g = jnp.log1p(-jnp.exp2(-5.0 - jnp.arange(N_HEADS, dtype=F32)))

    xp = x_prompt.reshape(B * S, D_MODEL)
    cos_p, sin_p = _rope_tables(jnp.arange(S, dtype=jnp.int32))
    (qpm, kpm, kt, vt, vt_blk, km, ga, gr, qr, kr, vr) = _proj(
        xp, cos_p, sin_p, lng, w_in_bf, qg, kg, tm=2 * MOBA_BLOCK, seq_len=S, feature_major=True)
    ret_p, rfin_p = _retention(qr, kr, vr, gr, jnp.zeros((B, N_PAIRS, LANES, LANES), F32),
                               _ret_tables(log_g, float(RET_CHUNK)), gng, B=B, S=S, T=1024, out_dtype=BF16)
    NI = S // MOBA_BLOCK
    km_pm = km.reshape(B, NI, N_PAIRS, LANES).transpose(0, 2, 1, 3).astype(BF16)

    RS = DB * DS
    RP = -(-RS // MOBA_BLOCK) * MOBA_BLOCK
    xs = x_sample.reshape(RS, D_MODEL)
    pos_s = past_len + jnp.arange(DS, dtype=jnp.int32)
    cos_s, sin_s = _rope_tables(jnp.pad(jnp.tile(pos_s, DB), (0, RP - RS)))
    (qpm_s, _, ka_s, va_s, _, _, ga_s, gr_s, qr_s, kr_s, vr_s) = _proj(
        jnp.pad(xs, ((0, RP - RS), (0, 0))), cos_s, sin_s, lng, w_in_bf, qg, kg,
        tm=RP, seq_len=RP, feature_major=False)
    qpm_s = qpm_s[:, :RS]
    ka_s, va_s, ga_s, gr_s, qr_s, kr_s, vr_s = (a[:RS] for a in (ka_s, va_s, ga_s, gr_s, qr_s, kr_s, vr_s))
    ret_s, rfin_s = _retention(qr_s, kr_s, vr_s, gr_s, _pair_states_to_blockdiag(state_ret[0].astype(F32)),
                               _ret_tables(log_g, float(DS)), gng, B=DB, S=DS, T=DS, out_dtype=F32,
                               nseq=math.gcd(DB, 8) if DS <= RET_CHUNK else 1)
    q_s = qpm_s.transpose(1, 0, 2).reshape(DB, DS, D_GRP).astype(F32)
    t = jnp.arange(DS, dtype=jnp.int32)
    kk = jnp.arange(MOBA_BLOCK, dtype=jnp.int32)
    b_last = _rel_bias_of(t[:, None] + (MOBA_BLOCK - kk)[None, :], rel_bias)
    b_last = b_last.transpose(2, 0, 1).reshape(N_HEADS * DS, MOBA_BLOCK)
    b_new = _rel_bias_of(t[:, None] - t[None, :], rel_bias)
    b_new = b_new.transpose(2, 0, 1).reshape(N_HEADS * DS, DS)
    b_new = jnp.pad(b_new, ((0, 0), (0, LANES - DS)), constant_values=NEG)
    att_p, att_s = _attention(page_table, rel_bias.astype(F32), qpm, kpm, vt_blk, km_pm, ga,
                              _feature_major_pages(cache_k[0]), _feature_major_pages(cache_v[0]),
                              q_s, ka_s.reshape(DB, DS, D_GRP), va_s.reshape(DB, DS, D_GRP),
                              ga_s.reshape(DB, DS, D_GRP), b_last, b_new, B=B, S=S)
    y_p = _out_proj(xp, att_p, ret_p, w_out_bf, tm=1024).reshape(B, S, D_MODEL)
    y_s = _out_proj(xs, att_s.reshape(RS, D_GRP), ret_s, w_out_bf, tm=RS).reshape(DB, DS, D_MODEL)

    sdt = state_ret.dtype
    return (y_p, y_s,
            _token_major_5d(kt, B, S), _token_major_5d(vt, B, S),
            _blockdiag_to_head_states(rfin_p).astype(sdt)[None],
            ka_s.reshape(1, DB, DS, N_HEADS, HEAD_DIM), va_s.reshape(1, DB, DS, N_HEADS, HEAD_DIM),
            _blockdiag_to_head_states(rfin_s).astype(sdt)[None])
```
